```python
import jax, jax.numpy as jnp
from jax import lax
import numpy as np

D_MODEL = 4096
BATCH = 1
SEQ = 8192
DEPTH = 4

HEAD_DIM = 128
N_FOX_HEADS = 16
N_DSA_HEADS = 16
N_IDX_HEADS = 32
IDX_DIM = 64
INDEX_TOPK_MAX = 256
D_FF = 3584
PLE_DIM = 256
ROPE_THETA = 500000.0
ROT_FRAC_DIV = 4
Q_BLOCK = 128
NORM_EPS = 1e-6
NEG_BIG = -1e30

FOX_W = N_FOX_HEADS * HEAD_DIM
DSA_W = N_DSA_HEADS * HEAD_DIM
IDX_Q_W = N_IDX_HEADS * IDX_DIM
COL_SIZES = (FOX_W, FOX_W, FOX_W, N_FOX_HEADS,
             DSA_W, DSA_W, DSA_W,
             IDX_Q_W, IDX_DIM, N_IDX_HEADS,
             D_MODEL, D_MODEL)
N_IN = FOX_W * 3 + N_FOX_HEADS + DSA_W * 3 + IDX_Q_W + IDX_DIM + N_IDX_HEADS + 2 * D_MODEL

kernel_name = 'hybrid_fox_dsa_macaron_ple'


def rmsnorm(x, g):
    xf = x.astype(jnp.float32)
    y = xf * lax.rsqrt(jnp.mean(xf * xf, axis=-1, keepdims=True) + NORM_EPS)
    return (y * g.astype(jnp.float32)).astype(x.dtype)


def swiglu(x, w_gate, w_up, w_down):
    return (jax.nn.silu(x @ w_gate) * (x @ w_up)) @ w_down


def rope_partial(x, pos):
    d = x.shape[-1]
    r = d // ROT_FRAC_DIV
    half = r // 2
    inv = jnp.power(jnp.float32(ROPE_THETA), -2.0 * jnp.arange(half, dtype=jnp.float32) / r)
    ang = pos.astype(jnp.float32)[:, None] * inv[None, :]
    cos = jnp.cos(ang)[None, :, None, :]
    sin = jnp.sin(ang)[None, :, None, :]
    xf = x.astype(jnp.float32)
    x1 = xf[..., :half]
    x2 = xf[..., half:r]
    out = jnp.concatenate([x1 * cos - x2 * sin, x2 * cos + x1 * sin, xf[..., r:]], axis=-1)
    return out.astype(x.dtype)


def split_cols(z):
    parts = []
    off = 0
    for n in COL_SIZES:
        parts.append(z[..., off:off + n])
        off += n
    return parts


def fox_attention(q, k, v, logf):
    B, S, H, Dh = q.shape
    nb = S // Q_BLOCK
    F = jnp.cumsum(logf, axis=1)
    F_k = F.transpose(0, 2, 1)
    kpos = jnp.arange(S)
    scale = Dh ** -0.5
    q_blk = q.reshape(B, nb, Q_BLOCK, H, Dh).swapaxes(0, 1)
    F_blk = F.reshape(B, nb, Q_BLOCK, H).swapaxes(0, 1)

    def block(args):
        qi, Fi, bi = args
        s = jnp.einsum('bqhd,bkhd->bhqk', qi, k, preferred_element_type=jnp.float32) * scale
        s = s + Fi.transpose(0, 2, 1)[..., None] - F_k[:, :, None, :]
        qpos = bi * Q_BLOCK + jnp.arange(Q_BLOCK)
        mask = kpos[None, :] <= qpos[:, None]
        s = jnp.where(mask[None, None], s, NEG_BIG)
        w = jax.nn.softmax(s, axis=-1)
        return jnp.einsum('bhqk,bkhd->bqhd', w.astype(v.dtype), v)

    o = lax.map(block, (q_blk, F_blk, jnp.arange(nb)))
    return o.swapaxes(0, 1).reshape(B, S, H * Dh)


def dsa_attention(q, k, v, qi, ki, wi):
    B, S, H, Dh = q.shape
    nb = S // Q_BLOCK
    topk = min(INDEX_TOPK_MAX, S // 4)
    kpos = jnp.arange(S)
    scale = Dh ** -0.5
    q_blk = q.reshape(B, nb, Q_BLOCK, H, Dh).swapaxes(0, 1)
    qi_blk = qi.reshape(B, nb, Q_BLOCK, qi.shape[2], qi.shape[3]).swapaxes(0, 1)
    wi_blk = wi.reshape(B, nb, Q_BLOCK, wi.shape[2]).swapaxes(0, 1)
    gather = jax.vmap(lambda kv, ix: kv[ix])

    def block(args):
        qb, qib, wib, bi = args
        qpos = bi * Q_BLOCK + jnp.arange(Q_BLOCK)
        rel = jax.nn.relu(jnp.einsum('bqhd,bsd->bqhs', qib, ki, preferred_element_type=jnp.float32))
        score = jnp.einsum('bqhs,bqh->bqs', rel, wib)
        mask = kpos[None, :] <= qpos[:, None]
        score = jnp.where(mask[None], score, NEG_BIG)
        _, idx = lax.top_k(score, topk)
        ksel = gather(k, idx)
        vsel = gather(v, idx)
        s = jnp.einsum('bqhd,bqkhd->bhqk', qb, ksel, preferred_element_type=jnp.float32) * scale
        valid = idx <= qpos[None, :, None]
        s = jnp.where(valid[:, None], s, NEG_BIG)
        w = jax.nn.softmax(s, axis=-1)
        return jnp.einsum('bhqk,bqkhd->bqhd', w.astype(v.dtype), vsel)

    o = lax.map(block, (q_blk, qi_blk, wi_blk, jnp.arange(nb)))
    return o.swapaxes(0, 1).reshape(B, S, H * Dh)


def setup_inputs(seed: int = 0) -> dict:
    key = jax.random.key(seed)
    ks = jax.random.split(key, 26)
    nrm = jax.random.normal
    L, D = DEPTH, D_MODEL

    def gain(k, n):
        return 1.0 + 0.02 * nrm(k, (L, n), jnp.float32)

    return {
        'x': nrm(ks[0], (BATCH, SEQ, D), jnp.float32),
        'p': nrm(ks[1], (L, BATCH, SEQ, PLE_DIM), jnp.float32),
        'g_ffn1': gain(ks[2], D),
        'w1_gate': nrm(ks[3], (L, D, D_FF), jnp.float32) * D ** -0.5,
        'w1_up': nrm(ks[4], (L, D, D_FF), jnp.float32) * D ** -0.5,
        'w1_down': nrm(ks[5], (L, D_FF, D), jnp.float32) * D_FF ** -0.5,
        'g_mix': gain(ks[6], D),
        'w_in': nrm(ks[7], (L, D, N_IN), jnp.float32) * D ** -0.5,
        'b_f': 2.0 + 0.5 * nrm(ks[8], (L, N_FOX_HEADS), jnp.float32),
        'g_qa': gain(ks[9], HEAD_DIM),
        'g_ka': gain(ks[10], HEAD_DIM),
        'g_qb': gain(ks[11], HEAD_DIM),
        'g_kb': gain(ks[12], HEAD_DIM),
        'g_ik': gain(ks[13], IDX_DIM),
        'w_br_fox': nrm(ks[14], (L, FOX_W, D), jnp.float32) * FOX_W ** -0.5,
        'w_br_dsa': nrm(ks[15], (L, DSA_W, D), jnp.float32) * DSA_W ** -0.5,
        'w_o': nrm(ks[16], (L, D, D), jnp.float32) * D ** -0.5,
        'g_ffn2': gain(ks[17], D),
        'w2_gate': nrm(ks[18], (L, D, D_FF), jnp.float32) * D ** -0.5,
        'w2_up': nrm(ks[19], (L, D, D_FF), jnp.float32) * D ** -0.5,
        'w2_down': nrm(ks[20], (L, D_FF, D), jnp.float32) * D_FF ** -0.5,
        'g_ple': gain(ks[21], D),
        'w_ple_gate': nrm(ks[22], (L, D, D), jnp.float32) * D ** -0.5,
        'w_ple_proj': nrm(ks[23], (L, PLE_DIM, D), jnp.float32) * PLE_DIM ** -0.5,
    }


def reference(x, p, g_ffn1, w1_gate, w1_up, w1_down, g_mix, w_in, b_f, g_qa, g_ka, g_qb, g_kb, g_ik,
              w_br_fox, w_br_dsa, w_o, g_ffn2, w2_gate, w2_up, w2_down, g_ple, w_ple_gate, w_ple_proj):
    B, S, _ = x.shape
    pos = jnp.arange(S)
    idx_scale = (N_IDX_HEADS ** -0.5) * (IDX_DIM ** -0.5)
    h = x
    for i in range(DEPTH):
        h = h + 0.5 * swiglu(rmsnorm(h, g_ffn1[i]), w1_gate[i], w1_up[i], w1_down[i])

        u = rmsnorm(h, g_mix[i])
        z = u @ w_in[i]
        qa, ka, va, fa, qb, kb, vb, qi, ki, wi, ga, gb = split_cols(z)

        qa = rmsnorm(qa.reshape(B, S, N_FOX_HEADS, HEAD_DIM), g_qa[i])
        ka = rmsnorm(ka.reshape(B, S, N_FOX_HEADS, HEAD_DIM), g_ka[i])
        va = va.reshape(B, S, N_FOX_HEADS, HEAD_DIM)
        logf = jax.nn.log_sigmoid(fa.astype(jnp.float32) + b_f[i].astype(jnp.float32))
        o_a = fox_attention(qa, ka, va, logf)

        qb = rope_partial(rmsnorm(qb.reshape(B, S, N_DSA_HEADS, HEAD_DIM), g_qb[i]), pos)
        kb = rope_partial(rmsnorm(kb.reshape(B, S, N_DSA_HEADS, HEAD_DIM), g_kb[i]), pos)
        vb = vb.reshape(B, S, N_DSA_HEADS, HEAD_DIM)
        qi = rope_partial(qi.reshape(B, S, N_IDX_HEADS, IDX_DIM), pos)
        ki = rope_partial(rmsnorm(ki, g_ik[i])[:, :, None, :], pos)[:, :, 0, :]
        wi = wi.astype(jnp.float32) * idx_scale
        o_b = dsa_attention(qb, kb, vb, qi, ki, wi)

        y = jax.nn.sigmoid(ga) * (o_a @ w_br_fox[i]) + jax.nn.sigmoid(gb) * (o_b @ w_br_dsa[i])
        h = h + y @ w_o[i]

        h = h + 0.5 * swiglu(rmsnorm(h, g_ffn2[i]), w2_gate[i], w2_up[i], w2_down[i])

        e = p[i] @ w_ple_proj[i]
        h = h + jax.nn.sigmoid(rmsnorm(h, g_ple[i]) @ w_ple_gate[i]) * e
    return h
```

```python
import functools

import jax
import jax.numpy as jnp
from jax import lax
from jax.experimental import pallas as pl
from jax.experimental.pallas import tpu as pltpu

D_MODEL = 4096
SEQ = 8192
DEPTH = 4
HEAD_DIM = 128
N_FOX_HEADS = 16
N_DSA_HEADS = 16
N_IDX_HEADS = 32
IDX_DIM = 64
INDEX_TOPK_MAX = 256
D_FF = 3584
PLE_DIM = 256
ROPE_THETA = 500000.0
ROT_FRAC_DIV = 4
NORM_EPS = 1e-6
NEG_BIG = -1e30
FOX_W = N_FOX_HEADS * HEAD_DIM
DSA_W = N_DSA_HEADS * HEAD_DIM
IDX_Q_W = N_IDX_HEADS * IDX_DIM

LANES = 128
VMEM_LIMIT = 56 * 1024 * 1024

MM_TM = 1024
MM_TN = 512
NORM_TM = 512
ATT_TQ = 1024
ATT_TK = 1024
IDX_TQ = 256
IDX_TC = 512

_INT_MIN = -(2 ** 31)


def _cparams(n_axes):
    return pltpu.CompilerParams(
        dimension_semantics=("arbitrary",) * n_axes,
        vmem_limit_bytes=VMEM_LIMIT,
    )


def _rmsnorm_kernel(x_ref, g_ref, o_ref):
    x = x_ref[...]
    ms = jnp.mean(x * x, axis=-1, keepdims=True)
    o_ref[...] = (x * lax.rsqrt(ms + NORM_EPS) * g_ref[...]).astype(o_ref.dtype)


def _rmsnorm(x, g):
    m, d = x.shape
    return pl.pallas_call(
        _rmsnorm_kernel,
        grid=(m // NORM_TM,),
        in_specs=[pl.BlockSpec((NORM_TM, d), lambda i: (i, 0)),
                  pl.BlockSpec((1, d), lambda i: (0, 0))],
        out_specs=pl.BlockSpec((NORM_TM, d), lambda i: (i, 0)),
        out_shape=jax.ShapeDtypeStruct((m, d), jnp.bfloat16),
        compiler_params=_cparams(1),
        name="rmsnorm",
    )(x, g.reshape(1, d))


def _mm_kernel(*refs, n_lhs, n_rhs, pairs, n_extra, epilogue):
    lhs = refs[:n_lhs]
    rhs = refs[n_lhs:n_lhs + n_rhs]
    extra = refs[n_lhs + n_rhs:n_lhs + n_rhs + n_extra]
    out = refs[n_lhs + n_rhs + n_extra]
    accs = [jnp.dot(lhs[a][...], rhs[b][...], preferred_element_type=jnp.float32)
            for a, b in pairs]
    epilogue(accs, extra, out)


def _matmul(lhs, rhs, pairs, extras, epilogue, out_dtype, *, name, tn=MM_TN):
    m = lhs[0].shape[0]
    n = rhs[0].shape[1]
    tm = MM_TM
    in_specs = []
    for a in lhs:
        in_specs.append(pl.BlockSpec((tm, a.shape[1]), lambda i, j: (i, 0)))
    for w in rhs:
        in_specs.append(pl.BlockSpec((w.shape[0], tn), lambda i, j: (0, j)))
    for _, bs, im in extras:
        in_specs.append(pl.BlockSpec(bs, im))
    kern = functools.partial(_mm_kernel, n_lhs=len(lhs), n_rhs=len(rhs), pairs=pairs,
                             n_extra=len(extras), epilogue=epilogue)
    return pl.pallas_call(
        kern,
        grid=(m // tm, n // tn),
        in_specs=in_specs,
        out_specs=pl.BlockSpec((tm, tn), lambda i, j: (i, j)),
        out_shape=jax.ShapeDtypeStruct((m, n), out_dtype),
        compiler_params=_cparams(2),
        name=name,
    )(*lhs, *rhs, *[e[0] for e in extras])


def _ep_cast(accs, extra, out):
    out[...] = accs[0].astype(out.dtype)


def _ep_sigmoid(accs, extra, out):
    out[...] = jax.nn.sigmoid(accs[0]).astype(out.dtype)


def _ep_swiglu(accs, extra, out):
    out[...] = (jax.nn.silu(accs[0]) * accs[1]).astype(out.dtype)


def _ep_residual(accs, extra, out, *, scale):
    out[...] = extra[0][...] + scale * accs[0]


def _ep_merge(accs, extra, out):
    ga = extra[0][...].astype(jnp.float32)
    gb = extra[1][...].astype(jnp.float32)
    out[...] = (ga * accs[0] + gb * accs[1]).astype(out.dtype)


def _ep_ple(accs, extra, out):
    out[...] = extra[0][...] + jax.nn.sigmoid(accs[0]) * accs[1]


def _rope_lanes(y, cos, s_up, s_dn, half):
    up = pltpu.roll(y, LANES - half, 1)
    dn = pltpu.roll(y, half, 1)
    return y * cos + up * s_up + dn * s_dn


def _ep_headnorm(accs, extra, out, *, rope_half):
    acc = accs[0]
    gain = extra[0][...]
    for g in range(acc.shape[1] // LANES):
        sl = slice(g * LANES, (g + 1) * LANES)
        x = acc[:, sl]
        ms = jnp.mean(x * x, axis=-1, keepdims=True)
        y = x * lax.rsqrt(ms + NORM_EPS) * gain[:, sl]
        if rope_half:
            y = _rope_lanes(y, extra[1][...], extra[2][...], extra[3][...], rope_half)
        out[:, sl] = y.astype(out.dtype)


def _ep_rope(accs, extra, out, *, rope_half):
    acc = accs[0]
    for g in range(acc.shape[1] // LANES):
        sl = slice(g * LANES, (g + 1) * LANES)
        y = _rope_lanes(acc[:, sl], extra[0][...], extra[1][...], extra[2][...], rope_half)
        out[:, sl] = y.astype(out.dtype)


def _small_kernel(z_ref, bf_ref, gik_ref, cos_ref, sup_ref, sdn_ref,
                  f_ref, kie_ref, kio_ref, wi_ref, carry_sc, *, idx_scale):
    s = z_ref.shape[0]

    @pl.when(pl.program_id(0) == 0)
    def _():
        carry_sc[...] = jnp.zeros(carry_sc.shape, jnp.float32)

    x = z_ref[:, 0:LANES] + bf_ref[...]
    logf = jnp.minimum(x, 0.0) - jnp.log1p(jnp.exp(-jnp.abs(x)))
    row = lax.broadcasted_iota(jnp.int32, (s, LANES), 0)
    d = 1
    while d < s:
        logf = logf + jnp.where(row >= d, pltpu.roll(logf, d, 0), 0.0)
        d *= 2
    f_ref[...] = logf + carry_sc[0:1, :]
    carry_sc[0:1, :] = f_ref[s - 1:s, :]
    k = z_ref[:, LANES:2 * LANES]
    ms = jnp.sum(k * k, axis=-1, keepdims=True) * (1.0 / IDX_DIM)
    k = k * lax.rsqrt(ms + NORM_EPS) * gik_ref[...]
    k = _rope_lanes(k, cos_ref[...], sup_ref[...], sdn_ref[...], IDX_DIM // ROT_FRAC_DIV // 2)
    lane = lax.broadcasted_iota(jnp.int32, (s, LANES), 1)
    k = jnp.where(lane < IDX_DIM, k, 0.0)
    kie_ref[...] = k.astype(kie_ref.dtype)
    kio_ref[...] = pltpu.roll(k, IDX_DIM, 1).astype(kio_ref.dtype)
    wi_ref[...] = z_ref[:, 2 * LANES:3 * LANES] * idx_scale


def _small(z, bf, gik, cos, sup, sdn, idx_scale):
    s = z.shape[0]
    tm = MM_TM
    rows = lambda w: pl.BlockSpec((tm, w), lambda i: (i, 0))
    vec = pl.BlockSpec((1, LANES), lambda i: (0, 0))
    return pl.pallas_call(
        functools.partial(_small_kernel, idx_scale=idx_scale),
        grid=(s // tm,),
        in_specs=[rows(3 * LANES), vec, vec, rows(LANES), rows(LANES), rows(LANES)],
        out_specs=[rows(LANES)] * 4,
        out_shape=[jax.ShapeDtypeStruct((s, LANES), jnp.float32),
                   jax.ShapeDtypeStruct((s, LANES), jnp.bfloat16),
                   jax.ShapeDtypeStruct((s, LANES), jnp.bfloat16),
                   jax.ShapeDtypeStruct((s, LANES), jnp.float32)],
        scratch_shapes=[pltpu.VMEM((8, LANES), jnp.float32)],
        compiler_params=_cparams(1),
        name="small_heads",
    )(z, bf, gik, cos, sup, sdn)


def _sortable(x):
    b = pltpu.bitcast(x, jnp.int32)
    return jnp.where(b < 0, b ^ jnp.int32(0x7FFFFFFF), b)


def _index_kernel(qi_ref, kte_ref, kto_ref, wi_ref, bias_ref, keys_sc, wb_sc, *, topk):
    tq, s = bias_ref.shape
    tc = IDX_TC
    i = pl.program_id(0)
    row0 = i * tq
    n_chunks = (row0 + tq + tc - 1) // tc
    n_tiles = tc // LANES

    for j in range(N_IDX_HEADS):
        wb_sc[j] = jnp.broadcast_to(wi_ref[:, j:j + 1], (tq, LANES))

    row_ids = row0 + lax.broadcasted_iota(jnp.int32, (tq, tc), 0)
    col_iota = lax.broadcasted_iota(jnp.int32, (tq, tc), 1)

    def score_chunk(c, carry):
        col0 = pl.multiple_of(c * tc, tc)
        kte = kte_ref[:, pl.ds(col0, tc)]
        kto = kto_ref[:, pl.ds(col0, tc)]
        score = jnp.zeros((tq, tc), jnp.float32)
        for p in range(N_IDX_HEADS // 2):
            qp = qi_ref[:, p * LANES:(p + 1) * LANES]
            re = jnp.maximum(jnp.dot(qp, kte, preferred_element_type=jnp.float32), 0.0)
            ro = jnp.maximum(jnp.dot(qp, kto, preferred_element_type=jnp.float32), 0.0)
            we = jnp.concatenate([wb_sc[2 * p]] * n_tiles, axis=1)
            wo = jnp.concatenate([wb_sc[2 * p + 1]] * n_tiles, axis=1)
            score = score + re * we + ro * wo
        causal = (col0 + col_iota) <= row_ids
        score = jnp.where(causal, score, NEG_BIG)
        keys_sc[:, pl.ds(col0, tc)] = _sortable(score)
        return carry

    lax.fori_loop(0, n_chunks, score_chunk, 0)

    def bit_step(b, thr):
        cand = thr + lax.shift_left(jnp.int32(1), 31 - b)

        def count_chunk(c, cnt):
            col0 = pl.multiple_of(c * tc, tc)
            k = keys_sc[:, pl.ds(col0, tc)]
            for t in range(n_tiles):
                ge = k[:, t * LANES:(t + 1) * LANES] >= cand
                cnt = cnt + jnp.where(ge, 1.0, 0.0)
            return cnt

        cnt = lax.fori_loop(0, n_chunks, count_chunk, jnp.zeros((tq, LANES), jnp.float32))
        total = jnp.sum(cnt, axis=1, keepdims=True)
        return jnp.where(total >= topk, cand, thr)

    thr = lax.fori_loop(0, 32, bit_step, jnp.full((tq, LANES), _INT_MIN, jnp.int32))

    def bias_chunk(c, carry):
        col0 = pl.multiple_of(c * tc, tc)
        k = keys_sc[:, pl.ds(col0, tc)]
        causal = (col0 + col_iota) <= row_ids
        thr_w = jnp.concatenate([thr] * n_tiles, axis=1)
        keep = jnp.logical_and(k >= thr_w, causal)
        bias_ref[:, pl.ds(col0, tc)] = jnp.where(keep, 0.0, NEG_BIG).astype(bias_ref.dtype)
        return carry

    lax.fori_loop(0, n_chunks, bias_chunk, 0)

    def fill_chunk(c, carry):
        col0 = pl.multiple_of(c * tc, tc)
        bias_ref[:, pl.ds(col0, tc)] = jnp.full((tq, tc), NEG_BIG, bias_ref.dtype)
        return carry

    lax.fori_loop(n_chunks, s // tc, fill_chunk, 0)


def _index_bias(qi, kte, kto, wi, topk):
    s = qi.shape[0]
    tq = IDX_TQ
    return pl.pallas_call(
        functools.partial(_index_kernel, topk=topk),
        grid=(s // tq,),
        in_specs=[pl.BlockSpec((tq, IDX_Q_W), lambda i: (i, 0)),
                  pl.BlockSpec((LANES, s), lambda i: (0, 0)),
                  pl.BlockSpec((LANES, s), lambda i: (0, 0)),
                  pl.BlockSpec((tq, LANES), lambda i: (i, 0))],
        out_specs=pl.BlockSpec((tq, s), lambda i: (i, 0)),
        out_shape=jax.ShapeDtypeStruct((s, s), jnp.bfloat16),
        scratch_shapes=[pltpu.VMEM((tq, s), jnp.int32),
                        pltpu.VMEM((N_IDX_HEADS, tq, LANES), jnp.float32)],
        compiler_params=_cparams(1),
        name="index_topk_bias",
    )(qi, kte, kto, wi)


def _attn_kernel(qt_ref, kt_ref, q_ref, k_ref, v_ref, *rest, fox):
    if fox:
        fq_ref, fk_ref, o_ref, m_sc, l_sc, acc_sc = rest
    else:
        bias_ref, o_ref, m_sc, l_sc, acc_sc = rest
    p_id = pl.program_id(1)
    qb = qt_ref[p_id]
    kb = kt_ref[p_id]
    tq, tk = q_ref.shape[0], k_ref.shape[0]

    @pl.when(kb == 0)
    def _():
        m_sc[...] = jnp.full(m_sc.shape, NEG_BIG, jnp.float32)
        l_sc[...] = jnp.zeros(l_sc.shape, jnp.float32)
        acc_sc[...] = jnp.zeros(acc_sc.shape, jnp.float32)

    s = lax.dot_general(q_ref[...], k_ref[...], (((1,), (1,)), ((), ())),
                        preferred_element_type=jnp.float32) * (HEAD_DIM ** -0.5)
    if fox:
        s = s + fq_ref[...] - fk_ref[...]
        rows = qb * tq + lax.broadcasted_iota(jnp.int32, (tq, tk), 0)
        cols = kb * tk + lax.broadcasted_iota(jnp.int32, (tq, tk), 1)
        s = jnp.where(cols <= rows, s, NEG_BIG)
    else:
        s = s + bias_ref[...].astype(jnp.float32)

    m_prev = m_sc[...]
    m_new = jnp.maximum(m_prev, jnp.max(s, axis=1, keepdims=True))
    alpha = jnp.exp(m_prev - m_new)
    p = jnp.exp(s - m_new)
    l_sc[...] = alpha * l_sc[...] + jnp.sum(p, axis=1, keepdims=True)
    acc_sc[...] = alpha * acc_sc[...] + jnp.dot(p.astype(v_ref.dtype), v_ref[...],
                                                preferred_element_type=jnp.float32)
    m_sc[...] = m_new

    @pl.when(kb == qb)
    def _():
        o_ref[...] = (acc_sc[...] / l_sc[...]).astype(o_ref.dtype)


def _attention(qk, v, q_col, k_col, v_col, *, fcol=None, frow=None, bias=None):
    s = qk.shape[0]
    n_heads = N_FOX_HEADS
    tq, tk = ATT_TQ, ATT_TK
    assert tq == tk
    nq = s // tq
    pairs = [(a, b) for a in range(nq) for b in range(a + 1)]
    qt = jnp.asarray([a for a, _ in pairs], jnp.int32)
    kt = jnp.asarray([b for _, b in pairs], jnp.int32)
    fox = bias is None
    in_specs = [
        pl.BlockSpec((tq, HEAD_DIM), lambda h, p, qt, kt: (qt[p], q_col + h)),
        pl.BlockSpec((tk, HEAD_DIM), lambda h, p, qt, kt: (kt[p], k_col + h)),
        pl.BlockSpec((tk, HEAD_DIM), lambda h, p, qt, kt: (kt[p], v_col + h)),
    ]
    if fox:
        in_specs += [pl.BlockSpec((None, tq, 1), lambda h, p, qt, kt: (h, qt[p], 0)),
                     pl.BlockSpec((None, 1, tk), lambda h, p, qt, kt: (h, 0, kt[p]))]
        operands = (qk, qk, v, fcol, frow)
    else:
        in_specs += [pl.BlockSpec((tq, tk), lambda h, p, qt, kt: (qt[p], kt[p]))]
        operands = (qk, qk, v, bias)
    grid_spec = pltpu.PrefetchScalarGridSpec(
        num_scalar_prefetch=2,
        grid=(n_heads, len(pairs)),
        in_specs=in_specs,
        out_specs=pl.BlockSpec((tq, HEAD_DIM), lambda h, p, qt, kt: (qt[p], h)),
        scratch_shapes=[pltpu.VMEM((tq, 1), jnp.float32),
                        pltpu.VMEM((tq, 1), jnp.float32),
                        pltpu.VMEM((tq, HEAD_DIM), jnp.float32)],
    )
    return pl.pallas_call(
        functools.partial(_attn_kernel, fox=fox),
        grid_spec=grid_spec,
        out_shape=jax.ShapeDtypeStruct((s, n_heads * HEAD_DIM), jnp.bfloat16),
        compiler_params=_cparams(2),
        name="fox_attention" if fox else "dsa_attention",
    )(qt, kt, *operands)


def _rope_tables(s, rot, period):
    half = rot // 2
    pos = jnp.arange(s, dtype=jnp.float32)
    inv = jnp.power(jnp.float32(ROPE_THETA), -2.0 * jnp.arange(half, dtype=jnp.float32) / rot)
    ang = pos[:, None] * inv[None, :]
    cos, sin = jnp.cos(ang), jnp.sin(ang)
    ones = jnp.ones((s, period - rot), jnp.float32)
    zeros = lambda n: jnp.zeros((s, n), jnp.float32)
    c = jnp.concatenate([cos, cos, ones], axis=1)
    s_up = jnp.concatenate([-sin, zeros(period - half)], axis=1)
    s_dn = jnp.concatenate([zeros(half), sin, zeros(period - rot)], axis=1)
    reps = LANES // period
    return tuple(jnp.tile(t, (1, reps)) for t in (c, s_up, s_dn))


def _pad_cols(w, n):
    return jnp.pad(w, ((0, 0), (0, n - w.shape[1])))


def _pad_row(v, n):
    return jnp.pad(v, (0, n - v.shape[0])).reshape(1, n)


def kernel(x, p, g_ffn1, w1_gate, w1_up, w1_down, g_mix, w_in, b_f, g_qa, g_ka, g_qb, g_kb, g_ik,
           w_br_fox, w_br_dsa, w_o, g_ffn2, w2_gate, w2_up, w2_down, g_ple, w_ple_gate, w_ple_proj):
    b, s, d = x.shape
    assert b == 1
    bf16 = jnp.bfloat16
    topk = min(INDEX_TOPK_MAX, s // 4)
    idx_scale = (N_IDX_HEADS ** -0.5) * (IDX_DIM ** -0.5)
    rope_b = _rope_tables(s, HEAD_DIM // ROT_FRAC_DIV, HEAD_DIM)
    rope_i = _rope_tables(s, IDX_DIM // ROT_FRAC_DIV, IDX_DIM)
    half_b = HEAD_DIM // ROT_FRAC_DIV // 2
    half_i = IDX_DIM // ROT_FRAC_DIV // 2
    row_tab = lambda t: (t, (MM_TM, LANES), lambda i, j: (i, 0))

    def ffn(h, g, wg, wu, wd, tag):
        u = _rmsnorm(h, g)
        a = _matmul([u], [wg.astype(bf16), wu.astype(bf16)], [(0, 0), (0, 1)], [],
                    _ep_swiglu, bf16, name=tag + "_swiglu")
        return _matmul([a], [wd.astype(bf16)], [(0, 0)],
                       [(h, (MM_TM, MM_TN), lambda i, j: (i, j))],
                       functools.partial(_ep_residual, scale=0.5), jnp.float32, name=tag + "_down")

    h = x.reshape(s, d)
    for i in range(DEPTH):
        h = ffn(h, g_ffn1[i], w1_gate[i], w1_up[i], w1_down[i], "ffn1")

        u = _rmsnorm(h, g_mix[i])
        w = w_in[i]
        c = 0
        cols = {}
        for name, n in (("qa", FOX_W), ("ka", FOX_W), ("va", FOX_W), ("fa", N_FOX_HEADS),
                        ("qb", DSA_W), ("kb", DSA_W), ("vb", DSA_W), ("qi", IDX_Q_W),
                        ("ki", IDX_DIM), ("wi", N_IDX_HEADS), ("ga", D_MODEL), ("gb", D_MODEL)):
            cols[name] = w[:, c:c + n]
            c += n
        w_qka = jnp.concatenate([cols["qa"], cols["ka"]], axis=1).astype(bf16)
        w_qkb = jnp.concatenate([cols["qb"], cols["kb"]], axis=1).astype(bf16)
        w_v = jnp.concatenate([cols["va"], cols["vb"]], axis=1).astype(bf16)
        w_g = jnp.concatenate([cols["ga"], cols["gb"]], axis=1).astype(bf16)
        w_small = jnp.concatenate([_pad_cols(cols["fa"], LANES), _pad_cols(cols["ki"], LANES),
                                   _pad_cols(cols["wi"], LANES)], axis=1).astype(bf16)
        gain_a = jnp.concatenate([jnp.tile(g_qa[i], N_FOX_HEADS), jnp.tile(g_ka[i], N_FOX_HEADS)])
        gain_b = jnp.concatenate([jnp.tile(g_qb[i], N_DSA_HEADS), jnp.tile(g_kb[i], N_DSA_HEADS)])
        col_vec = lambda v: (v.reshape(1, -1), (1, MM_TN), lambda i, j: (0, j))

        qk_a = _matmul([u], [w_qka], [(0, 0)], [col_vec(gain_a)],
                       functools.partial(_ep_headnorm, rope_half=0), bf16, name="proj_qk_fox")
        qk_b = _matmul([u], [w_qkb], [(0, 0)], [col_vec(gain_b)] + [row_tab(t) for t in rope_b],
                       functools.partial(_ep_headnorm, rope_half=half_b), bf16, name="proj_qk_dsa")
        v_ab = _matmul([u], [w_v], [(0, 0)], [], _ep_cast, bf16, name="proj_v")
        q_i = _matmul([u], [cols["qi"].astype(bf16)], [(0, 0)], [row_tab(t) for t in rope_i],
                      functools.partial(_ep_rope, rope_half=half_i), bf16, name="proj_qidx")
        gates = _matmul([u], [w_g], [(0, 0)], [], _ep_sigmoid, bf16, name="proj_gates")
        z_small = _matmul([u], [w_small], [(0, 0)], [], _ep_cast, jnp.float32,
                          name="proj_small", tn=3 * LANES)

        f_cum, ki_e, ki_o, wi_s = _small(z_small, _pad_row(b_f[i], LANES), _pad_row(g_ik[i], LANES),
                                         *rope_i, idx_scale)

        f_t = f_cum[:, :N_FOX_HEADS].T
        o_a = _attention(qk_a, v_ab, 0, N_FOX_HEADS, 0,
                         fcol=f_t[:, :, None], frow=f_t[:, None, :])

        bias = _index_bias(q_i, ki_e.T, ki_o.T, wi_s, topk)
        o_b = _attention(qk_b, v_ab, 0, N_DSA_HEADS, N_FOX_HEADS, bias=bias)

        n_gate_blocks = D_MODEL // MM_TN
        y = _matmul([o_a, o_b], [w_br_fox[i].astype(bf16), w_br_dsa[i].astype(bf16)],
                    [(0, 0), (1, 1)],
                    [(gates, (MM_TM, MM_TN), lambda i, j: (i, j)),
                     (gates, (MM_TM, MM_TN), lambda i, j: (i, j + n_gate_blocks))],
                    _ep_merge, bf16, name="branch_merge")
        h = _matmul([y], [w_o[i].astype(bf16)], [(0, 0)],
                    [(h, (MM_TM, MM_TN), lambda i, j: (i, j))],
                    functools.partial(_ep_residual, scale=1.0), jnp.float32, name="out_proj")

        h = ffn(h, g_ffn2[i], w2_gate[i], w2_up[i], w2_down[i], "ffn2")

        u = _rmsnorm(h, g_ple[i])
        h = _matmul([u, p[i].reshape(s, PLE_DIM).astype(bf16)],
                    [w_ple_gate[i].astype(bf16), w_ple_proj[i].astype(bf16)],
                    [(0, 0), (1, 1)],
                    [(h, (MM_TM, MM_TN), lambda i, j: (i, j))],
                    _ep_ple, jnp.float32, name="ple")
    return h.reshape(b, s, d)
```

```python
import functools

import jax
import jax.numpy as jnp
from jax import lax
from jax.experimental import pallas as pl
from jax.experimental.pallas import tpu as pltpu

D_MODEL = 4096
SEQ = 8192
DEPTH = 4
HEAD_DIM = 128
N_FOX_HEADS = 16
N_DSA_HEADS = 16
N_IDX_HEADS = 32
IDX_DIM = 64
INDEX_TOPK_MAX = 256
D_FF = 3584
PLE_DIM = 256
ROPE_THETA = 500000.0
ROT_FRAC_DIV = 4
NORM_EPS = 1e-6
NEG_BIG = -1e30
LOG2E = 1.4426950408889634
FOX_W = N_FOX_HEADS * HEAD_DIM
DSA_W = N_DSA_HEADS * HEAD_DIM
IDX_Q_W = N_IDX_HEADS * IDX_DIM

LANES = 128
VMEM_LIMIT = 56 * 1024 * 1024

MM_TM = 1024
MM_TN = 512
NORM_TM = 512
ATT_TQ = 1024
ATT_TK = 1024
ATT_RC = 256
IDX_TQ = 256
IDX_TC = 512

_INT_MIN = -(2 ** 31)


def _cparams(n_axes):
    return pltpu.CompilerParams(
        dimension_semantics=("arbitrary",) * n_axes,
        vmem_limit_bytes=VMEM_LIMIT,
    )


def _rmsnorm_kernel(x_ref, g_ref, o_ref):
    x = x_ref[...]
    ms = jnp.mean(x * x, axis=-1, keepdims=True)
    o_ref[...] = (x * lax.rsqrt(ms + NORM_EPS) * g_ref[...]).astype(o_ref.dtype)


def _rmsnorm(x, g):
    m, d = x.shape
    return pl.pallas_call(
        _rmsnorm_kernel,
        grid=(m // NORM_TM,),
        in_specs=[pl.BlockSpec((NORM_TM, d), lambda i: (i, 0)),
                  pl.BlockSpec((1, d), lambda i: (0, 0))],
        out_specs=pl.BlockSpec((NORM_TM, d), lambda i: (i, 0)),
        out_shape=jax.ShapeDtypeStruct((m, d), jnp.bfloat16),
        compiler_params=_cparams(1),
        name="rmsnorm",
    )(x, g.reshape(1, d))


def _mm_kernel(*refs, n_lhs, n_rhs, pairs, n_extra, epilogue):
    lhs = refs[:n_lhs]
    rhs = refs[n_lhs:n_lhs + n_rhs]
    extra = refs[n_lhs + n_rhs:n_lhs + n_rhs + n_extra]
    out = refs[n_lhs + n_rhs + n_extra]
    accs = [jnp.dot(lhs[a][...], rhs[b][...], preferred_element_type=jnp.float32)
            for a, b in pairs]
    epilogue(accs, extra, out)


def _matmul(lhs, rhs, pairs, extras, epilogue, out_dtype, *, name, tn=MM_TN, n=None, rhs_off=None):
    m = lhs[0].shape[0]
    n = rhs[0].shape[1] if n is None else n
    rhs_off = [0] * len(rhs) if rhs_off is None else rhs_off
    tm = MM_TM
    in_specs = []
    for a in lhs:
        in_specs.append(pl.BlockSpec((tm, a.shape[1]), lambda i, j: (i, 0)))
    for w, off in zip(rhs, rhs_off):
        in_specs.append(pl.BlockSpec((w.shape[0], tn), lambda i, j, off=off: (0, j + off)))
    for _, bs, im in extras:
        in_specs.append(pl.BlockSpec(bs, im))
    kern = functools.partial(_mm_kernel, n_lhs=len(lhs), n_rhs=len(rhs), pairs=pairs,
                             n_extra=len(extras), epilogue=epilogue)
    return pl.pallas_call(
        kern,
        grid=(m // tm, n // tn),
        in_specs=in_specs,
        out_specs=pl.BlockSpec((tm, tn), lambda i, j: (i, j)),
        out_shape=jax.ShapeDtypeStruct((m, n), out_dtype),
        compiler_params=_cparams(2),
        name=name,
    )(*lhs, *rhs, *[e[0] for e in extras])


def _ep_cast(accs, extra, out):
    out[...] = accs[0].astype(out.dtype)


def _ep_sigmoid(accs, extra, out):
    out[...] = jax.nn.sigmoid(accs[0]).astype(out.dtype)


def _ep_swiglu(accs, extra, out):
    out[...] = (jax.nn.silu(accs[0]) * accs[1]).astype(out.dtype)


def _ep_residual(accs, extra, out, *, scale):
    out[...] = extra[0][...] + scale * accs[0]


def _ep_merge(accs, extra, out):
    ga = extra[0][...].astype(jnp.float32)
    gb = extra[1][...].astype(jnp.float32)
    out[...] = (ga * accs[0] + gb * accs[1]).astype(out.dtype)


def _ep_ple(accs, extra, out):
    out[...] = extra[0][...] + jax.nn.sigmoid(accs[0]) * accs[1]


def _rope_lanes(y, cos, s_up, s_dn, half):
    up = pltpu.roll(y, LANES - half, 1)
    dn = pltpu.roll(y, half, 1)
    return y * cos + up * s_up + dn * s_dn


def _ep_headnorm(accs, extra, out, *, rope_half):
    acc = accs[0]
    gain = extra[0][...]
    for g in range(acc.shape[1] // LANES):
        sl = slice(g * LANES, (g + 1) * LANES)
        x = acc[:, sl]
        ms = jnp.mean(x * x, axis=-1, keepdims=True)
        y = x * lax.rsqrt(ms + NORM_EPS) * gain[:, sl]
        if rope_half:
            y = _rope_lanes(y, extra[1][...], extra[2][...], extra[3][...], rope_half)
        out[:, sl] = y.astype(out.dtype)


def _ep_rope(accs, extra, out, *, rope_half):
    acc = accs[0]
    for g in range(acc.shape[1] // LANES):
        sl = slice(g * LANES, (g + 1) * LANES)
        y = _rope_lanes(acc[:, sl], extra[0][...], extra[1][...], extra[2][...], rope_half)
        out[:, sl] = y.astype(out.dtype)


def _small_kernel(z_ref, bf_ref, gik_ref, cos_ref, sup_ref, sdn_ref,
                  f_ref, kie_ref, kio_ref, wi_ref, carry_sc, *, idx_scale):
    s = z_ref.shape[0]

    @pl.when(pl.program_id(0) == 0)
    def _():
        carry_sc[...] = jnp.zeros(carry_sc.shape, jnp.float32)

    x = z_ref[:, 0:LANES] + bf_ref[...]
    logf = jnp.minimum(x, 0.0) - jnp.log1p(jnp.exp(-jnp.abs(x)))
    row = lax.broadcasted_iota(jnp.int32, (s, LANES), 0)
    d = 1
    while d < s:
        logf = logf + jnp.where(row >= d, pltpu.roll(logf, d, 0), 0.0)
        d *= 2
    f_ref[...] = logf + carry_sc[0:1, :]
    carry_sc[0:1, :] = f_ref[s - 1:s, :]
    k = z_ref[:, LANES:2 * LANES]
    ms = jnp.sum(k * k, axis=-1, keepdims=True) * (1.0 / IDX_DIM)
    k = k * lax.rsqrt(ms + NORM_EPS) * gik_ref[...]
    k = _rope_lanes(k, cos_ref[...], sup_ref[...], sdn_ref[...], IDX_DIM // ROT_FRAC_DIV // 2)
    lane = lax.broadcasted_iota(jnp.int32, (s, LANES), 1)
    k = jnp.where(lane < IDX_DIM, k, 0.0)
    kie_ref[...] = k.astype(kie_ref.dtype)
    kio_ref[...] = pltpu.roll(k, IDX_DIM, 1).astype(kio_ref.dtype)
    wi_ref[...] = z_ref[:, 2 * LANES:3 * LANES] * idx_scale


def _small(z, bf, gik, cos, sup, sdn, idx_scale):
    s = z.shape[0]
    tm = MM_TM
    rows = lambda w: pl.BlockSpec((tm, w), lambda i: (i, 0))
    vec = pl.BlockSpec((1, LANES), lambda i: (0, 0))
    return pl.pallas_call(
        functools.partial(_small_kernel, idx_scale=idx_scale),
        grid=(s // tm,),
        in_specs=[rows(3 * LANES), vec, vec, rows(LANES), rows(LANES), rows(LANES)],
        out_specs=[rows(LANES)] * 4,
        out_shape=[jax.ShapeDtypeStruct((s, LANES), jnp.float32),
                   jax.ShapeDtypeStruct((s, LANES), jnp.bfloat16),
                   jax.ShapeDtypeStruct((s, LANES), jnp.bfloat16),
                   jax.ShapeDtypeStruct((s, LANES), jnp.float32)],
        scratch_shapes=[pltpu.VMEM((8, LANES), jnp.float32)],
        compiler_params=_cparams(1),
        name="small_heads",
    )(z, bf, gik, cos, sup, sdn)


def _sortable(x):
    b = pltpu.bitcast(x, jnp.int32)
    return jnp.where(b < 0, b ^ jnp.int32(0x7FFFFFFF), b)


def _index_kernel(qi_ref, kte_ref, kto_ref, wi_ref, bias_ref, keys_sc, wb_sc, *, topk):
    tq, s = bias_ref.shape
    tc = IDX_TC
    i = pl.program_id(0)
    row0 = i * tq
    n_chunks = (row0 + tq + tc - 1) // tc
    n_tiles = tc // LANES

    for j in range(N_IDX_HEADS):
        wb_sc[j] = jnp.broadcast_to(wi_ref[:, j:j + 1], (tq, LANES))

    row_ids = row0 + lax.broadcasted_iota(jnp.int32, (tq, tc), 0)
    col_iota = lax.broadcasted_iota(jnp.int32, (tq, tc), 1)

    def score_chunk(c, carry):
        col0 = pl.multiple_of(c * tc, tc)
        kte = kte_ref[:, pl.ds(col0, tc)]
        kto = kto_ref[:, pl.ds(col0, tc)]
        score = jnp.zeros((tq, tc), jnp.float32)
        for p in range(N_IDX_HEADS // 2):
            qp = qi_ref[:, p * LANES:(p + 1) * LANES]
            re = jnp.maximum(jnp.dot(qp, kte, preferred_element_type=jnp.float32), 0.0)
            ro = jnp.maximum(jnp.dot(qp, kto, preferred_element_type=jnp.float32), 0.0)
            we = jnp.concatenate([wb_sc[2 * p]] * n_tiles, axis=1)
            wo = jnp.concatenate([wb_sc[2 * p + 1]] * n_tiles, axis=1)
            score = score + re * we + ro * wo
        causal = (col0 + col_iota) <= row_ids
        score = jnp.where(causal, score, NEG_BIG)
        keys_sc[:, pl.ds(col0, tc)] = _sortable(score)
        return carry

    lax.fori_loop(0, n_chunks, score_chunk, 0)

    @pl.when(n_chunks % 2 == 1)
    def _():
        col0 = pl.multiple_of(n_chunks * tc, tc)
        keys_sc[:, pl.ds(col0, tc)] = _sortable(jnp.full((tq, tc), NEG_BIG, jnp.float32))

    tcc = 2 * tc

    def bit_step(state):
        b, thr, kept, _ = state
        cand = thr + lax.shift_left(jnp.int32(1), 31 - b)

        def count_chunk(c, cnt):
            col0 = pl.multiple_of(c * tcc, tcc)
            k = keys_sc[:, pl.ds(col0, tcc)]
            for t in range(tcc // LANES):
                ge = k[:, t * LANES:(t + 1) * LANES] >= cand
                cnt = cnt + jnp.where(ge, 1.0, 0.0)
            return cnt

        cnt = lax.fori_loop(0, (n_chunks + 1) // 2, count_chunk,
                            jnp.zeros((tq, LANES), jnp.float32))
        total = jnp.sum(cnt, axis=1, keepdims=True)
        accept = total >= topk
        thr = jnp.where(accept, cand, thr)
        kept = jnp.where(accept, total, kept)
        unresolved = jnp.max(jnp.where(kept == topk, 0.0, 1.0))
        return b + 1, thr, kept, unresolved

    def search_on(state):
        b, _, _, unresolved = state
        return jnp.logical_and(b < 32, unresolved > 0.5)

    _, thr, _, _ = lax.while_loop(
        search_on, bit_step,
        (jnp.int32(0), jnp.full((tq, LANES), _INT_MIN, jnp.int32),
         jnp.full((tq, LANES), float(2 * s), jnp.float32), jnp.float32(1.0)))

    def bias_chunk(c, carry):
        col0 = pl.multiple_of(c * tc, tc)
        k = keys_sc[:, pl.ds(col0, tc)]
        causal = (col0 + col_iota) <= row_ids
        thr_w = jnp.concatenate([thr] * n_tiles, axis=1)
        keep = jnp.logical_and(k >= thr_w, causal)
        bias_ref[:, pl.ds(col0, tc)] = jnp.where(keep, 0.0, NEG_BIG).astype(bias_ref.dtype)
        return carry

    lax.fori_loop(0, n_chunks, bias_chunk, 0)

    def fill_chunk(c, carry):
        col0 = pl.multiple_of(c * tc, tc)
        bias_ref[:, pl.ds(col0, tc)] = jnp.full((tq, tc), NEG_BIG, bias_ref.dtype)
        return carry

    lax.fori_loop(n_chunks, s // tc, fill_chunk, 0)


def _index_bias(qi, kte, kto, wi, topk):
    s = qi.shape[0]
    tq = IDX_TQ
    return pl.pallas_call(
        functools.partial(_index_kernel, topk=topk),
        grid=(s // tq,),
        in_specs=[pl.BlockSpec((tq, IDX_Q_W), lambda i: (i, 0)),
                  pl.BlockSpec((LANES, s), lambda i: (0, 0)),
                  pl.BlockSpec((LANES, s), lambda i: (0, 0)),
                  pl.BlockSpec((tq, LANES), lambda i: (i, 0))],
        out_specs=pl.BlockSpec((tq, s), lambda i: (i, 0)),
        out_shape=jax.ShapeDtypeStruct((s, s), jnp.bfloat16),
        scratch_shapes=[pltpu.VMEM((tq, s), jnp.int32),
                        pltpu.VMEM((N_IDX_HEADS, tq, LANES), jnp.float32)],
        compiler_params=_cparams(1),
        name="index_topk_bias",
    )(qi, kte, kto, wi)


def _attn_kernel(qt_ref, kt_ref, q_ref, k_ref, v_ref, *rest, fox):
    if fox:
        fq_ref, fk_ref, o_ref, m_sc, l_sc, acc_sc, s_sc, p_sc, alpha_sc = rest
    else:
        bias_ref, o_ref, m_sc, l_sc, acc_sc, s_sc, p_sc, alpha_sc = rest
    p_id = pl.program_id(1)
    qb = qt_ref[p_id]
    kb = kt_ref[p_id]
    tq, tk = q_ref.shape[0], k_ref.shape[0]
    rc = ATT_RC
    qk_scale = (HEAD_DIM ** -0.5) * LOG2E

    @pl.when(kb == 0)
    def _():
        m_sc[...] = jnp.full(m_sc.shape, NEG_BIG, jnp.float32)
        l_sc[...] = jnp.zeros(l_sc.shape, jnp.float32)
        acc_sc[...] = jnp.zeros(acc_sc.shape, jnp.float32)

    if fox:
        key_bias = (fq_ref[:, 0:1] - fk_ref[...]) * LOG2E
        col_minus_row = (lax.broadcasted_iota(jnp.int32, (rc, tk), 1)
                         - lax.broadcasted_iota(jnp.int32, (rc, tk), 0))

    def logits_stage(t, masked):
        rows = slice(t * rc, (t + 1) * rc)
        s = lax.dot_general(q_ref[rows, :], k_ref[...], (((1,), (1,)), ((), ())),
                            preferred_element_type=jnp.float32) * qk_scale
        if fox:
            s = s + key_bias
            if masked:
                s = jnp.where(col_minus_row <= t * rc + (qb * tq - kb * tk), s, NEG_BIG)
        else:
            s = s + bias_ref[rows, :].astype(jnp.float32)
        s_sc[t % 2] = s

    def softmax_stage(t):
        rows = slice(t * rc, (t + 1) * rc)
        s = s_sc[t % 2]
        m_prev = m_sc[rows, :]
        m_new = jnp.maximum(m_prev, jnp.max(s, axis=1, keepdims=True))
        alpha = jnp.exp2(m_prev - m_new)
        p = jnp.exp2(s - m_new)
        l_sc[rows, :] = alpha * l_sc[rows, :] + jnp.sum(p, axis=1, keepdims=True)
        m_sc[rows, :] = m_new
        alpha_sc[t % 2] = alpha
        p_sc[t % 2] = p.astype(p_sc.dtype)

    def value_stage(t):
        rows = slice(t * rc, (t + 1) * rc)
        acc_sc[rows, :] = alpha_sc[t % 2] * acc_sc[rows, :] + jnp.dot(
            p_sc[t % 2], v_ref[...], preferred_element_type=jnp.float32)

    def all_chunks(masked):
        n_rc = tq // rc
        for t in range(n_rc + 2):
            if t < n_rc:
                logits_stage(t, masked)
            if 1 <= t <= n_rc:
                softmax_stage(t - 1)
            if t >= 2:
                value_stage(t - 2)

    if fox:
        on_diag = (kb + 1) * tk > qb * tq + 1
        pl.when(on_diag)(functools.partial(all_chunks, True))
        pl.when(jnp.logical_not(on_diag))(functools.partial(all_chunks, False))
    else:
        all_chunks(False)

    @pl.when((kb + 1) * tk == (qb + 1) * tq)
    def _():
        o_ref[...] = (acc_sc[...] / l_sc[...]).astype(o_ref.dtype)


def _attention(qk, v, q_col, k_col, v_col, *, frow=None, bias=None):
    s = qk.shape[0]
    n_heads = N_FOX_HEADS
    tq, tk = ATT_TQ, ATT_TK
    nq = s // tq
    pairs = [(a, b) for a in range(nq) for b in range((a + 1) * tq // tk)]
    qt = jnp.asarray([a for a, _ in pairs], jnp.int32)
    kt = jnp.asarray([b for _, b in pairs], jnp.int32)
    fox = bias is None
    in_specs = [
        pl.BlockSpec((tq, HEAD_DIM), lambda h, p, qt, kt: (qt[p], q_col + h)),
        pl.BlockSpec((tk, HEAD_DIM), lambda h, p, qt, kt: (kt[p], k_col + h)),
        pl.BlockSpec((tk, HEAD_DIM), lambda h, p, qt, kt: (kt[p], v_col + h)),
    ]
    if fox:
        in_specs += [pl.BlockSpec((None, 1, tq), lambda h, p, qt, kt: (h, 0, qt[p])),
                     pl.BlockSpec((None, 1, tk), lambda h, p, qt, kt: (h, 0, kt[p]))]
        operands = (qk, qk, v, frow, frow)
    else:
        in_specs += [pl.BlockSpec((tq, tk), lambda h, p, qt, kt: (qt[p], kt[p]))]
        operands = (qk, qk, v, bias)
    grid_spec = pltpu.PrefetchScalarGridSpec(
        num_scalar_prefetch=2,
        grid=(n_heads, len(pairs)),
        in_specs=in_specs,
        out_specs=pl.BlockSpec((tq, HEAD_DIM), lambda h, p, qt, kt: (qt[p], h)),
        scratch_shapes=[pltpu.VMEM((tq, 1), jnp.float32),
                        pltpu.VMEM((tq, 1), jnp.float32),
                        pltpu.VMEM((tq, HEAD_DIM), jnp.float32),
                        pltpu.VMEM((2, ATT_RC, tk), jnp.float32),
                        pltpu.VMEM((2, ATT_RC, tk), jnp.bfloat16),
                        pltpu.VMEM((2, ATT_RC, 1), jnp.float32)],
    )
    return pl.pallas_call(
        functools.partial(_attn_kernel, fox=fox),
        grid_spec=grid_spec,
        out_shape=jax.ShapeDtypeStruct((s, n_heads * HEAD_DIM), jnp.bfloat16),
        compiler_params=_cparams(2),
        name="fox_attention" if fox else "dsa_attention",
    )(qt, kt, *operands)


def _rope_tables(s, rot, period):
    half = rot // 2
    pos = jnp.arange(s, dtype=jnp.float32)
    inv = jnp.power(jnp.float32(ROPE_THETA), -2.0 * jnp.arange(half, dtype=jnp.float32) / rot)
    ang = pos[:, None] * inv[None, :]
    cos, sin = jnp.cos(ang), jnp.sin(ang)
    ones = jnp.ones((s, period - rot), jnp.float32)
    zeros = lambda n: jnp.zeros((s, n), jnp.float32)
    c = jnp.concatenate([cos, cos, ones], axis=1)
    s_up = jnp.concatenate([-sin, zeros(period - half)], axis=1)
    s_dn = jnp.concatenate([zeros(half), sin, zeros(period - rot)], axis=1)
    reps = LANES // period
    return tuple(jnp.tile(t, (1, reps)) for t in (c, s_up, s_dn))


def _pad_cols(w, n):
    return jnp.pad(w, ((0, 0), (0, n - w.shape[1])))


def _pad_row(v, n):
    return jnp.pad(v, (0, n - v.shape[0])).reshape(1, n)


def kernel(x, p, g_ffn1, w1_gate, w1_up, w1_down, g_mix, w_in, b_f, g_qa, g_ka, g_qb, g_kb, g_ik,
           w_br_fox, w_br_dsa, w_o, g_ffn2, w2_gate, w2_up, w2_down, g_ple, w_ple_gate, w_ple_proj):
    b, s, d = x.shape
    assert b == 1
    bf16 = jnp.bfloat16
    topk = min(INDEX_TOPK_MAX, s // 4)
    idx_scale = (N_IDX_HEADS ** -0.5) * (IDX_DIM ** -0.5)
    rope_b = _rope_tables(s, HEAD_DIM // ROT_FRAC_DIV, HEAD_DIM)
    rope_i = _rope_tables(s, IDX_DIM // ROT_FRAC_DIV, IDX_DIM)
    half_b = HEAD_DIM // ROT_FRAC_DIV // 2
    half_i = IDX_DIM // ROT_FRAC_DIV // 2
    row_tab = lambda t: (t, (MM_TM, LANES), lambda i, j: (i, 0))

    def ffn(h, g, wg, wu, wd, tag):
        u = _rmsnorm(h, g)
        a = _matmul([u], [wg.astype(bf16), wu.astype(bf16)], [(0, 0), (0, 1)], [],
                    _ep_swiglu, bf16, name=tag + "_swiglu")
        return _matmul([a], [wd.astype(bf16)], [(0, 0)],
                       [(h, (MM_TM, MM_TN), lambda i, j: (i, j))],
                       functools.partial(_ep_residual, scale=0.5), jnp.float32, name=tag + "_down")

    h = x.reshape(s, d)
    for i in range(DEPTH):
        h = ffn(h, g_ffn1[i], w1_gate[i], w1_up[i], w1_down[i], "ffn1")

        u = _rmsnorm(h, g_mix[i])
        w = w_in[i]
        off_fa = 3 * FOX_W
        off_b = off_fa + N_FOX_HEADS
        off_ki = off_b + 3 * DSA_W + IDX_Q_W
        off_wi = off_ki + IDX_DIM
        off_g = off_wi + N_IDX_HEADS
        w_a = w[:, :off_fa].astype(bf16)
        w_b = w[:, off_b:off_ki].astype(bf16)
        w_g = w[:, off_g:].astype(bf16)
        w_small = jnp.concatenate([_pad_cols(w[:, off_fa:off_b], LANES),
                                   _pad_cols(w[:, off_ki:off_wi], LANES),
                                   _pad_cols(w[:, off_wi:off_g], LANES)], axis=1).astype(bf16)
        gain_a = jnp.concatenate([jnp.tile(g_qa[i], N_FOX_HEADS), jnp.tile(g_ka[i], N_FOX_HEADS)])
        gain_b = jnp.concatenate([jnp.tile(g_qb[i], N_DSA_HEADS), jnp.tile(g_kb[i], N_DSA_HEADS)])
        col_vec = lambda v: (v.reshape(1, -1), (1, MM_TN), lambda i, j: (0, j))
        blk = lambda cols: cols // MM_TN

        qk_a = _matmul([u], [w_a], [(0, 0)], [col_vec(gain_a)],
                       functools.partial(_ep_headnorm, rope_half=0), bf16,
                       name="proj_qk_fox", n=2 * FOX_W)
        v_a = _matmul([u], [w_a], [(0, 0)], [], _ep_cast, bf16,
                      name="proj_v_fox", n=FOX_W, rhs_off=[blk(2 * FOX_W)])
        qk_b = _matmul([u], [w_b], [(0, 0)], [col_vec(gain_b)] + [row_tab(t) for t in rope_b],
                       functools.partial(_ep_headnorm, rope_half=half_b), bf16,
                       name="proj_qk_dsa", n=2 * DSA_W)
        v_b = _matmul([u], [w_b], [(0, 0)], [], _ep_cast, bf16,
                      name="proj_v_dsa", n=DSA_W, rhs_off=[blk(2 * DSA_W)])
        q_i = _matmul([u], [w_b], [(0, 0)], [row_tab(t) for t in rope_i],
                      functools.partial(_ep_rope, rope_half=half_i), bf16,
                      name="proj_qidx", n=IDX_Q_W, rhs_off=[blk(3 * DSA_W)])
        gates = _matmul([u], [w_g], [(0, 0)], [], _ep_sigmoid, bf16, name="proj_gates")
        z_small = _matmul([u], [w_small], [(0, 0)], [], _ep_cast, jnp.float32,
                          name="proj_small", tn=3 * LANES)

        f_cum, ki_e, ki_o, wi_s = _small(z_small, _pad_row(b_f[i], LANES), _pad_row(g_ik[i], LANES),
                                         *rope_i, idx_scale)

        f_t = f_cum[:, :N_FOX_HEADS].T
        o_a = _attention(qk_a, v_a, 0, N_FOX_HEADS, 0, frow=f_t[:, None, :])

        bias = _index_bias(q_i, ki_e.T, ki_o.T, wi_s, topk)
        o_b = _attention(qk_b, v_b, 0, N_DSA_HEADS, 0, bias=bias)

        n_gate_blocks = D_MODEL // MM_TN
        y = _matmul([o_a, o_b], [w_br_fox[i].astype(bf16), w_br_dsa[i].astype(bf16)],
                    [(0, 0), (1, 1)],
                    [(gates, (MM_TM, MM_TN), lambda i, j: (i, j)),
                     (gates, (MM_TM, MM_TN), lambda i, j: (i, j + n_gate_blocks))],
                    _ep_merge, bf16, name="branch_merge")
        h = _matmul([y], [w_o[i].astype(bf16)], [(0, 0)],
                    [(h, (MM_TM, MM_TN), lambda i, j: (i, j))],
                    functools.partial(_ep_residual, scale=1.0), jnp.float32, name="out_proj")

        h = ffn(h, g_ffn2[i], w2_gate[i], w2_up[i], w2_down[i], "ffn2")

        u = _rmsnorm(h, g_ple[i])
        h = _matmul([u, p[i].reshape(s, PLE_DIM).astype(bf16)],
                    [w_ple_gate[i].astype(bf16), w_ple_proj[i].astype(bf16)],
                    [(0, 0), (1, 1)],
                    [(h, (MM_TM, MM_TN), lambda i, j: (i, j))],
                    _ep_ple, jnp.float32, name="ple")
    return h.reshape(b, s, d)
```

```python
import functools
import math

import jax
import jax.numpy as jnp
from jax import lax
from jax.experimental import pallas as pl
from jax.experimental.pallas import tpu as pltpu

D_MODEL = 4096
SEQ = 8192
DEPTH = 4
HEAD_DIM = 128
N_FOX_HEADS = 16
N_DSA_HEADS = 16
N_IDX_HEADS = 32
IDX_DIM = 64
INDEX_TOPK_MAX = 256
D_FF = 3584
PLE_DIM = 256
ROPE_THETA = 500000.0
ROT_FRAC_DIV = 4
NORM_EPS = 1e-6
NEG_BIG = -1e30
LOG2E = 1.4426950408889634
FOX_W = N_FOX_HEADS * HEAD_DIM
DSA_W = N_DSA_HEADS * HEAD_DIM
IDX_Q_W = N_IDX_HEADS * IDX_DIM

LANES = 128
VMEM_LIMIT = 56 * 1024 * 1024

MM_TM = 1024
MM_TN = 512
NORM_TM = 512
CAST_ROWS = 512
ATT_TQ = 1024
ATT_TK = 1024
ATT_RC = 256
IDX_TQ = 256
IDX_TC = 512

_INT_MIN = -(2 ** 31)


def _cparams(n_axes):
    return pltpu.CompilerParams(
        dimension_semantics=("arbitrary",) * n_axes,
        vmem_limit_bytes=VMEM_LIMIT,
    )


def _rmsnorm_kernel(x_ref, g_ref, o_ref):
    x = x_ref[...]
    ms = jnp.mean(x * x, axis=-1, keepdims=True)
    o_ref[...] = (x * lax.rsqrt(ms + NORM_EPS) * g_ref[...]).astype(o_ref.dtype)


def _rmsnorm(x, g):
    m, d = x.shape
    return pl.pallas_call(
        _rmsnorm_kernel,
        grid=(m // NORM_TM,),
        in_specs=[pl.BlockSpec((NORM_TM, d), lambda i: (i, 0)),
                  pl.BlockSpec((1, d), lambda i: (0, 0))],
        out_specs=pl.BlockSpec((NORM_TM, d), lambda i: (i, 0)),
        out_shape=jax.ShapeDtypeStruct((m, d), jnp.bfloat16),
        compiler_params=_cparams(1),
        name="rmsnorm",
    )(x, g.reshape(1, d))


def _mm_kernel(*refs, n_lhs, n_rhs, pairs, n_extra, epilogue):
    lhs = refs[:n_lhs]
    rhs = refs[n_lhs:n_lhs + n_rhs]
    extra = refs[n_lhs + n_rhs:n_lhs + n_rhs + n_extra]
    out = refs[n_lhs + n_rhs + n_extra]
    accs = [jnp.dot(lhs[a][...], rhs[b][...], preferred_element_type=jnp.float32)
            for a, b in pairs]
    epilogue(accs, extra, out)


def _matmul(lhs, rhs, pairs, extras, epilogue, out_dtype, *, layer, name, tn=MM_TN, n=None,
            rhs_off=None):
    m = lhs[0].shape[0]
    n = rhs[0].shape[2] if n is None else n
    rhs_off = [0] * len(rhs) if rhs_off is None else rhs_off
    tm = MM_TM
    in_specs = []
    for a in lhs:
        in_specs.append(pl.BlockSpec((tm, a.shape[1]), lambda i, j: (i, 0)))
    for w, off in zip(rhs, rhs_off):
        in_specs.append(pl.BlockSpec((None, w.shape[1], tn),
                                     lambda i, j, off=off: (layer, 0, j + off)))
    for _, bs, im in extras:
        in_specs.append(pl.BlockSpec(bs, im))
    kern = functools.partial(_mm_kernel, n_lhs=len(lhs), n_rhs=len(rhs), pairs=pairs,
                             n_extra=len(extras), epilogue=epilogue)
    return pl.pallas_call(
        kern,
        grid=(m // tm, n // tn),
        in_specs=in_specs,
        out_specs=pl.BlockSpec((tm, tn), lambda i, j: (i, j)),
        out_shape=jax.ShapeDtypeStruct((m, n), out_dtype),
        compiler_params=_cparams(2),
        name=name,
    )(*lhs, *rhs, *[e[0] for e in extras])


def _ep_cast(accs, extra, out):
    out[...] = accs[0].astype(out.dtype)


def _ep_sigmoid(accs, extra, out):
    out[...] = jax.nn.sigmoid(accs[0]).astype(out.dtype)


def _ep_swiglu(accs, extra, out):
    out[...] = (jax.nn.silu(accs[0]) * accs[1]).astype(out.dtype)


def _ep_residual(accs, extra, out, *, scale):
    out[...] = extra[0][...] + scale * accs[0]


def _ep_merge(accs, extra, out):
    ga = extra[0][...].astype(jnp.float32)
    gb = extra[1][...].astype(jnp.float32)
    out[...] = (ga * accs[0] + gb * accs[1]).astype(out.dtype)


def _ep_ple(accs, extra, out):
    out[...] = extra[0][...] + jax.nn.sigmoid(accs[0]) * accs[1]


def _rope_lanes(y, cos, s_up, s_dn, half):
    up = pltpu.roll(y, LANES - half, 1)
    dn = pltpu.roll(y, half, 1)
    return y * cos + up * s_up + dn * s_dn


def _ep_headnorm(accs, extra, out, *, rope):
    acc = accs[0]
    gain = extra[0][...]
    for g in range(acc.shape[1] // LANES):
        sl = slice(g * LANES, (g + 1) * LANES)
        x = acc[:, sl]
        ms = jnp.mean(x * x, axis=-1, keepdims=True)
        y = x * lax.rsqrt(ms + NORM_EPS) * gain[:, sl]
        if rope:
            y = y * extra[1][...] + pltpu.roll(y, LANES // 2, 1) * extra[2][...]
        out[:, sl] = y.astype(out.dtype)


def _ep_rope(accs, extra, out, *, rope_half):
    acc = accs[0]
    for g in range(acc.shape[1] // LANES):
        sl = slice(g * LANES, (g + 1) * LANES)
        y = _rope_lanes(acc[:, sl], extra[0][...], extra[1][...], extra[2][...], rope_half)
        out[:, sl] = y.astype(out.dtype)


def _small_kernel(z_ref, bf_ref, gik_ref, cos_ref, sup_ref, sdn_ref,
                  f_ref, kie_ref, kio_ref, wi_ref, carry_sc, *, idx_scale):
    s = z_ref.shape[0]

    @pl.when(pl.program_id(0) == 0)
    def _():
        carry_sc[...] = jnp.zeros(carry_sc.shape, jnp.float32)

    x = z_ref[:, 0:LANES] + bf_ref[...]
    logf = jnp.minimum(x, 0.0) - jnp.log1p(jnp.exp(-jnp.abs(x)))
    row = lax.broadcasted_iota(jnp.int32, (s, LANES), 0)
    d = 1
    while d < s:
        logf = logf + jnp.where(row >= d, pltpu.roll(logf, d, 0), 0.0)
        d *= 2
    f_ref[...] = logf + carry_sc[0:1, :]
    carry_sc[0:1, :] = f_ref[s - 1:s, :]
    k = z_ref[:, LANES:2 * LANES]
    ms = jnp.sum(k * k, axis=-1, keepdims=True) * (1.0 / IDX_DIM)
    k = k * lax.rsqrt(ms + NORM_EPS) * gik_ref[...]
    k = _rope_lanes(k, cos_ref[...], sup_ref[...], sdn_ref[...], IDX_DIM // ROT_FRAC_DIV // 2)
    lane = lax.broadcasted_iota(jnp.int32, (s, LANES), 1)
    k = jnp.where(lane < IDX_DIM, k, 0.0)
    kie_ref[...] = k.astype(kie_ref.dtype)
    kio_ref[...] = pltpu.roll(k, IDX_DIM, 1).astype(kio_ref.dtype)
    wi_ref[...] = z_ref[:, 2 * LANES:3 * LANES] * idx_scale


def _small(z, bf, gik, cos, sup, sdn, idx_scale):
    s = z.shape[0]
    tm = MM_TM
    rows = lambda w: pl.BlockSpec((tm, w), lambda i: (i, 0))
    vec = pl.BlockSpec((1, LANES), lambda i: (0, 0))
    return pl.pallas_call(
        functools.partial(_small_kernel, idx_scale=idx_scale),
        grid=(s // tm,),
        in_specs=[rows(3 * LANES), vec, vec, rows(LANES), rows(LANES), rows(LANES)],
        out_specs=[rows(LANES)] * 4,
        out_shape=[jax.ShapeDtypeStruct((s, LANES), jnp.float32),
                   jax.ShapeDtypeStruct((s, LANES), jnp.bfloat16),
                   jax.ShapeDtypeStruct((s, LANES), jnp.bfloat16),
                   jax.ShapeDtypeStruct((s, LANES), jnp.float32)],
        scratch_shapes=[pltpu.VMEM((8, LANES), jnp.float32)],
        compiler_params=_cparams(1),
        name="small_heads",
    )(z, bf, gik, cos, sup, sdn)


def _sortable(x):
    b = pltpu.bitcast(x, jnp.int32)
    return jnp.where(b < 0, b ^ jnp.int32(0x7FFFFFFF), b)


def _index_kernel(qi_ref, kte_ref, kto_ref, wi_ref, bias_ref, keys_sc, wb_sc, *, topk):
    tq, s = bias_ref.shape
    tc = IDX_TC
    i = pl.program_id(0)
    row0 = i * tq
    n_chunks = (row0 + tq + tc - 1) // tc
    n_tiles = tc // LANES

    for j in range(N_IDX_HEADS):
        wb_sc[j] = jnp.broadcast_to(wi_ref[:, j:j + 1], (tq, LANES))

    row_ids = row0 + lax.broadcasted_iota(jnp.int32, (tq, tc), 0)
    col_iota = lax.broadcasted_iota(jnp.int32, (tq, tc), 1)

    def score_chunk(c, carry):
        col0 = pl.multiple_of(c * tc, tc)
        kte = kte_ref[:, pl.ds(col0, tc)]
        kto = kto_ref[:, pl.ds(col0, tc)]
        score = jnp.zeros((tq, tc), jnp.float32)
        for p in range(N_IDX_HEADS // 2):
            qp = qi_ref[:, p * LANES:(p + 1) * LANES]
            re = jnp.maximum(jnp.dot(qp, kte, preferred_element_type=jnp.float32), 0.0)
            ro = jnp.maximum(jnp.dot(qp, kto, preferred_element_type=jnp.float32), 0.0)
            we = jnp.concatenate([wb_sc[2 * p]] * n_tiles, axis=1)
            wo = jnp.concatenate([wb_sc[2 * p + 1]] * n_tiles, axis=1)
            score = score + re * we + ro * wo
        causal = (col0 + col_iota) <= row_ids
        score = jnp.where(causal, score, NEG_BIG)
        keys_sc[:, pl.ds(col0, tc)] = _sortable(score)
        return carry

    lax.fori_loop(0, n_chunks, score_chunk, 0)

    @pl.when(n_chunks % 2 == 1)
    def _():
        col0 = pl.multiple_of(n_chunks * tc, tc)
        keys_sc[:, pl.ds(col0, tc)] = _sortable(jnp.full((tq, tc), NEG_BIG, jnp.float32))

    tcc = 2 * tc

    def bit_step(state):
        b, thr, kept, _ = state
        cand = thr + lax.shift_left(jnp.int32(1), 31 - b)

        def count_chunk(c, cnt):
            col0 = pl.multiple_of(c * tcc, tcc)
            k = keys_sc[:, pl.ds(col0, tcc)]
            for t in range(tcc // LANES):
                ge = k[:, t * LANES:(t + 1) * LANES] >= cand
                cnt = cnt + jnp.where(ge, 1.0, 0.0)
            return cnt

        cnt = lax.fori_loop(0, (n_chunks + 1) // 2, count_chunk,
                            jnp.zeros((tq, LANES), jnp.float32))
        total = jnp.sum(cnt, axis=1, keepdims=True)
        accept = total >= topk
        thr = jnp.where(accept, cand, thr)
        kept = jnp.where(accept, total, kept)
        unresolved = jnp.max(jnp.where(kept == topk, 0.0, 1.0))
        return b + 1, thr, kept, unresolved

    def search_on(state):
        b, _, _, unresolved = state
        return jnp.logical_and(b < 32, unresolved > 0.5)

    _, thr, _, _ = lax.while_loop(
        search_on, bit_step,
        (jnp.int32(0), jnp.full((tq, LANES), _INT_MIN, jnp.int32),
         jnp.full((tq, LANES), float(2 * s), jnp.float32), jnp.float32(1.0)))

    def bias_chunk(c, carry):
        col0 = pl.multiple_of(c * tc, tc)
        k = keys_sc[:, pl.ds(col0, tc)]
        causal = (col0 + col_iota) <= row_ids
        thr_w = jnp.concatenate([thr] * n_tiles, axis=1)
        keep = jnp.logical_and(k >= thr_w, causal)
        bias_ref[:, pl.ds(col0, tc)] = jnp.where(keep, 0.0, NEG_BIG).astype(bias_ref.dtype)
        return carry

    lax.fori_loop(0, n_chunks, bias_chunk, 0)

    def fill_chunk(c, carry):
        col0 = pl.multiple_of(c * tc, tc)
        bias_ref[:, pl.ds(col0, tc)] = jnp.full((tq, tc), NEG_BIG, bias_ref.dtype)
        return carry

    lax.fori_loop(n_chunks, s // tc, fill_chunk, 0)


def _index_bias(qi, kte, kto, wi, topk):
    s = qi.shape[0]
    tq = IDX_TQ
    return pl.pallas_call(
        functools.partial(_index_kernel, topk=topk),
        grid=(s // tq,),
        in_specs=[pl.BlockSpec((tq, IDX_Q_W), lambda i: (i, 0)),
                  pl.BlockSpec((LANES, s), lambda i: (0, 0)),
                  pl.BlockSpec((LANES, s), lambda i: (0, 0)),
                  pl.BlockSpec((tq, LANES), lambda i: (i, 0))],
        out_specs=pl.BlockSpec((tq, s), lambda i: (i, 0)),
        out_shape=jax.ShapeDtypeStruct((s, s), jnp.bfloat16),
        scratch_shapes=[pltpu.VMEM((tq, s), jnp.int32),
                        pltpu.VMEM((N_IDX_HEADS, tq, LANES), jnp.float32)],
        compiler_params=_cparams(1),
        name="index_topk_bias",
    )(qi, kte, kto, wi)


def _attn_kernel(qt_ref, kt_ref, q_ref, k_ref, v_ref, *rest, fox):
    if fox:
        fq_ref, fk_ref, o_ref, m_sc, l_sc, acc_sc, s_sc, p_sc, alpha_sc = rest
    else:
        bias_ref, o_ref, m_sc, l_sc, acc_sc, s_sc, p_sc, alpha_sc = rest
    p_id = pl.program_id(1)
    qb = qt_ref[p_id]
    kb = kt_ref[p_id]
    tq, tk = q_ref.shape[0], k_ref.shape[0]
    rc = ATT_RC
    qk_scale = (HEAD_DIM ** -0.5) * LOG2E

    @pl.when(kb == 0)
    def _():
        m_sc[...] = jnp.full(m_sc.shape, NEG_BIG, jnp.float32)
        l_sc[...] = jnp.zeros(l_sc.shape, jnp.float32)
        acc_sc[...] = jnp.zeros(acc_sc.shape, jnp.float32)

    if fox:
        key_bias = (fq_ref[:, 0:1] - fk_ref[...]) * LOG2E
        col_minus_row = (lax.broadcasted_iota(jnp.int32, (rc, tk), 1)
                         - lax.broadcasted_iota(jnp.int32, (rc, tk), 0))

    def key_cols(t, diag):
        return (t + 1) * rc if diag else tk

    def logits_stage(t, diag):
        rows = slice(t * rc, (t + 1) * rc)
        kc = key_cols(t, diag)
        s = lax.dot_general(q_ref[rows, :], k_ref[0:kc, :], (((1,), (1,)), ((), ())),
                            preferred_element_type=jnp.float32) * qk_scale
        if fox:
            s = s + key_bias[:, 0:kc]
            if diag:
                s = jnp.where(col_minus_row[:, 0:kc] <= t * rc, s, NEG_BIG)
        else:
            s = s + bias_ref[rows, 0:kc].astype(jnp.float32)
        s_sc[t % 2, :, 0:kc] = s

    def softmax_stage(t, diag):
        rows = slice(t * rc, (t + 1) * rc)
        kc = key_cols(t, diag)
        s = s_sc[t % 2, :, 0:kc]
        m_prev = m_sc[rows, :]
        m_new = jnp.maximum(m_prev, jnp.max(s, axis=1, keepdims=True))
        alpha = jnp.exp2(m_prev - m_new)
        p = jnp.exp2(s - m_new)
        l_sc[rows, :] = alpha * l_sc[rows, :] + jnp.sum(p, axis=1, keepdims=True)
        m_sc[rows, :] = m_new
        alpha_sc[t % 2] = alpha
        p_sc[t % 2, :, 0:kc] = p.astype(p_sc.dtype)

    def value_stage(t, diag):
        rows = slice(t * rc, (t + 1) * rc)
        kc = key_cols(t, diag)
        acc_sc[rows, :] = alpha_sc[t % 2] * acc_sc[rows, :] + jnp.dot(
            p_sc[t % 2, :, 0:kc], v_ref[0:kc, :], preferred_element_type=jnp.float32)

    def all_chunks(diag):
        n_rc = tq // rc
        for t in range(n_rc + 2):
            if t < n_rc:
                logits_stage(t, diag)
            if 1 <= t <= n_rc:
                softmax_stage(t - 1, diag)
            if t >= 2:
                value_stage(t - 2, diag)

    pl.when(kb == qb)(functools.partial(all_chunks, True))
    pl.when(kb != qb)(functools.partial(all_chunks, False))

    @pl.when(kb == qb)
    def _():
        o_ref[...] = (acc_sc[...] / l_sc[...]).astype(o_ref.dtype)


def _attention(qk, v, q_col, k_col, v_col, *, frow=None, bias=None):
    s = qk.shape[0]
    n_heads = N_FOX_HEADS
    tq, tk = ATT_TQ, ATT_TK
    assert tq == tk
    nq = s // tq
    pairs = [(a, b) for a in range(nq) for b in range(a + 1)]
    qt = jnp.asarray([a for a, _ in pairs], jnp.int32)
    kt = jnp.asarray([b for _, b in pairs], jnp.int32)
    fox = bias is None
    in_specs = [
        pl.BlockSpec((tq, HEAD_DIM), lambda h, p, qt, kt: (qt[p], q_col + h)),
        pl.BlockSpec((tk, HEAD_DIM), lambda h, p, qt, kt: (kt[p], k_col + h)),
        pl.BlockSpec((tk, HEAD_DIM), lambda h, p, qt, kt: (kt[p], v_col + h)),
    ]
    if fox:
        in_specs += [pl.BlockSpec((None, 1, tq), lambda h, p, qt, kt: (h, 0, qt[p])),
                     pl.BlockSpec((None, 1, tk), lambda h, p, qt, kt: (h, 0, kt[p]))]
        operands = (qk, qk, v, frow, frow)
    else:
        in_specs += [pl.BlockSpec((tq, tk), lambda h, p, qt, kt: (qt[p], kt[p]))]
        operands = (qk, qk, v, bias)
    grid_spec = pltpu.PrefetchScalarGridSpec(
        num_scalar_prefetch=2,
        grid=(n_heads, len(pairs)),
        in_specs=in_specs,
        out_specs=pl.BlockSpec((tq, HEAD_DIM), lambda h, p, qt, kt: (qt[p], h)),
        scratch_shapes=[pltpu.VMEM((tq, 1), jnp.float32),
                        pltpu.VMEM((tq, 1), jnp.float32),
                        pltpu.VMEM((tq, HEAD_DIM), jnp.float32),
                        pltpu.VMEM((2, ATT_RC, tk), jnp.float32),
                        pltpu.VMEM((2, ATT_RC, tk), jnp.bfloat16),
                        pltpu.VMEM((2, ATT_RC, 1), jnp.float32)],
    )
    return pl.pallas_call(
        functools.partial(_attn_kernel, fox=fox),
        grid_spec=grid_spec,
        out_shape=jax.ShapeDtypeStruct((s, n_heads * HEAD_DIM), jnp.bfloat16),
        compiler_params=_cparams(2),
        name="fox_attention" if fox else "dsa_attention",
    )(qt, kt, *operands)


def _cast_kernel(x_ref, o_ref):
    o_ref[...] = x_ref[...].astype(o_ref.dtype)


def _cast_bf16(w):
    l, k, n = w.shape
    rows = math.gcd(CAST_ROWS, l * k)
    out = pl.pallas_call(
        _cast_kernel,
        grid=(l * k // rows,),
        in_specs=[pl.BlockSpec((rows, n), lambda i: (i, 0))],
        out_specs=pl.BlockSpec((rows, n), lambda i: (i, 0)),
        out_shape=jax.ShapeDtypeStruct((l * k, n), jnp.bfloat16),
        compiler_params=_cparams(1),
        name="cast_bf16",
    )(w.reshape(l * k, n))
    return out.reshape(l, k, n)


def _rope_angles(s, rot):
    half = rot // 2
    pos = jnp.arange(s, dtype=jnp.float32)
    inv = jnp.power(jnp.float32(ROPE_THETA), -2.0 * jnp.arange(half, dtype=jnp.float32) / rot)
    ang = pos[:, None] * inv[None, :]
    return jnp.cos(ang), jnp.sin(ang)


def _rope_tables(s, rot, period):
    half = rot // 2
    cos, sin = _rope_angles(s, rot)
    ones = jnp.ones((s, period - rot), jnp.float32)
    zeros = lambda n: jnp.zeros((s, n), jnp.float32)
    c = jnp.concatenate([cos, cos, ones], axis=1)
    s_up = jnp.concatenate([-sin, zeros(period - half)], axis=1)
    s_dn = jnp.concatenate([zeros(half), sin, zeros(period - rot)], axis=1)
    reps = LANES // period
    return tuple(jnp.tile(t, (1, reps)) for t in (c, s_up, s_dn))


def _rope_perm(a):
    h = HEAD_DIM // ROT_FRAC_DIV // 2
    half = LANES // 2
    a4 = a.reshape(a.shape[:-1] + (a.shape[-1] // HEAD_DIM, HEAD_DIM))
    a4 = jnp.concatenate([a4[..., 0:h], a4[..., 2 * h:half + h], a4[..., h:2 * h],
                          a4[..., half + h:]], axis=-1)
    return a4.reshape(a.shape)


def _rope_tables_split(s):
    h = HEAD_DIM // ROT_FRAC_DIV // 2
    half = LANES // 2
    cos, sin = _rope_angles(s, 2 * h)
    ones = jnp.ones((s, half - h), jnp.float32)
    zeros = jnp.zeros((s, half - h), jnp.float32)
    return (jnp.concatenate([cos, ones, cos, ones], axis=1),
            jnp.concatenate([-sin, zeros, sin, zeros], axis=1))


def _pad_last(w, n):
    return jnp.pad(w, [(0, 0)] * (w.ndim - 1) + [(0, n - w.shape[-1])])


def _pad_row(v, n):
    return jnp.pad(v, (0, n - v.shape[0])).reshape(1, n)


def kernel(x, p, g_ffn1, w1_gate, w1_up, w1_down, g_mix, w_in, b_f, g_qa, g_ka, g_qb, g_kb, g_ik,
           w_br_fox, w_br_dsa, w_o, g_ffn2, w2_gate, w2_up, w2_down, g_ple, w_ple_gate, w_ple_proj):
    b, s, d = x.shape
    assert b == 1
    bf16 = jnp.bfloat16
    topk = min(INDEX_TOPK_MAX, s // 4)
    idx_scale = (N_IDX_HEADS ** -0.5) * (IDX_DIM ** -0.5)
    rope_b = _rope_tables_split(s)
    rope_i = _rope_tables(s, IDX_DIM // ROT_FRAC_DIV, IDX_DIM)
    half_i = IDX_DIM // ROT_FRAC_DIV // 2
    row_tab = lambda t: (t, (MM_TM, LANES), lambda i, j: (i, 0))
    res_tile = lambda h: (h, (MM_TM, MM_TN), lambda i, j: (i, j))
    blk = lambda cols: cols // MM_TN

    w1g, w1u, w1d = _cast_bf16(w1_gate), _cast_bf16(w1_up), _cast_bf16(w1_down)
    w2g, w2u, w2d = _cast_bf16(w2_gate), _cast_bf16(w2_up), _cast_bf16(w2_down)
    wbf, wbd, wo = _cast_bf16(w_br_fox), _cast_bf16(w_br_dsa), _cast_bf16(w_o)
    wpg, wpp = _cast_bf16(w_ple_gate), _cast_bf16(w_ple_proj)
    off_fa = 3 * FOX_W
    off_b = off_fa + N_FOX_HEADS
    off_ki = off_b + 3 * DSA_W + IDX_Q_W
    off_wi = off_ki + IDX_DIM
    off_g = off_wi + N_IDX_HEADS
    w_a = w_in[:, :, :off_fa].astype(bf16)
    w_b = jnp.concatenate([_rope_perm(w_in[:, :, off_b:off_b + 2 * DSA_W]),
                           w_in[:, :, off_b + 2 * DSA_W:off_ki]], axis=2).astype(bf16)
    w_g = w_in[:, :, off_g:].astype(bf16)
    w_small = jnp.concatenate([_pad_last(w_in[:, :, off_fa:off_b], LANES),
                               _pad_last(w_in[:, :, off_ki:off_wi], LANES),
                               _pad_last(w_in[:, :, off_wi:off_g], LANES)], axis=2).astype(bf16)
    p_bf = p.reshape(DEPTH, s, PLE_DIM).astype(bf16)

    def ffn(h, g, wg, wu, wd, layer, tag):
        u = _rmsnorm(h, g)
        a = _matmul([u], [wg, wu], [(0, 0), (0, 1)], [], _ep_swiglu, bf16,
                    layer=layer, name=tag + "_swiglu")
        return _matmul([a], [wd], [(0, 0)], [res_tile(h)],
                       functools.partial(_ep_residual, scale=0.5), jnp.float32,
                       layer=layer, name=tag + "_down")

    h = x.reshape(s, d)
    for i in range(DEPTH):
        h = ffn(h, g_ffn1[i], w1g, w1u, w1d, i, "ffn1")

        u = _rmsnorm(h, g_mix[i])
        gain_a = jnp.concatenate([jnp.tile(g_qa[i], N_FOX_HEADS), jnp.tile(g_ka[i], N_FOX_HEADS)])
        gain_b = _rope_perm(jnp.concatenate([jnp.tile(g_qb[i], N_DSA_HEADS),
                                             jnp.tile(g_kb[i], N_DSA_HEADS)]))
        col_vec = lambda v: (v.reshape(1, -1), (1, MM_TN), lambda i, j: (0, j))

        qk_a = _matmul([u], [w_a], [(0, 0)], [col_vec(gain_a)],
                       functools.partial(_ep_headnorm, rope=False), bf16,
                       layer=i, name="proj_qk_fox", n=2 * FOX_W)
        v_a = _matmul([u], [w_a], [(0, 0)], [], _ep_cast, bf16,
                      layer=i, name="proj_v_fox", n=FOX_W, rhs_off=[blk(2 * FOX_W)])
        qk_b = _matmul([u], [w_b], [(0, 0)], [col_vec(gain_b)] + [row_tab(t) for t in rope_b],
                       functools.partial(_ep_headnorm, rope=True), bf16,
                       layer=i, name="proj_qk_dsa", n=2 * DSA_W)
        v_b = _matmul([u], [w_b], [(0, 0)], [], _ep_cast, bf16,
                      layer=i, name="proj_v_dsa", n=DSA_W, rhs_off=[blk(2 * DSA_W)])
        q_i = _matmul([u], [w_b], [(0, 0)], [row_tab(t) for t in rope_i],
                      functools.partial(_ep_rope, rope_half=half_i), bf16,
                      layer=i, name="proj_qidx", n=IDX_Q_W, rhs_off=[blk(3 * DSA_W)])
        gates = _matmul([u], [w_g], [(0, 0)], [], _ep_sigmoid, bf16, layer=i, name="proj_gates")
        z_small = _matmul([u], [w_small], [(0, 0)], [], _ep_cast, jnp.float32,
                          layer=i, name="proj_small", tn=3 * LANES)

        f_cum, ki_e, ki_o, wi_s = _small(z_small, _pad_row(b_f[i], LANES), _pad_row(g_ik[i], LANES),
                                         *rope_i, idx_scale)

        f_t = f_cum[:, :N_FOX_HEADS].T
        o_a = _attention(qk_a, v_a, 0, N_FOX_HEADS, 0, frow=f_t[:, None, :])

        bias = _index_bias(q_i, ki_e.T, ki_o.T, wi_s, topk)
        o_b = _attention(qk_b, v_b, 0, N_DSA_HEADS, 0, bias=bias)

        n_gate_blocks = D_MODEL // MM_TN
        y = _matmul([o_a, o_b], [wbf, wbd], [(0, 0), (1, 1)],
                    [(gates, (MM_TM, MM_TN), lambda i, j: (i, j)),
                     (gates, (MM_TM, MM_TN), lambda i, j: (i, j + n_gate_blocks))],
                    _ep_merge, bf16, layer=i, name="branch_merge")
        h = _matmul([y], [wo], [(0, 0)], [res_tile(h)],
                    functools.partial(_ep_residual, scale=1.0), jnp.float32,
                    layer=i, name="out_proj")

        h = ffn(h, g_ffn2[i], w2g, w2u, w2d, i, "ffn2")

        u = _rmsnorm(h, g_ple[i])
        h = _matmul([u, p_bf[i]], [wpg, wpp], [(0, 0), (1, 1)], [res_tile(h)],
                    _ep_ple, jnp.float32, layer=i, name="ple")
    return h.reshape(b, s, d)
```

```python
import functools
import math

import jax
import jax.numpy as jnp
from jax import lax
from jax.experimental import pallas as pl
from jax.experimental.pallas import tpu as pltpu

D_MODEL = 4096
SEQ = 8192
DEPTH = 4
HEAD_DIM = 128
N_FOX_HEADS = 16
N_DSA_HEADS = 16
N_IDX_HEADS = 32
IDX_DIM = 64
INDEX_TOPK_MAX = 256
D_FF = 3584
PLE_DIM = 256
ROPE_THETA = 500000.0
ROT_FRAC_DIV = 4
NORM_EPS = 1e-6
NEG_BIG = -1e30
LOG2E = 1.4426950408889634
EXP2_ZERO_BELOW = -160.0
BF16_NORM_SLACK = 1.02
FOX_W = N_FOX_HEADS * HEAD_DIM
DSA_W = N_DSA_HEADS * HEAD_DIM
IDX_Q_W = N_IDX_HEADS * IDX_DIM

LANES = 128
VMEM_LIMIT = 56 * 1024 * 1024

MM_TM = 1024
MM_TN = 512
NORM_TM = 512
CAST_ROWS = 512
CAST_ROWS_WIDE = 128
ATT_TQ = 1024
ATT_TK = 1024
ATT_RC = 256
IDX_TQ = 256
IDX_TC = 512

_INT_MIN = -(2 ** 31)


def _cparams(n_axes):
    return pltpu.CompilerParams(
        dimension_semantics=("arbitrary",) * n_axes,
        vmem_limit_bytes=VMEM_LIMIT,
    )


def _rmsnorm_kernel(x_ref, g_ref, o_ref):
    x = x_ref[...]
    ms = jnp.mean(x * x, axis=-1, keepdims=True)
    o_ref[...] = (x * lax.rsqrt(ms + NORM_EPS) * g_ref[...]).astype(o_ref.dtype)


def _rmsnorm(x, g):
    m, d = x.shape
    return pl.pallas_call(
        _rmsnorm_kernel,
        grid=(m // NORM_TM,),
        in_specs=[pl.BlockSpec((NORM_TM, d), lambda i: (i, 0)),
                  pl.BlockSpec((1, d), lambda i: (0, 0))],
        out_specs=pl.BlockSpec((NORM_TM, d), lambda i: (i, 0)),
        out_shape=jax.ShapeDtypeStruct((m, d), jnp.bfloat16),
        compiler_params=_cparams(1),
        name="rmsnorm",
    )(x, g.reshape(1, d))


def _mm_kernel(*refs, n_lhs, n_rhs, pairs, n_extra, epilogue):
    lhs = refs[:n_lhs]
    rhs = refs[n_lhs:n_lhs + n_rhs]
    extra = refs[n_lhs + n_rhs:n_lhs + n_rhs + n_extra]
    out = refs[n_lhs + n_rhs + n_extra]
    accs = [jnp.dot(lhs[a][...], rhs[b][...], preferred_element_type=jnp.float32)
            for a, b in pairs]
    epilogue(accs, extra, out)


def _matmul(lhs, rhs, pairs, extras, epilogue, out_dtype, *, layer, name, tn=MM_TN, n=None,
            rhs_off=None):
    m = lhs[0].shape[0]
    n = rhs[0].shape[2] if n is None else n
    rhs_off = [0] * len(rhs) if rhs_off is None else rhs_off
    tm = MM_TM
    in_specs = []
    for a in lhs:
        in_specs.append(pl.BlockSpec((tm, a.shape[1]), lambda i, j: (i, 0)))
    for w, off in zip(rhs, rhs_off):
        in_specs.append(pl.BlockSpec((None, w.shape[1], tn),
                                     lambda i, j, off=off: (layer, 0, j + off)))
    for _, bs, im in extras:
        in_specs.append(pl.BlockSpec(bs, im))
    kern = functools.partial(_mm_kernel, n_lhs=len(lhs), n_rhs=len(rhs), pairs=pairs,
                             n_extra=len(extras), epilogue=epilogue)
    return pl.pallas_call(
        kern,
        grid=(m // tm, n // tn),
        in_specs=in_specs,
        out_specs=pl.BlockSpec((tm, tn), lambda i, j: (i, j)),
        out_shape=jax.ShapeDtypeStruct((m, n), out_dtype),
        compiler_params=_cparams(2),
        name=name,
    )(*lhs, *rhs, *[e[0] for e in extras])


def _ep_cast(accs, extra, out):
    out[...] = accs[0].astype(out.dtype)


def _ep_sigmoid(accs, extra, out):
    out[...] = jax.nn.sigmoid(accs[0]).astype(out.dtype)


def _ep_swiglu(accs, extra, out):
    out[...] = (jax.nn.silu(accs[0]) * accs[1]).astype(out.dtype)


def _ep_residual(accs, extra, out, *, scale):
    out[...] = extra[0][...] + scale * accs[0]


def _ep_merge(accs, extra, out):
    ga = extra[0][...].astype(jnp.float32)
    gb = extra[1][...].astype(jnp.float32)
    out[...] = (ga * accs[0] + gb * accs[1]).astype(out.dtype)


def _ep_ple(accs, extra, out):
    out[...] = extra[0][...] + jax.nn.sigmoid(accs[0]) * accs[1]


def _rope_lanes(y, cos, s_up, s_dn, half):
    up = pltpu.roll(y, LANES - half, 1)
    dn = pltpu.roll(y, half, 1)
    return y * cos + up * s_up + dn * s_dn


def _ep_headnorm(accs, extra, out, *, rope):
    acc = accs[0]
    gain = extra[0][...]
    groups = [slice(g * LANES, (g + 1) * LANES) for g in range(acc.shape[1] // LANES)]
    normed = []
    for sl in groups:
        x = acc[:, sl]
        ms = jnp.mean(x * x, axis=-1, keepdims=True)
        normed.append(x * lax.rsqrt(ms + NORM_EPS) * gain[:, sl])
    for sl, y in zip(groups, normed):
        if rope:
            y = y * extra[1][...] + pltpu.roll(y, LANES // 2, 1) * extra[2][...]
        out[:, sl] = y.astype(out.dtype)


def _ep_rope(accs, extra, out, *, rope_half):
    acc = accs[0]
    for g in range(acc.shape[1] // LANES):
        sl = slice(g * LANES, (g + 1) * LANES)
        y = _rope_lanes(acc[:, sl], extra[0][...], extra[1][...], extra[2][...], rope_half)
        out[:, sl] = y.astype(out.dtype)


def _small_kernel(z_ref, bf_ref, gik_ref, cos_ref, sup_ref, sdn_ref,
                  f_ref, kie_ref, kio_ref, wi_ref, carry_sc, *, idx_scale):
    s = z_ref.shape[0]

    @pl.when(pl.program_id(0) == 0)
    def _():
        carry_sc[...] = jnp.zeros(carry_sc.shape, jnp.float32)

    x = z_ref[:, 0:LANES] + bf_ref[...]
    logf = jnp.minimum(x, 0.0) - jnp.log1p(jnp.exp(-jnp.abs(x)))
    row = lax.broadcasted_iota(jnp.int32, (s, LANES), 0)
    d = 1
    while d < s:
        logf = logf + jnp.where(row >= d, pltpu.roll(logf, d, 0), 0.0)
        d *= 2
    f_ref[...] = logf + carry_sc[0:1, :]
    carry_sc[0:1, :] = f_ref[s - 1:s, :]
    k = z_ref[:, LANES:2 * LANES]
    ms = jnp.sum(k * k, axis=-1, keepdims=True) * (1.0 / IDX_DIM)
    k = k * lax.rsqrt(ms + NORM_EPS) * gik_ref[...]
    k = _rope_lanes(k, cos_ref[...], sup_ref[...], sdn_ref[...], IDX_DIM // ROT_FRAC_DIV // 2)
    lane = lax.broadcasted_iota(jnp.int32, (s, LANES), 1)
    k = jnp.where(lane < IDX_DIM, k, 0.0)
    kie_ref[...] = k.astype(kie_ref.dtype)
    kio_ref[...] = pltpu.roll(k, IDX_DIM, 1).astype(kio_ref.dtype)
    wi_ref[...] = z_ref[:, 2 * LANES:3 * LANES] * idx_scale


def _small(z, bf, gik, cos, sup, sdn, idx_scale):
    s = z.shape[0]
    tm = MM_TM
    rows = lambda w: pl.BlockSpec((tm, w), lambda i: (i, 0))
    vec = pl.BlockSpec((1, LANES), lambda i: (0, 0))
    return pl.pallas_call(
        functools.partial(_small_kernel, idx_scale=idx_scale),
        grid=(s // tm,),
        in_specs=[rows(3 * LANES), vec, vec, rows(LANES), rows(LANES), rows(LANES)],
        out_specs=[rows(LANES)] * 4,
        out_shape=[jax.ShapeDtypeStruct((s, LANES), jnp.float32),
                   jax.ShapeDtypeStruct((s, LANES), jnp.bfloat16),
                   jax.ShapeDtypeStruct((s, LANES), jnp.bfloat16),
                   jax.ShapeDtypeStruct((s, LANES), jnp.float32)],
        scratch_shapes=[pltpu.VMEM((8, LANES), jnp.float32)],
        compiler_params=_cparams(1),
        name="small_heads",
    )(z, bf, gik, cos, sup, sdn)


def _sortable(x):
    b = pltpu.bitcast(x, jnp.int32)
    return jnp.where(b < 0, b ^ jnp.int32(0x7FFFFFFF), b)


def _index_kernel(qi_ref, kte_ref, kto_ref, wi_ref, bias_ref, keys_sc, wb_sc, *, topk):
    tq, s = bias_ref.shape
    tc = IDX_TC
    i = pl.program_id(0)
    row0 = i * tq
    n_chunks = (row0 + tq + tc - 1) // tc
    n_tiles = tc // LANES

    for j in range(N_IDX_HEADS):
        wb_sc[j] = jnp.broadcast_to(wi_ref[:, j:j + 1], (tq, LANES))

    row_ids = row0 + lax.broadcasted_iota(jnp.int32, (tq, tc), 0)
    col_iota = lax.broadcasted_iota(jnp.int32, (tq, tc), 1)

    def score_chunk(c, carry):
        col0 = pl.multiple_of(c * tc, tc)
        kte = kte_ref[:, pl.ds(col0, tc)]
        kto = kto_ref[:, pl.ds(col0, tc)]
        score = jnp.zeros((tq, tc), jnp.float32)
        for p in range(N_IDX_HEADS // 2):
            qp = qi_ref[:, p * LANES:(p + 1) * LANES]
            re = jnp.maximum(jnp.dot(qp, kte, preferred_element_type=jnp.float32), 0.0)
            ro = jnp.maximum(jnp.dot(qp, kto, preferred_element_type=jnp.float32), 0.0)
            we = jnp.concatenate([wb_sc[2 * p]] * n_tiles, axis=1)
            wo = jnp.concatenate([wb_sc[2 * p + 1]] * n_tiles, axis=1)
            score = score + re * we + ro * wo
        causal = (col0 + col_iota) <= row_ids
        score = jnp.where(causal, score, NEG_BIG)
        keys_sc[:, pl.ds(col0, tc)] = _sortable(score)
        return carry

    lax.fori_loop(0, n_chunks, score_chunk, 0)

    @pl.when(n_chunks % 2 == 1)
    def _():
        col0 = pl.multiple_of(n_chunks * tc, tc)
        keys_sc[:, pl.ds(col0, tc)] = _sortable(jnp.full((tq, tc), NEG_BIG, jnp.float32))

    tcc = 2 * tc

    def bit_step(state):
        b, thr, kept, _ = state
        cand = thr + lax.shift_left(jnp.int32(1), 31 - b)

        def count_chunk(c, cnt):
            col0 = pl.multiple_of(c * tcc, tcc)
            k = keys_sc[:, pl.ds(col0, tcc)]
            for t in range(tcc // LANES):
                ge = k[:, t * LANES:(t + 1) * LANES] >= cand
                cnt = cnt + jnp.where(ge, 1.0, 0.0)
            return cnt

        cnt = lax.fori_loop(0, (n_chunks + 1) // 2, count_chunk,
                            jnp.zeros((tq, LANES), jnp.float32))
        total = jnp.sum(cnt, axis=1, keepdims=True)
        accept = total >= topk
        thr = jnp.where(accept, cand, thr)
        kept = jnp.where(accept, total, kept)
        unresolved = jnp.max(jnp.where(kept == topk, 0.0, 1.0))
        return b + 1, thr, kept, unresolved

    def search_on(state):
        b, _, _, unresolved = state
        return jnp.logical_and(b < 32, unresolved > 0.5)

    _, thr, _, _ = lax.while_loop(
        search_on, bit_step,
        (jnp.int32(0), jnp.full((tq, LANES), _INT_MIN, jnp.int32),
         jnp.full((tq, LANES), float(2 * s), jnp.float32), jnp.float32(1.0)))

    def bias_chunk(c, carry):
        col0 = pl.multiple_of(c * tc, tc)
        k = keys_sc[:, pl.ds(col0, tc)]
        causal = (col0 + col_iota) <= row_ids
        thr_w = jnp.concatenate([thr] * n_tiles, axis=1)
        keep = jnp.logical_and(k >= thr_w, causal)
        bias_ref[:, pl.ds(col0, tc)] = jnp.where(keep, 0.0, NEG_BIG).astype(bias_ref.dtype)
        return carry

    lax.fori_loop(0, n_chunks, bias_chunk, 0)

    def fill_chunk(c, carry):
        col0 = pl.multiple_of(c * tc, tc)
        bias_ref[:, pl.ds(col0, tc)] = jnp.full((tq, tc), NEG_BIG, bias_ref.dtype)
        return carry

    lax.fori_loop(n_chunks, s // tc, fill_chunk, 0)


def _index_bias(qi, kte, kto, wi, topk):
    s = qi.shape[0]
    tq = IDX_TQ
    return pl.pallas_call(
        functools.partial(_index_kernel, topk=topk),
        grid=(s // tq,),
        in_specs=[pl.BlockSpec((tq, IDX_Q_W), lambda i: (i, 0)),
                  pl.BlockSpec((LANES, s), lambda i: (0, 0)),
                  pl.BlockSpec((LANES, s), lambda i: (0, 0)),
                  pl.BlockSpec((tq, LANES), lambda i: (i, 0))],
        out_specs=pl.BlockSpec((tq, s), lambda i: (i, 0)),
        out_shape=jax.ShapeDtypeStruct((s, s), jnp.bfloat16),
        scratch_shapes=[pltpu.VMEM((tq, s), jnp.int32),
                        pltpu.VMEM((N_IDX_HEADS, tq, LANES), jnp.float32)],
        compiler_params=_cparams(1),
        name="index_topk_bias",
    )(qi, kte, kto, wi)


def _attn_kernel(qt_ref, kt_ref, skip_ref, q_ref, k_ref, v_ref, *rest, fox):
    if fox:
        fq_ref, fk_ref, o_ref, m_sc, l_sc, acc_sc, s_sc, p_sc, alpha_sc = rest
    else:
        bias_ref, o_ref, m_sc, l_sc, acc_sc, s_sc, p_sc, alpha_sc = rest
    p_id = pl.program_id(1)
    qb = qt_ref[p_id]
    kb = kt_ref[p_id]
    tq, tk = q_ref.shape[0], k_ref.shape[0]
    rc = ATT_RC
    qk_scale = (HEAD_DIM ** -0.5) * LOG2E

    @pl.when(kb == qb)
    def _():
        m_sc[...] = jnp.full(m_sc.shape, NEG_BIG, jnp.float32)
        l_sc[...] = jnp.zeros(l_sc.shape, jnp.float32)
        acc_sc[...] = jnp.zeros(acc_sc.shape, jnp.float32)

    if fox:
        key_bias = (fq_ref[:, 0:1] - fk_ref[...]) * LOG2E
        col_minus_row = (lax.broadcasted_iota(jnp.int32, (rc, tk), 1)
                         - lax.broadcasted_iota(jnp.int32, (rc, tk), 0))

    def key_cols(t, diag):
        return (t + 1) * rc if diag else tk

    def logits_stage(t, diag):
        rows = slice(t * rc, (t + 1) * rc)
        kc = key_cols(t, diag)
        s = lax.dot_general(q_ref[rows, :], k_ref[0:kc, :], (((1,), (1,)), ((), ())),
                            preferred_element_type=jnp.float32) * qk_scale
        if fox:
            s = s + key_bias[:, 0:kc]
            if diag:
                s = jnp.where(col_minus_row[:, 0:kc] <= t * rc, s, NEG_BIG)
        else:
            s = s + bias_ref[rows, 0:kc].astype(jnp.float32)
        s_sc[t % 2, :, 0:kc] = s

    def softmax_stage(t, diag):
        rows = slice(t * rc, (t + 1) * rc)
        kc = key_cols(t, diag)
        s = s_sc[t % 2, :, 0:kc]
        m_prev = m_sc[rows, :]
        m_new = jnp.maximum(m_prev, jnp.max(s, axis=1, keepdims=True))
        alpha = jnp.exp2(m_prev - m_new)
        p = jnp.exp2(s - m_new)
        l_sc[rows, :] = alpha * l_sc[rows, :] + jnp.sum(p, axis=1, keepdims=True)
        m_sc[rows, :] = m_new
        alpha_sc[t % 2] = alpha
        p_sc[t % 2, :, 0:kc] = p.astype(p_sc.dtype)

    def value_stage(t, diag):
        rows = slice(t * rc, (t + 1) * rc)
        kc = key_cols(t, diag)
        acc_sc[rows, :] = alpha_sc[t % 2] * acc_sc[rows, :] + jnp.dot(
            p_sc[t % 2, :, 0:kc], v_ref[0:kc, :], preferred_element_type=jnp.float32)

    def all_chunks(diag):
        n_rc = tq // rc
        for t in range(n_rc + 2):
            if t < n_rc:
                logits_stage(t, diag)
            if 1 <= t <= n_rc:
                softmax_stage(t - 1, diag)
            if t >= 2:
                value_stage(t - 2, diag)

    pl.when(kb == qb)(functools.partial(all_chunks, True))
    live = skip_ref[pl.program_id(0), p_id] == 0
    pl.when(jnp.logical_and(kb != qb, live))(functools.partial(all_chunks, False))

    @pl.when(kb == 0)
    def _():
        o_ref[...] = (acc_sc[...] / l_sc[...]).astype(o_ref.dtype)


def _block_pairs(s):
    nq = s // ATT_TQ
    return [(a, b) for a in range(nq) for b in range(a, -1, -1)]


def _fox_skip_flags(f_t, g_q, g_k):
    s = f_t.shape[1]
    bound2 = (HEAD_DIM * jnp.max(jnp.abs(g_q)) * jnp.max(jnp.abs(g_k)) * BF16_NORM_SLACK
              * (HEAD_DIM ** -0.5) * LOG2E)
    pairs = _block_pairs(s)
    qrow = jnp.asarray([a * ATT_TQ for a, _ in pairs], jnp.int32)
    kcol = jnp.asarray([b * ATT_TK + ATT_TK - 1 for _, b in pairs], jnp.int32)
    gap2 = (f_t[:, qrow] - f_t[:, kcol]) * LOG2E
    return (gap2 + 2.0 * bound2 < EXP2_ZERO_BELOW).astype(jnp.int32)


def _attention(qk, v, q_col, k_col, v_col, *, frow=None, skip=None, bias=None):
    s = qk.shape[0]
    n_heads = N_FOX_HEADS
    tq, tk = ATT_TQ, ATT_TK
    assert tq == tk
    pairs = _block_pairs(s)
    qt = jnp.asarray([a for a, _ in pairs], jnp.int32)
    kt = jnp.asarray([b for _, b in pairs], jnp.int32)
    fox = bias is None
    if skip is None:
        skip = jnp.zeros((n_heads, len(pairs)), jnp.int32)
    in_specs = [
        pl.BlockSpec((tq, HEAD_DIM), lambda h, p, qt, kt, sk: (qt[p], q_col + h)),
        pl.BlockSpec((tk, HEAD_DIM), lambda h, p, qt, kt, sk: (kt[p], k_col + h)),
        pl.BlockSpec((tk, HEAD_DIM), lambda h, p, qt, kt, sk: (kt[p], v_col + h)),
    ]
    if fox:
        in_specs += [pl.BlockSpec((None, 1, tq), lambda h, p, qt, kt, sk: (h, 0, qt[p])),
                     pl.BlockSpec((None, 1, tk), lambda h, p, qt, kt, sk: (h, 0, kt[p]))]
        operands = (qk, qk, v, frow, frow)
    else:
        in_specs += [pl.BlockSpec((tq, tk), lambda h, p, qt, kt, sk: (qt[p], kt[p]))]
        operands = (qk, qk, v, bias)
    grid_spec = pltpu.PrefetchScalarGridSpec(
        num_scalar_prefetch=3,
        grid=(n_heads, len(pairs)),
        in_specs=in_specs,
        out_specs=pl.BlockSpec((tq, HEAD_DIM), lambda h, p, qt, kt, sk: (qt[p], h)),
        scratch_shapes=[pltpu.VMEM((tq, 1), jnp.float32),
                        pltpu.VMEM((tq, 1), jnp.float32),
                        pltpu.VMEM((tq, HEAD_DIM), jnp.float32),
                        pltpu.VMEM((2, ATT_RC, tk), jnp.float32),
                        pltpu.VMEM((2, ATT_RC, tk), jnp.bfloat16),
                        pltpu.VMEM((2, ATT_RC, 1), jnp.float32)],
    )
    return pl.pallas_call(
        functools.partial(_attn_kernel, fox=fox),
        grid_spec=grid_spec,
        out_shape=jax.ShapeDtypeStruct((s, n_heads * HEAD_DIM), jnp.bfloat16),
        compiler_params=_cparams(2),
        name="fox_attention" if fox else "dsa_attention",
    )(qt, kt, skip, *operands)


def _cast_kernel(x_ref, o_ref):
    o_ref[...] = x_ref[...].astype(o_ref.dtype)


def _cast_bf16(w, rows=None):
    l, k, n = w.shape
    rows = math.gcd(CAST_ROWS if rows is None else rows, l * k)
    out = pl.pallas_call(
        _cast_kernel,
        grid=(l * k // rows,),
        in_specs=[pl.BlockSpec((rows, n), lambda i: (i, 0))],
        out_specs=pl.BlockSpec((rows, n), lambda i: (i, 0)),
        out_shape=jax.ShapeDtypeStruct((l * k, n), jnp.bfloat16),
        compiler_params=_cparams(1),
        name="cast_bf16",
    )(w.reshape(l * k, n))
    return out.reshape(l, k, n)


def _rope_angles(s, rot):
    half = rot // 2
    pos = jnp.arange(s, dtype=jnp.float32)
    inv = jnp.power(jnp.float32(ROPE_THETA), -2.0 * jnp.arange(half, dtype=jnp.float32) / rot)
    ang = pos[:, None] * inv[None, :]
    return jnp.cos(ang), jnp.sin(ang)


def _rope_tables(s, rot, period):
    half = rot // 2
    cos, sin = _rope_angles(s, rot)
    ones = jnp.ones((s, period - rot), jnp.float32)
    zeros = lambda n: jnp.zeros((s, n), jnp.float32)
    c = jnp.concatenate([cos, cos, ones], axis=1)
    s_up = jnp.concatenate([-sin, zeros(period - half)], axis=1)
    s_dn = jnp.concatenate([zeros(half), sin, zeros(period - rot)], axis=1)
    reps = LANES // period
    return tuple(jnp.tile(t, (1, reps)) for t in (c, s_up, s_dn))


def _rope_perm(a):
    h = HEAD_DIM // ROT_FRAC_DIV // 2
    half = LANES // 2
    a4 = a.reshape(a.shape[:-1] + (a.shape[-1] // HEAD_DIM, HEAD_DIM))
    a4 = jnp.concatenate([a4[..., 0:h], a4[..., 2 * h:half + h], a4[..., h:2 * h],
                          a4[..., half + h:]], axis=-1)
    return a4.reshape(a.shape)


def _rope_tables_split(s):
    h = HEAD_DIM // ROT_FRAC_DIV // 2
    half = LANES // 2
    cos, sin = _rope_angles(s, 2 * h)
    ones = jnp.ones((s, half - h), jnp.float32)
    zeros = jnp.zeros((s, half - h), jnp.float32)
    return (jnp.concatenate([cos, ones, cos, ones], axis=1),
            jnp.concatenate([-sin, zeros, sin, zeros], axis=1))


def _pad_last(w, n):
    return jnp.pad(w, [(0, 0)] * (w.ndim - 1) + [(0, n - w.shape[-1])])


def _pad_row(v, n):
    return jnp.pad(v, (0, n - v.shape[0])).reshape(1, n)


def kernel(x, p, g_ffn1, w1_gate, w1_up, w1_down, g_mix, w_in, b_f, g_qa, g_ka, g_qb, g_kb, g_ik,
           w_br_fox, w_br_dsa, w_o, g_ffn2, w2_gate, w2_up, w2_down, g_ple, w_ple_gate, w_ple_proj):
    b, s, d = x.shape
    assert b == 1
    bf16 = jnp.bfloat16
    topk = min(INDEX_TOPK_MAX, s // 4)
    idx_scale = (N_IDX_HEADS ** -0.5) * (IDX_DIM ** -0.5)
    rope_b = _rope_tables_split(s)
    rope_i = _rope_tables(s, IDX_DIM // ROT_FRAC_DIV, IDX_DIM)
    half_i = IDX_DIM // ROT_FRAC_DIV // 2
    row_tab = lambda t: (t, (MM_TM, LANES), lambda i, j: (i, 0))
    res_tile = lambda h: (h, (MM_TM, MM_TN), lambda i, j: (i, j))
    blk = lambda cols: cols // MM_TN

    w1g, w1u, w1d = _cast_bf16(w1_gate), _cast_bf16(w1_up), _cast_bf16(w1_down)
    w2g, w2u, w2d = _cast_bf16(w2_gate), _cast_bf16(w2_up), _cast_bf16(w2_down)
    wbf, wbd, wo = _cast_bf16(w_br_fox), _cast_bf16(w_br_dsa), _cast_bf16(w_o)
    wpg, wpp = _cast_bf16(w_ple_gate), _cast_bf16(w_ple_proj)
    off_fa = 3 * FOX_W
    off_b = off_fa + N_FOX_HEADS
    off_ki = off_b + 3 * DSA_W + IDX_Q_W
    off_wi = off_ki + IDX_DIM
    off_g = off_wi + N_IDX_HEADS
    w_in_bf = _cast_bf16(w_in, rows=CAST_ROWS_WIDE)
    w_a = w_in_bf
    w_b = jnp.concatenate([_rope_perm(w_in_bf[:, :, off_b:off_b + 2 * DSA_W]),
                           w_in_bf[:, :, off_b + 2 * DSA_W:off_ki]], axis=2)
    w_g = w_in_bf[:, :, off_g:]
    w_small = jnp.concatenate([_pad_last(w_in_bf[:, :, off_fa:off_b], LANES),
                               _pad_last(w_in_bf[:, :, off_ki:off_wi], LANES),
                               _pad_last(w_in_bf[:, :, off_wi:off_g], LANES)], axis=2)
    p_bf = p.reshape(DEPTH, s, PLE_DIM).astype(bf16)

    def ffn(h, g, wg, wu, wd, layer, tag):
        u = _rmsnorm(h, g)
        a = _matmul([u], [wg, wu], [(0, 0), (0, 1)], [], _ep_swiglu, bf16,
                    layer=layer, name=tag + "_swiglu")
        return _matmul([a], [wd], [(0, 0)], [res_tile(h)],
                       functools.partial(_ep_residual, scale=0.5), jnp.float32,
                       layer=layer, name=tag + "_down")

    h = x.reshape(s, d)
    for i in range(DEPTH):
        h = ffn(h, g_ffn1[i], w1g, w1u, w1d, i, "ffn1")

        u = _rmsnorm(h, g_mix[i])
        gain_a = jnp.concatenate([jnp.tile(g_qa[i], N_FOX_HEADS), jnp.tile(g_ka[i], N_FOX_HEADS)])
        gain_b = _rope_perm(jnp.concatenate([jnp.tile(g_qb[i], N_DSA_HEADS),
                                             jnp.tile(g_kb[i], N_DSA_HEADS)]))
        col_vec = lambda v: (v.reshape(1, -1), (1, MM_TN), lambda i, j: (0, j))

        qk_a = _matmul([u], [w_a], [(0, 0)], [col_vec(gain_a)],
                       functools.partial(_ep_headnorm, rope=False), bf16,
                       layer=i, name="proj_qk_fox", n=2 * FOX_W)
        v_a = _matmul([u], [w_a], [(0, 0)], [], _ep_cast, bf16,
                      layer=i, name="proj_v_fox", n=FOX_W, rhs_off=[blk(2 * FOX_W)])
        qk_b = _matmul([u], [w_b], [(0, 0)], [col_vec(gain_b)] + [row_tab(t) for t in rope_b],
                       functools.partial(_ep_headnorm, rope=True), bf16,
                       layer=i, name="proj_qk_dsa", n=2 * DSA_W)
        v_b = _matmul([u], [w_b], [(0, 0)], [], _ep_cast, bf16,
                      layer=i, name="proj_v_dsa", n=DSA_W, rhs_off=[blk(2 * DSA_W)])
        q_i = _matmul([u], [w_b], [(0, 0)], [row_tab(t) for t in rope_i],
                      functools.partial(_ep_rope, rope_half=half_i), bf16,
                      layer=i, name="proj_qidx", n=IDX_Q_W, rhs_off=[blk(3 * DSA_W)])
        gates = _matmul([u], [w_g], [(0, 0)], [], _ep_sigmoid, bf16, layer=i, name="proj_gates")
        z_small = _matmul([u], [w_small], [(0, 0)], [], _ep_cast, jnp.float32,
                          layer=i, name="proj_small", tn=3 * LANES)

        f_cum, ki_e, ki_o, wi_s = _small(z_small, _pad_row(b_f[i], LANES), _pad_row(g_ik[i], LANES),
                                         *rope_i, idx_scale)

        f_t = f_cum[:, :N_FOX_HEADS].T
        o_a = _attention(qk_a, v_a, 0, N_FOX_HEADS, 0, frow=f_t[:, None, :],
                         skip=_fox_skip_flags(f_t, g_qa[i], g_ka[i]))

        bias = _index_bias(q_i, ki_e.T, ki_o.T, wi_s, topk)
        o_b = _attention(qk_b, v_b, 0, N_DSA_HEADS, 0, bias=bias)

        n_gate_blocks = D_MODEL // MM_TN
        y = _matmul([o_a, o_b], [wbf, wbd], [(0, 0), (1, 1)],
                    [(gates, (MM_TM, MM_TN), lambda i, j: (i, j)),
                     (gates, (MM_TM, MM_TN), lambda i, j: (i, j + n_gate_blocks))],
                    _ep_merge, bf16, layer=i, name="branch_merge")
        h = _matmul([y], [wo], [(0, 0)], [res_tile(h)],
                    functools.partial(_ep_residual, scale=1.0), jnp.float32,
                    layer=i, name="out_proj")

        h = ffn(h, g_ffn2[i], w2g, w2u, w2d, i, "ffn2")

        u = _rmsnorm(h, g_ple[i])
        h = _matmul([u, p_bf[i]], [wpg, wpp], [(0, 0), (1, 1)], [res_tile(h)],
                    _ep_ple, jnp.float32, layer=i, name="ple")
    return h.reshape(b, s, d)
```

```python
import functools
import math

import jax
import jax.numpy as jnp
from jax import lax
from jax.experimental import pallas as pl
from jax.experimental.pallas import tpu as pltpu

D_MODEL = 4096
SEQ = 8192
DEPTH = 4
HEAD_DIM = 128
N_FOX_HEADS = 16
N_DSA_HEADS = 16
N_IDX_HEADS = 32
IDX_DIM = 64
INDEX_TOPK_MAX = 256
D_FF = 3584
PLE_DIM = 256
ROPE_THETA = 500000.0
ROT_FRAC_DIV = 4
NORM_EPS = 1e-6
NEG_BIG = -1e30
LOG2E = 1.4426950408889634
EXP2_ZERO_BELOW = -160.0
FIXED_SHIFT_MAX_RANGE = 100.0
BF16_NORM_SLACK = 1.02
FOX_W = N_FOX_HEADS * HEAD_DIM
DSA_W = N_DSA_HEADS * HEAD_DIM
IDX_Q_W = N_IDX_HEADS * IDX_DIM

LANES = 128
VMEM_LIMIT = 56 * 1024 * 1024

MM_TM = 1024
MM_TN = 512
NORM_TM = 512
CAST_ROWS = 512
CAST_ROWS_WIDE = 128
ATT_TQ = 1024
ATT_TK = 1024
ATT_RC = 256
IDX_TQ = 256
IDX_TC = 512

_INT_MIN = -(2 ** 31)


def _cparams(n_axes):
    return pltpu.CompilerParams(
        dimension_semantics=("arbitrary",) * n_axes,
        vmem_limit_bytes=VMEM_LIMIT,
    )


def _rmsnorm_kernel(x_ref, g_ref, o_ref):
    x = x_ref[...]
    ms = jnp.mean(x * x, axis=-1, keepdims=True)
    o_ref[...] = (x * lax.rsqrt(ms + NORM_EPS) * g_ref[...]).astype(o_ref.dtype)


def _rmsnorm(x, g):
    m, d = x.shape
    return pl.pallas_call(
        _rmsnorm_kernel,
        grid=(m // NORM_TM,),
        in_specs=[pl.BlockSpec((NORM_TM, d), lambda i: (i, 0)),
                  pl.BlockSpec((1, d), lambda i: (0, 0))],
        out_specs=pl.BlockSpec((NORM_TM, d), lambda i: (i, 0)),
        out_shape=jax.ShapeDtypeStruct((m, d), jnp.bfloat16),
        compiler_params=_cparams(1),
        name="rmsnorm",
    )(x, g.reshape(1, d))


def _mm_kernel(*refs, n_lhs, n_rhs, pairs, n_extra, epilogue):
    lhs = refs[:n_lhs]
    rhs = refs[n_lhs:n_lhs + n_rhs]
    extra = refs[n_lhs + n_rhs:n_lhs + n_rhs + n_extra]
    out = refs[n_lhs + n_rhs + n_extra]
    accs = [jnp.dot(lhs[a][...], rhs[b][...], preferred_element_type=jnp.float32)
            for a, b in pairs]
    epilogue(accs, extra, out)


def _matmul(lhs, rhs, pairs, extras, epilogue, out_dtype, *, layer, name, tn=MM_TN, n=None,
            rhs_off=None):
    m = lhs[0].shape[0]
    n = rhs[0].shape[2] if n is None else n
    rhs_off = [0] * len(rhs) if rhs_off is None else rhs_off
    tm = MM_TM
    in_specs = []
    for a in lhs:
        in_specs.append(pl.BlockSpec((tm, a.shape[1]), lambda i, j: (i, 0)))
    for w, off in zip(rhs, rhs_off):
        in_specs.append(pl.BlockSpec((None, w.shape[1], tn),
                                     lambda i, j, off=off: (layer, 0, j + off)))
    for _, bs, im in extras:
        in_specs.append(pl.BlockSpec(bs, im))
    kern = functools.partial(_mm_kernel, n_lhs=len(lhs), n_rhs=len(rhs), pairs=pairs,
                             n_extra=len(extras), epilogue=epilogue)
    return pl.pallas_call(
        kern,
        grid=(m // tm, n // tn),
        in_specs=in_specs,
        out_specs=pl.BlockSpec((tm, tn), lambda i, j: (i, j)),
        out_shape=jax.ShapeDtypeStruct((m, n), out_dtype),
        compiler_params=_cparams(2),
        name=name,
    )(*lhs, *rhs, *[e[0] for e in extras])


def _ep_cast(accs, extra, out):
    out[...] = accs[0].astype(out.dtype)


def _ep_sigmoid(accs, extra, out):
    out[...] = jax.nn.sigmoid(accs[0]).astype(out.dtype)


def _ep_swiglu(accs, extra, out):
    out[...] = (jax.nn.silu(accs[0]) * accs[1]).astype(out.dtype)


def _ep_residual(accs, extra, out, *, scale):
    out[...] = extra[0][...] + scale * accs[0]


def _ep_merge(accs, extra, out):
    ga = extra[0][...].astype(jnp.float32)
    gb = extra[1][...].astype(jnp.float32)
    out[...] = (ga * accs[0] + gb * accs[1]).astype(out.dtype)


def _ep_ple(accs, extra, out):
    out[...] = extra[0][...] + jax.nn.sigmoid(accs[0]) * accs[1]


def _rope_lanes(y, cos, s_up, s_dn, half):
    up = pltpu.roll(y, LANES - half, 1)
    dn = pltpu.roll(y, half, 1)
    return y * cos + up * s_up + dn * s_dn


def _ep_headnorm(accs, extra, out, *, rope):
    acc = accs[0]
    gain = extra[0][...]
    groups = [slice(g * LANES, (g + 1) * LANES) for g in range(acc.shape[1] // LANES)]
    normed = []
    for sl in groups:
        x = acc[:, sl]
        ms = jnp.mean(x * x, axis=-1, keepdims=True)
        normed.append(x * lax.rsqrt(ms + NORM_EPS) * gain[:, sl])
    for sl, y in zip(groups, normed):
        if rope:
            y = y * extra[1][...] + pltpu.roll(y, LANES // 2, 1) * extra[2][...]
        out[:, sl] = y.astype(out.dtype)


def _ep_rope(accs, extra, out, *, rope_half):
    acc = accs[0]
    for g in range(acc.shape[1] // LANES):
        sl = slice(g * LANES, (g + 1) * LANES)
        y = _rope_lanes(acc[:, sl], extra[0][...], extra[1][...], extra[2][...], rope_half)
        out[:, sl] = y.astype(out.dtype)


def _small_kernel(zf_ref, zkw_ref, bf_ref, gik_ref, cos_ref, sup_ref, sdn_ref,
                  f_ref, kie_ref, kio_ref, wi_ref, carry_sc, *, idx_scale, ki_lane, wi_lane):
    s = zf_ref.shape[0]
    lane = lax.broadcasted_iota(jnp.int32, (s, LANES), 1)

    @pl.when(pl.program_id(0) == 0)
    def _():
        carry_sc[...] = jnp.zeros(carry_sc.shape, jnp.float32)

    x = jnp.where(lane < N_FOX_HEADS, zf_ref[...], 0.0) + bf_ref[...]
    logf = jnp.minimum(x, 0.0) - jnp.log1p(jnp.exp(-jnp.abs(x)))
    row = lax.broadcasted_iota(jnp.int32, (s, LANES), 0)
    d = 1
    while d < s:
        logf = logf + jnp.where(row >= d, pltpu.roll(logf, d, 0), 0.0)
        d *= 2
    f_ref[...] = logf + carry_sc[0:1, :]
    carry_sc[0:1, :] = f_ref[s - 1:s, :]
    zkw = zkw_ref[...]
    k = jnp.where(lane < IDX_DIM, pltpu.roll(zkw, (LANES - ki_lane) % LANES, 1), 0.0)
    ms = jnp.sum(k * k, axis=-1, keepdims=True) * (1.0 / IDX_DIM)
    k = k * lax.rsqrt(ms + NORM_EPS) * gik_ref[...]
    k = _rope_lanes(k, cos_ref[...], sup_ref[...], sdn_ref[...], IDX_DIM // ROT_FRAC_DIV // 2)
    k = jnp.where(lane < IDX_DIM, k, 0.0)
    kie_ref[...] = k.astype(kie_ref.dtype)
    kio_ref[...] = pltpu.roll(k, IDX_DIM, 1).astype(kio_ref.dtype)
    wi = pltpu.roll(zkw, (LANES - wi_lane) % LANES, 1)
    wi_ref[...] = jnp.where(lane < N_IDX_HEADS, wi, 0.0) * idx_scale


def _small(zf, zkw, bf, gik, cos, sup, sdn, idx_scale, ki_lane, wi_lane):
    s = zf.shape[0]
    tm = MM_TM
    rows = lambda w: pl.BlockSpec((tm, w), lambda i: (i, 0))
    vec = pl.BlockSpec((1, LANES), lambda i: (0, 0))
    return pl.pallas_call(
        functools.partial(_small_kernel, idx_scale=idx_scale, ki_lane=ki_lane, wi_lane=wi_lane),
        grid=(s // tm,),
        in_specs=[rows(LANES), rows(LANES), vec, vec, rows(LANES), rows(LANES), rows(LANES)],
        out_specs=[rows(LANES)] * 4,
        out_shape=[jax.ShapeDtypeStruct((s, LANES), jnp.float32),
                   jax.ShapeDtypeStruct((s, LANES), jnp.bfloat16),
                   jax.ShapeDtypeStruct((s, LANES), jnp.bfloat16),
                   jax.ShapeDtypeStruct((s, LANES), jnp.float32)],
        scratch_shapes=[pltpu.VMEM((8, LANES), jnp.float32)],
        compiler_params=_cparams(1),
        name="small_heads",
    )(zf, zkw, bf, gik, cos, sup, sdn)


def _sortable(x):
    b = pltpu.bitcast(x, jnp.int32)
    return jnp.where(b < 0, b ^ jnp.int32(0x7FFFFFFF), b)


def _index_kernel(qi_ref, kte_ref, kto_ref, wi_ref, keep_ref, bias_ref, keys_sc, wb_sc, *, topk):
    tq, s = bias_ref.shape
    tc = IDX_TC
    i = pl.program_id(0)
    row0 = i * tq
    n_chunks = (row0 + tq + tc - 1) // tc
    n_tiles = tc // LANES

    for j in range(N_IDX_HEADS):
        wb_sc[j] = jnp.broadcast_to(wi_ref[:, j:j + 1], (tq, LANES))

    row_ids = row0 + lax.broadcasted_iota(jnp.int32, (tq, tc), 0)
    col_iota = lax.broadcasted_iota(jnp.int32, (tq, tc), 1)

    def score_chunk(c, carry):
        col0 = pl.multiple_of(c * tc, tc)
        kte = kte_ref[:, pl.ds(col0, tc)]
        kto = kto_ref[:, pl.ds(col0, tc)]
        score = jnp.zeros((tq, tc), jnp.float32)
        for p in range(N_IDX_HEADS // 2):
            qp = qi_ref[:, p * LANES:(p + 1) * LANES]
            re = jnp.maximum(jnp.dot(qp, kte, preferred_element_type=jnp.float32), 0.0)
            ro = jnp.maximum(jnp.dot(qp, kto, preferred_element_type=jnp.float32), 0.0)
            we = jnp.concatenate([wb_sc[2 * p]] * n_tiles, axis=1)
            wo = jnp.concatenate([wb_sc[2 * p + 1]] * n_tiles, axis=1)
            score = score + re * we + ro * wo
        causal = (col0 + col_iota) <= row_ids
        score = jnp.where(causal, score, NEG_BIG)
        keys_sc[:, pl.ds(col0, tc)] = _sortable(score)
        return carry

    lax.fori_loop(0, n_chunks, score_chunk, 0)

    @pl.when(n_chunks % 2 == 1)
    def _():
        col0 = pl.multiple_of(n_chunks * tc, tc)
        keys_sc[:, pl.ds(col0, tc)] = _sortable(jnp.full((tq, tc), NEG_BIG, jnp.float32))

    tcc = 2 * tc

    def bit_step(state):
        b, thr, kept, _ = state
        cand = thr + lax.shift_left(jnp.int32(1), 31 - b)

        def count_chunk(c, cnt):
            col0 = pl.multiple_of(c * tcc, tcc)
            k = keys_sc[:, pl.ds(col0, tcc)]
            for t in range(tcc // LANES):
                ge = k[:, t * LANES:(t + 1) * LANES] >= cand
                cnt = cnt + jnp.where(ge, 1.0, 0.0)
            return cnt

        cnt = lax.fori_loop(0, (n_chunks + 1) // 2, count_chunk,
                            jnp.zeros((tq, LANES), jnp.float32))
        total = jnp.sum(cnt, axis=1, keepdims=True)
        accept = total >= topk
        thr = jnp.where(accept, cand, thr)
        kept = jnp.where(accept, total, kept)
        unresolved = jnp.max(jnp.where(kept == topk, 0.0, 1.0))
        return b + 1, thr, kept, unresolved

    def search_on(state):
        b, _, _, unresolved = state
        return jnp.logical_and(b < 32, unresolved > 0.5)

    _, thr, _, _ = lax.while_loop(
        search_on, bit_step,
        (jnp.int32(0), jnp.full((tq, LANES), _INT_MIN, jnp.int32),
         jnp.full((tq, LANES), float(2 * s), jnp.float32), jnp.float32(1.0)))

    def bias_chunk(c, carry):
        col0 = pl.multiple_of(c * tc, tc)
        k = keys_sc[:, pl.ds(col0, tc)]
        causal = (col0 + col_iota) <= row_ids
        thr_w = jnp.concatenate([thr] * n_tiles, axis=1)
        keep = jnp.logical_and(k >= thr_w, causal)
        bias_ref[:, pl.ds(col0, tc)] = jnp.where(keep, keep_ref[...], NEG_BIG).astype(bias_ref.dtype)
        return carry

    lax.fori_loop(0, n_chunks, bias_chunk, 0)

    def fill_chunk(c, carry):
        col0 = pl.multiple_of(c * tc, tc)
        bias_ref[:, pl.ds(col0, tc)] = jnp.full((tq, tc), NEG_BIG, bias_ref.dtype)
        return carry

    lax.fori_loop(n_chunks, s // tc, fill_chunk, 0)


def _index_bias(qi, kte, kto, wi, topk, keep_value):
    s = qi.shape[0]
    tq = IDX_TQ
    return pl.pallas_call(
        functools.partial(_index_kernel, topk=topk),
        grid=(s // tq,),
        in_specs=[pl.BlockSpec((tq, IDX_Q_W), lambda i: (i, 0)),
                  pl.BlockSpec((LANES, s), lambda i: (0, 0)),
                  pl.BlockSpec((LANES, s), lambda i: (0, 0)),
                  pl.BlockSpec((tq, LANES), lambda i: (i, 0)),
                  pl.BlockSpec((1, IDX_TC), lambda i: (0, 0))],
        out_specs=pl.BlockSpec((tq, s), lambda i: (i, 0)),
        out_shape=jax.ShapeDtypeStruct((s, s), jnp.bfloat16),
        scratch_shapes=[pltpu.VMEM((tq, s), jnp.int32),
                        pltpu.VMEM((N_IDX_HEADS, tq, LANES), jnp.float32)],
        compiler_params=_cparams(1),
        name="index_topk_bias",
    )(qi, kte, kto, wi, jnp.full((1, IDX_TC), keep_value, jnp.float32))


def _attn_kernel(qt_ref, kt_ref, skip_ref, q_ref, k_ref, v_ref, *rest, fox):
    if fox:
        fq_ref, fk_ref, o_ref, m_sc, l_sc, acc_sc, s_sc, p_sc, alpha_sc = rest
    else:
        bias_ref, o_ref, m_sc, l_sc, acc_sc, s_sc, p_sc, alpha_sc = rest
    p_id = pl.program_id(1)
    qb = qt_ref[p_id]
    kb = kt_ref[p_id]
    tq, tk = q_ref.shape[0], k_ref.shape[0]
    rc = ATT_RC
    qk_scale = (HEAD_DIM ** -0.5) * LOG2E

    @pl.when(kb == qb)
    def _():
        m_sc[...] = jnp.full(m_sc.shape, NEG_BIG, jnp.float32)
        l_sc[...] = jnp.zeros(l_sc.shape, jnp.float32)
        acc_sc[...] = jnp.zeros(acc_sc.shape, jnp.float32)

    if fox:
        key_bias = (fq_ref[:, 0:1] - fk_ref[...]) * LOG2E
        col_minus_row = (lax.broadcasted_iota(jnp.int32, (rc, tk), 1)
                         - lax.broadcasted_iota(jnp.int32, (rc, tk), 0))

    def key_cols(t, diag):
        return (t + 1) * rc if diag else tk

    def logits_stage(t, diag):
        rows = slice(t * rc, (t + 1) * rc)
        kc = key_cols(t, diag)
        s = lax.dot_general(q_ref[rows, :], k_ref[0:kc, :], (((1,), (1,)), ((), ())),
                            preferred_element_type=jnp.float32) * qk_scale
        if fox:
            s = s + key_bias[:, 0:kc]
            if diag:
                s = jnp.where(col_minus_row[:, 0:kc] <= t * rc, s, NEG_BIG)
        else:
            s = s + bias_ref[rows, 0:kc].astype(jnp.float32)
        s_sc[t % 2, :, 0:kc] = s

    def softmax_stage(t, diag):
        rows = slice(t * rc, (t + 1) * rc)
        kc = key_cols(t, diag)
        s = s_sc[t % 2, :, 0:kc]
        m_prev = m_sc[rows, :]
        m_new = jnp.maximum(m_prev, jnp.max(s, axis=1, keepdims=True))
        alpha = jnp.exp2(m_prev - m_new)
        p = jnp.exp2(s - m_new)
        l_sc[rows, :] = alpha * l_sc[rows, :] + jnp.sum(p, axis=1, keepdims=True)
        m_sc[rows, :] = m_new
        alpha_sc[t % 2] = alpha
        p_sc[t % 2, :, 0:kc] = p.astype(p_sc.dtype)

    def value_stage(t, diag):
        rows = slice(t * rc, (t + 1) * rc)
        kc = key_cols(t, diag)
        acc_sc[rows, :] = alpha_sc[t % 2] * acc_sc[rows, :] + jnp.dot(
            p_sc[t % 2, :, 0:kc], v_ref[0:kc, :], preferred_element_type=jnp.float32)

    def all_chunks(diag):
        n_rc = tq // rc
        for t in range(n_rc + 2):
            if t < n_rc:
                logits_stage(t, diag)
            if 1 <= t <= n_rc:
                softmax_stage(t - 1, diag)
            if t >= 2:
                value_stage(t - 2, diag)

    pl.when(kb == qb)(functools.partial(all_chunks, True))
    live = skip_ref[pl.program_id(0), p_id] == 0
    pl.when(jnp.logical_and(kb != qb, live))(functools.partial(all_chunks, False))

    @pl.when(kb == 0)
    def _():
        o_ref[...] = (acc_sc[...] / l_sc[...]).astype(o_ref.dtype)


def _block_pairs(s):
    nq = s // ATT_TQ
    return [(a, b) for a in range(nq) for b in range(a, -1, -1)]


def _logit_bound2(g_q, g_k):
    return (HEAD_DIM * jnp.max(jnp.abs(g_q)) * jnp.max(jnp.abs(g_k)) * BF16_NORM_SLACK
            * (HEAD_DIM ** -0.5) * LOG2E)


def _fox_skip_flags(f_t, g_q, g_k):
    s = f_t.shape[1]
    bound2 = _logit_bound2(g_q, g_k)
    pairs = _block_pairs(s)
    qrow = jnp.asarray([a * ATT_TQ for a, _ in pairs], jnp.int32)
    kcol = jnp.asarray([b * ATT_TK + ATT_TK - 1 for _, b in pairs], jnp.int32)
    gap2 = (f_t[:, qrow] - f_t[:, kcol]) * LOG2E
    return (gap2 + 2.0 * bound2 < EXP2_ZERO_BELOW).astype(jnp.int32)


def _attention(qk, v, q_col, k_col, v_col, *, frow=None, skip=None, bias=None):
    s = qk.shape[0]
    n_heads = N_FOX_HEADS
    tq, tk = ATT_TQ, ATT_TK
    assert tq == tk
    pairs = _block_pairs(s)
    qt = jnp.asarray([a for a, _ in pairs], jnp.int32)
    kt = jnp.asarray([b for _, b in pairs], jnp.int32)
    fox = bias is None
    if skip is None:
        skip = jnp.zeros((n_heads, len(pairs)), jnp.int32)
    in_specs = [
        pl.BlockSpec((tq, HEAD_DIM), lambda h, p, qt, kt, sk: (qt[p], q_col + h)),
        pl.BlockSpec((tk, HEAD_DIM), lambda h, p, qt, kt, sk: (kt[p], k_col + h)),
        pl.BlockSpec((tk, HEAD_DIM), lambda h, p, qt, kt, sk: (kt[p], v_col + h)),
    ]
    if fox:
        in_specs += [pl.BlockSpec((None, 1, tq), lambda h, p, qt, kt, sk: (h, 0, qt[p])),
                     pl.BlockSpec((None, 1, tk), lambda h, p, qt, kt, sk: (h, 0, kt[p]))]
        operands = (qk, qk, v, frow, frow)
    else:
        in_specs += [pl.BlockSpec((tq, tk), lambda h, p, qt, kt, sk: (qt[p], kt[p]))]
        operands = (qk, qk, v, bias)
    grid_spec = pltpu.PrefetchScalarGridSpec(
        num_scalar_prefetch=3,
        grid=(n_heads, len(pairs)),
        in_specs=in_specs,
        out_specs=pl.BlockSpec((tq, HEAD_DIM), lambda h, p, qt, kt, sk: (qt[p], h)),
        scratch_shapes=[pltpu.VMEM((tq, 1), jnp.float32),
                        pltpu.VMEM((tq, 1), jnp.float32),
                        pltpu.VMEM((tq, HEAD_DIM), jnp.float32),
                        pltpu.VMEM((2, ATT_RC, tk), jnp.float32),
                        pltpu.VMEM((2, ATT_RC, tk), jnp.bfloat16),
                        pltpu.VMEM((2, ATT_RC, 1), jnp.float32)],
    )
    return pl.pallas_call(
        functools.partial(_attn_kernel, fox=fox),
        grid_spec=grid_spec,
        out_shape=jax.ShapeDtypeStruct((s, n_heads * HEAD_DIM), jnp.bfloat16),
        compiler_params=_cparams(2),
        name="fox_attention" if fox else "dsa_attention",
    )(qt, kt, skip, *operands)


def _attn_shift_kernel(qt_ref, kt_ref, skip_ref, q_ref, k_ref, v_ref, *rest, fox):
    if fox:
        fcol_ref, frow_ref, o_ref, acc_sc = rest
    else:
        bias_ref, o_ref, acc_sc = rest
    p_id = pl.program_id(1)
    qb = qt_ref[p_id]
    kb = kt_ref[p_id]
    tq, tk = q_ref.shape[0], k_ref.shape[0]
    rc = ATT_RC
    qk_scale = (HEAD_DIM ** -0.5) * LOG2E

    @pl.when(kb == qb)
    def _():
        acc_sc[...] = jnp.zeros(acc_sc.shape, jnp.float32)

    def chunks(diag):
        v1 = jnp.concatenate([v_ref[...], jnp.ones(v_ref.shape, v_ref.dtype)], axis=1)
        if fox:
            col_minus_row = (lax.broadcasted_iota(jnp.int32, (rc, tk), 1)
                             - lax.broadcasted_iota(jnp.int32, (rc, tk), 0))
        for t in range(tq // rc):
            rows = slice(t * rc, (t + 1) * rc)
            kc = (t + 1) * rc if diag else tk
            s = lax.dot_general(q_ref[rows, :], k_ref[0:kc, :], (((1,), (1,)), ((), ())),
                                preferred_element_type=jnp.float32) * qk_scale
            if fox:
                s = s + fcol_ref[rows, :] - frow_ref[:, 0:kc]
                if diag:
                    s = jnp.where(col_minus_row[:, 0:kc] <= t * rc, s, NEG_BIG)
            else:
                s = s + bias_ref[rows, 0:kc].astype(jnp.float32)
            p = jnp.exp2(s).astype(v_ref.dtype)
            acc_sc[rows, :] += jnp.dot(p, v1[0:kc, :], preferred_element_type=jnp.float32)

    pl.when(kb == qb)(functools.partial(chunks, True))
    live = skip_ref[pl.program_id(0), p_id] == 0
    pl.when(jnp.logical_and(kb != qb, live))(functools.partial(chunks, False))

    @pl.when(kb == 0)
    def _():
        o_ref[...] = (acc_sc[:, 0:HEAD_DIM] / acc_sc[:, HEAD_DIM:]).astype(o_ref.dtype)


def _attention_shift(qk, v, q_col, k_col, v_col, *, fcol=None, frow=None, skip=None, bias=None):
    s = qk.shape[0]
    n_heads = N_FOX_HEADS
    tq, tk = ATT_TQ, ATT_TK
    assert tq == tk
    pairs = _block_pairs(s)
    qt = jnp.asarray([a for a, _ in pairs], jnp.int32)
    kt = jnp.asarray([b for _, b in pairs], jnp.int32)
    fox = bias is None
    if skip is None:
        skip = jnp.zeros((n_heads, len(pairs)), jnp.int32)
    in_specs = [
        pl.BlockSpec((tq, HEAD_DIM), lambda h, p, qt, kt, sk: (qt[p], q_col + h)),
        pl.BlockSpec((tk, HEAD_DIM), lambda h, p, qt, kt, sk: (kt[p], k_col + h)),
        pl.BlockSpec((tk, HEAD_DIM), lambda h, p, qt, kt, sk: (kt[p], v_col + h)),
    ]
    if fox:
        in_specs += [pl.BlockSpec((None, tq, 1), lambda h, p, qt, kt, sk: (h, qt[p], 0)),
                     pl.BlockSpec((None, 1, tk), lambda h, p, qt, kt, sk: (h, 0, kt[p]))]
        operands = (qk, qk, v, fcol, frow)
    else:
        in_specs += [pl.BlockSpec((tq, tk), lambda h, p, qt, kt, sk: (qt[p], kt[p]))]
        operands = (qk, qk, v, bias)
    grid_spec = pltpu.PrefetchScalarGridSpec(
        num_scalar_prefetch=3,
        grid=(n_heads, len(pairs)),
        in_specs=in_specs,
        out_specs=pl.BlockSpec((tq, HEAD_DIM), lambda h, p, qt, kt, sk: (qt[p], h)),
        scratch_shapes=[pltpu.VMEM((tq, 2 * HEAD_DIM), jnp.float32)],
    )
    return pl.pallas_call(
        functools.partial(_attn_shift_kernel, fox=fox),
        grid_spec=grid_spec,
        out_shape=jax.ShapeDtypeStruct((s, n_heads * HEAD_DIM), jnp.bfloat16),
        compiler_params=_cparams(2),
        name="fox_attention_shift" if fox else "dsa_attention_shift",
    )(qt, kt, skip, *operands)


def _cast_kernel(x_ref, o_ref):
    o_ref[...] = x_ref[...].astype(o_ref.dtype)


def _cast_bf16(w, rows=None):
    l, k, n = w.shape
    rows = math.gcd(CAST_ROWS if rows is None else rows, k)
    spec = pl.BlockSpec((None, rows, n), lambda i, j: (i, j, 0))
    return pl.pallas_call(
        _cast_kernel,
        grid=(l, k // rows),
        in_specs=[spec],
        out_specs=spec,
        out_shape=jax.ShapeDtypeStruct((l, k, n), jnp.bfloat16),
        compiler_params=_cparams(2),
        name="cast_bf16",
    )(w)


def _rope_angles(s, rot):
    half = rot // 2
    pos = jnp.arange(s, dtype=jnp.float32)
    inv = jnp.power(jnp.float32(ROPE_THETA), -2.0 * jnp.arange(half, dtype=jnp.float32) / rot)
    ang = pos[:, None] * inv[None, :]
    return jnp.cos(ang), jnp.sin(ang)


def _rope_tables(s, rot, period):
    half = rot // 2
    cos, sin = _rope_angles(s, rot)
    ones = jnp.ones((s, period - rot), jnp.float32)
    zeros = lambda n: jnp.zeros((s, n), jnp.float32)
    c = jnp.concatenate([cos, cos, ones], axis=1)
    s_up = jnp.concatenate([-sin, zeros(period - half)], axis=1)
    s_dn = jnp.concatenate([zeros(half), sin, zeros(period - rot)], axis=1)
    reps = LANES // period
    return tuple(jnp.tile(t, (1, reps)) for t in (c, s_up, s_dn))


def _rope_perm(a):
    h = HEAD_DIM // ROT_FRAC_DIV // 2
    half = LANES // 2
    a4 = a.reshape(a.shape[:-1] + (a.shape[-1] // HEAD_DIM, HEAD_DIM))
    a4 = jnp.concatenate([a4[..., 0:h], a4[..., 2 * h:half + h], a4[..., h:2 * h],
                          a4[..., half + h:]], axis=-1)
    return a4.reshape(a.shape)


def _rope_tables_split(s):
    h = HEAD_DIM // ROT_FRAC_DIV // 2
    half = LANES // 2
    cos, sin = _rope_angles(s, 2 * h)
    ones = jnp.ones((s, half - h), jnp.float32)
    zeros = jnp.zeros((s, half - h), jnp.float32)
    return (jnp.concatenate([cos, ones, cos, ones], axis=1),
            jnp.concatenate([-sin, zeros, sin, zeros], axis=1))


def _pad_row(v, n):
    return jnp.pad(v, (0, n - v.shape[0])).reshape(1, n)


def kernel(x, p, g_ffn1, w1_gate, w1_up, w1_down, g_mix, w_in, b_f, g_qa, g_ka, g_qb, g_kb, g_ik,
           w_br_fox, w_br_dsa, w_o, g_ffn2, w2_gate, w2_up, w2_down, g_ple, w_ple_gate, w_ple_proj):
    b, s, d = x.shape
    assert b == 1
    bf16 = jnp.bfloat16
    topk = min(INDEX_TOPK_MAX, s // 4)
    idx_scale = (N_IDX_HEADS ** -0.5) * (IDX_DIM ** -0.5)
    rope_b = _rope_tables_split(s)
    rope_i = _rope_tables(s, IDX_DIM // ROT_FRAC_DIV, IDX_DIM)
    half_i = IDX_DIM // ROT_FRAC_DIV // 2
    row_tab = lambda t: (t, (MM_TM, LANES), lambda i, j: (i, 0))
    res_tile = lambda h: (h, (MM_TM, MM_TN), lambda i, j: (i, j))
    blk = lambda cols: cols // MM_TN

    w1g, w1u, w1d = _cast_bf16(w1_gate), _cast_bf16(w1_up), _cast_bf16(w1_down)
    w2g, w2u, w2d = _cast_bf16(w2_gate), _cast_bf16(w2_up), _cast_bf16(w2_down)
    wbf, wbd, wo = _cast_bf16(w_br_fox), _cast_bf16(w_br_dsa), _cast_bf16(w_o)
    wpg, wpp = _cast_bf16(w_ple_gate), _cast_bf16(w_ple_proj)
    off_fa = 3 * FOX_W
    off_b = off_fa + N_FOX_HEADS
    off_ki = off_b + 3 * DSA_W + IDX_Q_W
    off_wi = off_ki + IDX_DIM
    off_g = off_wi + N_IDX_HEADS
    w_in_bf = _cast_bf16(w_in, rows=CAST_ROWS_WIDE)
    w_a = w_in_bf
    w_b = jnp.concatenate([_rope_perm(w_in_bf[:, :, off_b:off_b + 2 * DSA_W]),
                           w_in_bf[:, :, off_b + 2 * DSA_W:off_ki]], axis=2)
    w_g = w_in_bf[:, :, off_g:]
    assert off_fa % LANES == 0 and off_g <= (off_ki // LANES + 1) * LANES
    kw_window = off_ki // LANES
    ki_lane, wi_lane = off_ki - kw_window * LANES, off_wi - kw_window * LANES
    p_bf = p.reshape(DEPTH, s, PLE_DIM).astype(bf16)

    def ffn(h, g, wg, wu, wd, layer, tag):
        u = _rmsnorm(h, g)
        a = _matmul([u], [wg, wu], [(0, 0), (0, 1)], [], _ep_swiglu, bf16,
                    layer=layer, name=tag + "_swiglu")
        return _matmul([a], [wd], [(0, 0)], [res_tile(h)],
                       functools.partial(_ep_residual, scale=0.5), jnp.float32,
                       layer=layer, name=tag + "_down")

    h = x.reshape(s, d)
    for i in range(DEPTH):
        h = ffn(h, g_ffn1[i], w1g, w1u, w1d, i, "ffn1")

        u = _rmsnorm(h, g_mix[i])
        gain_a = jnp.concatenate([jnp.tile(g_qa[i], N_FOX_HEADS), jnp.tile(g_ka[i], N_FOX_HEADS)])
        gain_b = _rope_perm(jnp.concatenate([jnp.tile(g_qb[i], N_DSA_HEADS),
                                             jnp.tile(g_kb[i], N_DSA_HEADS)]))
        col_vec = lambda v: (v.reshape(1, -1), (1, MM_TN), lambda i, j: (0, j))

        qk_a = _matmul([u], [w_a], [(0, 0)], [col_vec(gain_a)],
                       functools.partial(_ep_headnorm, rope=False), bf16,
                       layer=i, name="proj_qk_fox", n=2 * FOX_W)
        v_a = _matmul([u], [w_a], [(0, 0)], [], _ep_cast, bf16,
                      layer=i, name="proj_v_fox", n=FOX_W, rhs_off=[blk(2 * FOX_W)])
        qk_b = _matmul([u], [w_b], [(0, 0)], [col_vec(gain_b)] + [row_tab(t) for t in rope_b],
                       functools.partial(_ep_headnorm, rope=True), bf16,
                       layer=i, name="proj_qk_dsa", n=2 * DSA_W)
        v_b = _matmul([u], [w_b], [(0, 0)], [], _ep_cast, bf16,
                      layer=i, name="proj_v_dsa", n=DSA_W, rhs_off=[blk(2 * DSA_W)])
        q_i = _matmul([u], [w_b], [(0, 0)], [row_tab(t) for t in rope_i],
                      functools.partial(_ep_rope, rope_half=half_i), bf16,
                      layer=i, name="proj_qidx", n=IDX_Q_W, rhs_off=[blk(3 * DSA_W)])
        gates = _matmul([u], [w_g], [(0, 0)], [], _ep_sigmoid, bf16, layer=i, name="proj_gates")
        z_f = _matmul([u], [w_in_bf], [(0, 0)], [], _ep_cast, jnp.float32, layer=i,
                      name="proj_forget", tn=LANES, n=LANES, rhs_off=[off_fa // LANES])
        z_kw = _matmul([u], [w_in_bf], [(0, 0)], [], _ep_cast, jnp.float32, layer=i,
                       name="proj_idx_key", tn=LANES, n=LANES, rhs_off=[kw_window])

        f_cum, ki_e, ki_o, wi_s = _small(z_f, z_kw, _pad_row(b_f[i], LANES),
                                         _pad_row(g_ik[i], LANES), *rope_i, idx_scale,
                                         ki_lane, wi_lane)

        f_t = f_cum[:, :N_FOX_HEADS].T
        skip = _fox_skip_flags(f_t, g_qa[i], g_ka[i])
        shift_a = _logit_bound2(g_qa[i], g_ka[i])
        o_a = lax.cond(
            2.0 * shift_a < FIXED_SHIFT_MAX_RANGE,
            lambda: _attention_shift(qk_a, v_a, 0, N_FOX_HEADS, 0,
                                     fcol=(f_t * LOG2E - shift_a)[:, :, None],
                                     frow=(f_t * LOG2E)[:, None, :], skip=skip),
            lambda: _attention(qk_a, v_a, 0, N_FOX_HEADS, 0, frow=f_t[:, None, :], skip=skip))

        shift_b = _logit_bound2(g_qb[i], g_kb[i])
        fixed_b = 2.0 * shift_b < FIXED_SHIFT_MAX_RANGE
        bias = _index_bias(q_i, ki_e.T, ki_o.T, wi_s, topk, jnp.where(fixed_b, -shift_b, 0.0))
        o_b = lax.cond(
            fixed_b,
            lambda: _attention_shift(qk_b, v_b, 0, N_DSA_HEADS, 0, bias=bias),
            lambda: _attention(qk_b, v_b, 0, N_DSA_HEADS, 0, bias=bias))

        n_gate_blocks = D_MODEL // MM_TN
        y = _matmul([o_a, o_b], [wbf, wbd], [(0, 0), (1, 1)],
                    [(gates, (MM_TM, MM_TN), lambda i, j: (i, j)),
                     (gates, (MM_TM, MM_TN), lambda i, j: (i, j + n_gate_blocks))],
                    _ep_merge, bf16, layer=i, name="branch_merge")
        h = _matmul([y], [wo], [(0, 0)], [res_tile(h)],
                    functools.partial(_ep_residual, scale=1.0), jnp.float32,
                    layer=i, name="out_proj")

        h = ffn(h, g_ffn2[i], w2g, w2u, w2d, i, "ffn2")

        u = _rmsnorm(h, g_ple[i])
        h = _matmul([u, p_bf[i]], [wpg, wpp], [(0, 0), (1, 1)], [res_tile(h)],
                    _ep_ple, jnp.float32, layer=i, name="ple")
    return h.reshape(b, s, d)
```

```python
import functools
import math

import jax
import jax.numpy as jnp
from jax import lax
from jax.experimental import pallas as pl
from jax.experimental.pallas import tpu as pltpu

D_MODEL = 4096
SEQ = 8192
DEPTH = 4
HEAD_DIM = 128
N_FOX_HEADS = 16
N_DSA_HEADS = 16
N_IDX_HEADS = 32
IDX_DIM = 64
INDEX_TOPK_MAX = 256
D_FF = 3584
PLE_DIM = 256
ROPE_THETA = 500000.0
ROT_FRAC_DIV = 4
NORM_EPS = 1e-6
NEG_BIG = -1e30
LOG2E = 1.4426950408889634
EXP2_ZERO_BELOW = -160.0
FIXED_SHIFT_MAX_RANGE = 100.0
BF16_NORM_SLACK = 1.02
FOX_W = N_FOX_HEADS * HEAD_DIM
DSA_W = N_DSA_HEADS * HEAD_DIM
IDX_Q_W = N_IDX_HEADS * IDX_DIM

LANES = 128
VMEM_LIMIT = 56 * 1024 * 1024

MM_TM = 1024
MM_TN = 512
NORM_TM = 512
CAST_ROWS = 512
ATT_TQ = 1024
ATT_TK = 1024
ATT_RC = 256
IDX_TQ = 256
IDX_TC = 512
IDX_SG = 128

_INT_MIN = -(2 ** 31)


def _cparams(n_axes):
    return pltpu.CompilerParams(
        dimension_semantics=("arbitrary",) * n_axes,
        vmem_limit_bytes=VMEM_LIMIT,
    )


def _rmsnorm_kernel(x_ref, g_ref, o_ref):
    x = x_ref[...]
    ms = jnp.mean(x * x, axis=-1, keepdims=True)
    o_ref[...] = (x * lax.rsqrt(ms + NORM_EPS) * g_ref[...]).astype(o_ref.dtype)


def _rmsnorm(x, g):
    m, d = x.shape
    return pl.pallas_call(
        _rmsnorm_kernel,
        grid=(m // NORM_TM,),
        in_specs=[pl.BlockSpec((NORM_TM, d), lambda i: (i, 0)),
                  pl.BlockSpec((1, d), lambda i: (0, 0))],
        out_specs=pl.BlockSpec((NORM_TM, d), lambda i: (i, 0)),
        out_shape=jax.ShapeDtypeStruct((m, d), jnp.bfloat16),
        compiler_params=_cparams(1),
        name="rmsnorm",
    )(x, g.reshape(1, d))


def _mm_kernel(*refs, n_lhs, n_rhs, pairs, n_extra, epilogue):
    lhs = refs[:n_lhs]
    rhs = refs[n_lhs:n_lhs + n_rhs]
    extra = refs[n_lhs + n_rhs:n_lhs + n_rhs + n_extra]
    out = refs[n_lhs + n_rhs + n_extra]
    accs = [jnp.dot(lhs[a][...], rhs[b][...], preferred_element_type=jnp.float32)
            for a, b in pairs]
    epilogue(accs, extra, out)


def _matmul(lhs, rhs, pairs, extras, epilogue, out_dtype, *, layer, name, tn=MM_TN, n=None,
            rhs_off=None):
    m = lhs[0].shape[0]
    n = rhs[0].shape[2] if n is None else n
    rhs_off = [0] * len(rhs) if rhs_off is None else rhs_off
    tm = MM_TM
    in_specs = []
    for a in lhs:
        in_specs.append(pl.BlockSpec((tm, a.shape[1]), lambda i, j: (i, 0)))
    for w, off in zip(rhs, rhs_off):
        in_specs.append(pl.BlockSpec((None, w.shape[1], tn),
                                     lambda i, j, off=off: (layer, 0, j + off)))
    for _, bs, im in extras:
        in_specs.append(pl.BlockSpec(bs, im))
    kern = functools.partial(_mm_kernel, n_lhs=len(lhs), n_rhs=len(rhs), pairs=pairs,
                             n_extra=len(extras), epilogue=epilogue)
    return pl.pallas_call(
        kern,
        grid=(m // tm, n // tn),
        in_specs=in_specs,
        out_specs=pl.BlockSpec((tm, tn), lambda i, j: (i, j)),
        out_shape=jax.ShapeDtypeStruct((m, n), out_dtype),
        compiler_params=_cparams(2),
        name=name,
    )(*lhs, *rhs, *[e[0] for e in extras])


def _ep_cast(accs, extra, out):
    out[...] = accs[0].astype(out.dtype)


def _ep_sigmoid(accs, extra, out):
    out[...] = jax.nn.sigmoid(accs[0]).astype(out.dtype)


def _ep_swiglu(accs, extra, out):
    out[...] = (jax.nn.silu(accs[0]) * accs[1]).astype(out.dtype)


def _ep_residual(accs, extra, out, *, scale):
    out[...] = extra[0][...] + scale * accs[0]


def _ep_merge(accs, extra, out):
    ga = extra[0][...].astype(jnp.float32)
    gb = extra[1][...].astype(jnp.float32)
    out[...] = (ga * accs[0] + gb * accs[1]).astype(out.dtype)


def _ep_ple(accs, extra, out):
    out[...] = extra[0][...] + jax.nn.sigmoid(accs[0]) * accs[1]


def _rope_lanes(y, cos, s_up, s_dn, half):
    up = pltpu.roll(y, LANES - half, 1)
    dn = pltpu.roll(y, half, 1)
    return y * cos + up * s_up + dn * s_dn


def _ep_headnorm(accs, extra, out, *, rope_half):
    acc = accs[0]
    gain = extra[0][...]
    groups = [slice(g * LANES, (g + 1) * LANES) for g in range(acc.shape[1] // LANES)]
    normed = []
    for sl in groups:
        x = acc[:, sl]
        ms = jnp.mean(x * x, axis=-1, keepdims=True)
        normed.append(x * lax.rsqrt(ms + NORM_EPS) * gain[:, sl])
    for sl, y in zip(groups, normed):
        if rope_half:
            y = _rope_lanes(y, extra[1][...], extra[2][...], extra[3][...], rope_half)
        out[:, sl] = y.astype(out.dtype)


def _ep_rope(accs, extra, out, *, rope_half):
    acc = accs[0]
    for g in range(acc.shape[1] // LANES):
        sl = slice(g * LANES, (g + 1) * LANES)
        y = _rope_lanes(acc[:, sl], extra[0][...], extra[1][...], extra[2][...], rope_half)
        out[:, sl] = y.astype(out.dtype)


def _small_kernel(zf_ref, zkw_ref, bf_ref, gik_ref, cos_ref, sup_ref, sdn_ref,
                  f_ref, kie_ref, kio_ref, wi_ref, carry_sc, *, idx_scale, ki_lane, wi_lane):
    s = zf_ref.shape[0]
    lane = lax.broadcasted_iota(jnp.int32, (s, LANES), 1)

    @pl.when(pl.program_id(0) == 0)
    def _():
        carry_sc[...] = jnp.zeros(carry_sc.shape, jnp.float32)

    x = jnp.where(lane < N_FOX_HEADS, zf_ref[...], 0.0) + bf_ref[...]
    logf = jnp.minimum(x, 0.0) - jnp.log1p(jnp.exp(-jnp.abs(x)))
    row = lax.broadcasted_iota(jnp.int32, (s, LANES), 0)
    d = 1
    while d < s:
        logf = logf + jnp.where(row >= d, pltpu.roll(logf, d, 0), 0.0)
        d *= 2
    f_ref[...] = logf + carry_sc[0:1, :]
    carry_sc[0:1, :] = f_ref[s - 1:s, :]
    zkw = zkw_ref[...]
    k = jnp.where(lane < IDX_DIM, pltpu.roll(zkw, (LANES - ki_lane) % LANES, 1), 0.0)
    ms = jnp.sum(k * k, axis=-1, keepdims=True) * (1.0 / IDX_DIM)
    k = k * lax.rsqrt(ms + NORM_EPS) * gik_ref[...]
    k = _rope_lanes(k, cos_ref[...], sup_ref[...], sdn_ref[...], IDX_DIM // ROT_FRAC_DIV // 2)
    k = jnp.where(lane < IDX_DIM, k, 0.0)
    kie_ref[...] = k.astype(kie_ref.dtype)
    kio_ref[...] = pltpu.roll(k, IDX_DIM, 1).astype(kio_ref.dtype)
    wi = pltpu.roll(zkw, (LANES - wi_lane) % LANES, 1)
    wi_ref[...] = jnp.where(lane < N_IDX_HEADS, wi, 0.0) * idx_scale


def _small(zf, zkw, bf, gik, cos, sup, sdn, idx_scale, ki_lane, wi_lane):
    s = zf.shape[0]
    tm = MM_TM
    rows = lambda w: pl.BlockSpec((tm, w), lambda i: (i, 0))
    vec = pl.BlockSpec((1, LANES), lambda i: (0, 0))
    return pl.pallas_call(
        functools.partial(_small_kernel, idx_scale=idx_scale, ki_lane=ki_lane, wi_lane=wi_lane),
        grid=(s // tm,),
        in_specs=[rows(LANES), rows(LANES), vec, vec, rows(LANES), rows(LANES), rows(LANES)],
        out_specs=[rows(LANES)] * 4,
        out_shape=[jax.ShapeDtypeStruct((s, LANES), jnp.float32),
                   jax.ShapeDtypeStruct((s, LANES), jnp.bfloat16),
                   jax.ShapeDtypeStruct((s, LANES), jnp.bfloat16),
                   jax.ShapeDtypeStruct((s, LANES), jnp.float32)],
        scratch_shapes=[pltpu.VMEM((8, LANES), jnp.float32)],
        compiler_params=_cparams(1),
        name="small_heads",
    )(zf, zkw, bf, gik, cos, sup, sdn)


def _sortable(x):
    b = pltpu.bitcast(x, jnp.int32)
    return jnp.where(b < 0, b ^ jnp.int32(0x7FFFFFFF), b)


def _index_kernel(qi_ref, kte_ref, kto_ref, wi_ref, keep_ref, bias_ref, keys_sc, wb_sc, *, topk):
    tq, s = bias_ref.shape
    tc = IDX_TC
    i = pl.program_id(0)
    row0 = i * tq
    n_chunks = (row0 + tq + tc - 1) // tc
    n_tiles = tc // LANES

    for j in range(N_IDX_HEADS):
        wb_sc[j] = jnp.broadcast_to(wi_ref[:, j:j + 1], (tq, LANES))

    row_ids = row0 + lax.broadcasted_iota(jnp.int32, (tq, tc), 0)
    col_iota = lax.broadcasted_iota(jnp.int32, (tq, tc), 1)

    def score_chunk(c, carry):
        col0 = pl.multiple_of(c * tc, tc)
        kte = kte_ref[:, pl.ds(col0, tc)]
        kto = kto_ref[:, pl.ds(col0, tc)]
        score = jnp.zeros((tq, tc), jnp.float32)
        for p in range(N_IDX_HEADS // 2):
            qp = qi_ref[:, p * LANES:(p + 1) * LANES]
            re = jnp.maximum(jnp.dot(qp, kte, preferred_element_type=jnp.float32), 0.0)
            ro = jnp.maximum(jnp.dot(qp, kto, preferred_element_type=jnp.float32), 0.0)
            we = jnp.concatenate([wb_sc[2 * p]] * n_tiles, axis=1)
            wo = jnp.concatenate([wb_sc[2 * p + 1]] * n_tiles, axis=1)
            score = score + re * we + ro * wo
        causal = (col0 + col_iota) <= row_ids
        score = jnp.where(causal, score, NEG_BIG)
        keys_sc[:, pl.ds(col0, tc)] = _sortable(score)
        return carry

    lax.fori_loop(0, n_chunks, score_chunk, 0)

    @pl.when(n_chunks % 2 == 1)
    def _():
        col0 = pl.multiple_of(n_chunks * tc, tc)
        keys_sc[:, pl.ds(col0, tc)] = _sortable(jnp.full((tq, tc), NEG_BIG, jnp.float32))

    tcc = 2 * tc

    def search(rows):
        n_rows = rows.stop - rows.start

        def one_bit(b, thr, kept):
            cand = thr + lax.shift_left(jnp.int32(1), 31 - b)

            def count_chunk(c, cnt):
                col0 = pl.multiple_of(c * tcc, tcc)
                k = keys_sc[rows, pl.ds(col0, tcc)]
                for t in range(tcc // LANES):
                    ge = k[:, t * LANES:(t + 1) * LANES] >= cand
                    cnt = cnt + jnp.where(ge, 1.0, 0.0)
                return cnt

            cnt = lax.fori_loop(0, (n_chunks + 1) // 2, count_chunk,
                                jnp.zeros((n_rows, LANES), jnp.float32))
            total = jnp.sum(cnt, axis=1, keepdims=True)
            accept = total >= topk
            return jnp.where(accept, cand, thr), jnp.where(accept, total, kept)

        def bit_step(state):
            b, thr, kept, _ = state
            thr, kept = one_bit(b, thr, kept)
            thr, kept = one_bit(b + 1, thr, kept)
            unresolved = jnp.max(jnp.where(kept == topk, 0.0, 1.0))
            return b + 2, thr, kept, unresolved

        def search_on(state):
            b, _, _, unresolved = state
            return jnp.logical_and(b < 32, unresolved > 0.5)

        return lax.while_loop(
            search_on, bit_step,
            (jnp.int32(0), jnp.full((n_rows, LANES), _INT_MIN, jnp.int32),
             jnp.full((n_rows, LANES), float(2 * s), jnp.float32), jnp.float32(1.0)))[1]

    thr = jnp.concatenate([search(slice(g * IDX_SG, (g + 1) * IDX_SG))
                           for g in range(tq // IDX_SG)], axis=0)

    def bias_chunk(c, carry):
        col0 = pl.multiple_of(c * tc, tc)
        k = keys_sc[:, pl.ds(col0, tc)]
        causal = (col0 + col_iota) <= row_ids
        thr_w = jnp.concatenate([thr] * n_tiles, axis=1)
        keep = jnp.logical_and(k >= thr_w, causal)
        bias_ref[:, pl.ds(col0, tc)] = jnp.where(keep, keep_ref[...], NEG_BIG).astype(bias_ref.dtype)
        return carry

    lax.fori_loop(0, n_chunks, bias_chunk, 0)

    def fill_chunk(c, carry):
        col0 = pl.multiple_of(c * tc, tc)
        bias_ref[:, pl.ds(col0, tc)] = jnp.full((tq, tc), NEG_BIG, bias_ref.dtype)
        return carry

    lax.fori_loop(n_chunks, s // tc, fill_chunk, 0)


def _index_bias(qi, kte, kto, wi, topk, keep_value):
    s = qi.shape[0]
    tq = IDX_TQ
    return pl.pallas_call(
        functools.partial(_index_kernel, topk=topk),
        grid=(s // tq,),
        in_specs=[pl.BlockSpec((tq, IDX_Q_W), lambda i: (i, 0)),
                  pl.BlockSpec((LANES, s), lambda i: (0, 0)),
                  pl.BlockSpec((LANES, s), lambda i: (0, 0)),
                  pl.BlockSpec((tq, LANES), lambda i: (i, 0)),
                  pl.BlockSpec((1, IDX_TC), lambda i: (0, 0))],
        out_specs=pl.BlockSpec((tq, s), lambda i: (i, 0)),
        out_shape=jax.ShapeDtypeStruct((s, s), jnp.bfloat16),
        scratch_shapes=[pltpu.VMEM((tq, s), jnp.int32),
                        pltpu.VMEM((N_IDX_HEADS, tq, LANES), jnp.float32)],
        compiler_params=_cparams(1),
        name="index_topk_bias",
    )(qi, kte, kto, wi, jnp.full((1, IDX_TC), keep_value, jnp.float32))


def _attn_kernel(qt_ref, kt_ref, skip_ref, q_ref, k_ref, v_ref, *rest, fox):
    if fox:
        fq_ref, fk_ref, o_ref, m_sc, l_sc, acc_sc, s_sc, p_sc, alpha_sc = rest
    else:
        bias_ref, o_ref, m_sc, l_sc, acc_sc, s_sc, p_sc, alpha_sc = rest
    p_id = pl.program_id(1)
    qb = qt_ref[p_id]
    kb = kt_ref[p_id]
    tq, tk = q_ref.shape[0], k_ref.shape[0]
    rc = ATT_RC
    qk_scale = (HEAD_DIM ** -0.5) * LOG2E

    @pl.when(kb == qb)
    def _():
        m_sc[...] = jnp.full(m_sc.shape, NEG_BIG, jnp.float32)
        l_sc[...] = jnp.zeros(l_sc.shape, jnp.float32)
        acc_sc[...] = jnp.zeros(acc_sc.shape, jnp.float32)

    if fox:
        key_bias = (fq_ref[:, 0:1] - fk_ref[...]) * LOG2E
        col_minus_row = (lax.broadcasted_iota(jnp.int32, (rc, tk), 1)
                         - lax.broadcasted_iota(jnp.int32, (rc, tk), 0))

    def key_cols(t, diag):
        return (t + 1) * rc if diag else tk

    def logits_stage(t, diag):
        rows = slice(t * rc, (t + 1) * rc)
        kc = key_cols(t, diag)
        s = lax.dot_general(q_ref[rows, :], k_ref[0:kc, :], (((1,), (1,)), ((), ())),
                            preferred_element_type=jnp.float32) * qk_scale
        if fox:
            s = s + key_bias[:, 0:kc]
            if diag:
                s = jnp.where(col_minus_row[:, 0:kc] <= t * rc, s, NEG_BIG)
        else:
            s = s + bias_ref[rows, 0:kc].astype(jnp.float32)
        s_sc[t % 2, :, 0:kc] = s

    def softmax_stage(t, diag):
        rows = slice(t * rc, (t + 1) * rc)
        kc = key_cols(t, diag)
        s = s_sc[t % 2, :, 0:kc]
        m_prev = m_sc[rows, :]
        m_new = jnp.maximum(m_prev, jnp.max(s, axis=1, keepdims=True))
        alpha = jnp.exp2(m_prev - m_new)
        p = jnp.exp2(s - m_new)
        l_sc[rows, :] = alpha * l_sc[rows, :] + jnp.sum(p, axis=1, keepdims=True)
        m_sc[rows, :] = m_new
        alpha_sc[t % 2] = alpha
        p_sc[t % 2, :, 0:kc] = p.astype(p_sc.dtype)

    def value_stage(t, diag):
        rows = slice(t * rc, (t + 1) * rc)
        kc = key_cols(t, diag)
        acc_sc[rows, :] = alpha_sc[t % 2] * acc_sc[rows, :] + jnp.dot(
            p_sc[t % 2, :, 0:kc], v_ref[0:kc, :], preferred_element_type=jnp.float32)

    def all_chunks(diag):
        n_rc = tq // rc
        for t in range(n_rc + 2):
            if t < n_rc:
                logits_stage(t, diag)
            if 1 <= t <= n_rc:
                softmax_stage(t - 1, diag)
            if t >= 2:
                value_stage(t - 2, diag)

    pl.when(kb == qb)(functools.partial(all_chunks, True))
    live = skip_ref[pl.program_id(0), p_id] == 0
    pl.when(jnp.logical_and(kb != qb, live))(functools.partial(all_chunks, False))

    @pl.when(kb == 0)
    def _():
        o_ref[...] = (acc_sc[...] / l_sc[...]).astype(o_ref.dtype)


def _block_pairs(s):
    nq = s // ATT_TQ
    return [(a, b) for a in range(nq) for b in range(a, -1, -1)]


def _logit_bound2(g_q, g_k):
    return (HEAD_DIM * jnp.max(jnp.abs(g_q)) * jnp.max(jnp.abs(g_k)) * BF16_NORM_SLACK
            * (HEAD_DIM ** -0.5) * LOG2E)


def _fox_skip_flags(f_t, g_q, g_k):
    s = f_t.shape[1]
    bound2 = _logit_bound2(g_q, g_k)
    pairs = _block_pairs(s)
    qrow = jnp.asarray([a * ATT_TQ for a, _ in pairs], jnp.int32)
    kcol = jnp.asarray([b * ATT_TK + ATT_TK - 1 for _, b in pairs], jnp.int32)
    gap2 = (f_t[:, qrow] - f_t[:, kcol]) * LOG2E
    return (gap2 + 2.0 * bound2 < EXP2_ZERO_BELOW).astype(jnp.int32)


def _attention(qk, v, q_col, k_col, v_col, *, frow=None, skip=None, bias=None):
    s = qk.shape[0]
    n_heads = N_FOX_HEADS
    tq, tk = ATT_TQ, ATT_TK
    assert tq == tk
    pairs = _block_pairs(s)
    qt = jnp.asarray([a for a, _ in pairs], jnp.int32)
    kt = jnp.asarray([b for _, b in pairs], jnp.int32)
    fox = bias is None
    if skip is None:
        skip = jnp.zeros((n_heads, len(pairs)), jnp.int32)
    in_specs = [
        pl.BlockSpec((tq, HEAD_DIM), lambda h, p, qt, kt, sk: (qt[p], q_col + h)),
        pl.BlockSpec((tk, HEAD_DIM), lambda h, p, qt, kt, sk: (kt[p], k_col + h)),
        pl.BlockSpec((tk, HEAD_DIM), lambda h, p, qt, kt, sk: (kt[p], v_col + h)),
    ]
    if fox:
        in_specs += [pl.BlockSpec((None, 1, tq), lambda h, p, qt, kt, sk: (h, 0, qt[p])),
                     pl.BlockSpec((None, 1, tk), lambda h, p, qt, kt, sk: (h, 0, kt[p]))]
        operands = (qk, qk, v, frow, frow)
    else:
        in_specs += [pl.BlockSpec((tq, tk), lambda h, p, qt, kt, sk: (qt[p], kt[p]))]
        operands = (qk, qk, v, bias)
    grid_spec = pltpu.PrefetchScalarGridSpec(
        num_scalar_prefetch=3,
        grid=(n_heads, len(pairs)),
        in_specs=in_specs,
        out_specs=pl.BlockSpec((tq, HEAD_DIM), lambda h, p, qt, kt, sk: (qt[p], h)),
        scratch_shapes=[pltpu.VMEM((tq, 1), jnp.float32),
                        pltpu.VMEM((tq, 1), jnp.float32),
                        pltpu.VMEM((tq, HEAD_DIM), jnp.float32),
                        pltpu.VMEM((2, ATT_RC, tk), jnp.float32),
                        pltpu.VMEM((2, ATT_RC, tk), jnp.bfloat16),
                        pltpu.VMEM((2, ATT_RC, 1), jnp.float32)],
    )
    return pl.pallas_call(
        functools.partial(_attn_kernel, fox=fox),
        grid_spec=grid_spec,
        out_shape=jax.ShapeDtypeStruct((s, n_heads * HEAD_DIM), jnp.bfloat16),
        compiler_params=_cparams(2),
        name="fox_attention" if fox else "dsa_attention",
    )(qt, kt, skip, *operands)


def _attn_shift_kernel(qt_ref, kt_ref, skip_ref, q_ref, k_ref, v_ref, *rest, fox):
    if fox:
        fcol_ref, frow_ref, o_ref, acc_sc = rest
    else:
        bias_ref, o_ref, acc_sc = rest
    p_id = pl.program_id(1)
    qb = qt_ref[p_id]
    kb = kt_ref[p_id]
    tq, tk = q_ref.shape[0], k_ref.shape[0]
    rc = ATT_RC
    qk_scale = (HEAD_DIM ** -0.5) * LOG2E

    @pl.when(kb == qb)
    def _():
        acc_sc[...] = jnp.zeros(acc_sc.shape, jnp.float32)

    def chunks(diag):
        v1 = jnp.concatenate([v_ref[...], jnp.ones(v_ref.shape, v_ref.dtype)], axis=1)
        if fox:
            col_minus_row = (lax.broadcasted_iota(jnp.int32, (rc, tk), 1)
                             - lax.broadcasted_iota(jnp.int32, (rc, tk), 0))
        for t in range(tq // rc):
            rows = slice(t * rc, (t + 1) * rc)
            kc = (t + 1) * rc if diag else tk
            s = lax.dot_general(q_ref[rows, :], k_ref[0:kc, :], (((1,), (1,)), ((), ())),
                                preferred_element_type=jnp.float32) * qk_scale
            if fox:
                s = s + fcol_ref[rows, :] - frow_ref[:, 0:kc]
                if diag:
                    s = jnp.where(col_minus_row[:, 0:kc] <= t * rc, s, NEG_BIG)
            else:
                s = s + bias_ref[rows, 0:kc].astype(jnp.float32)
            p = jnp.exp2(s).astype(v_ref.dtype)
            acc_sc[rows, :] += jnp.dot(p, v1[0:kc, :], preferred_element_type=jnp.float32)

    pl.when(kb == qb)(functools.partial(chunks, True))
    live = skip_ref[pl.program_id(0), p_id] == 0
    pl.when(jnp.logical_and(kb != qb, live))(functools.partial(chunks, False))

    @pl.when(kb == 0)
    def _():
        o_ref[...] = (acc_sc[:, 0:HEAD_DIM] / acc_sc[:, HEAD_DIM:]).astype(o_ref.dtype)


def _attention_shift(qk, v, q_col, k_col, v_col, *, fcol=None, frow=None, skip=None, bias=None):
    s = qk.shape[0]
    n_heads = N_FOX_HEADS
    tq, tk = ATT_TQ, ATT_TK
    assert tq == tk
    pairs = _block_pairs(s)
    qt = jnp.asarray([a for a, _ in pairs], jnp.int32)
    kt = jnp.asarray([b for _, b in pairs], jnp.int32)
    fox = bias is None
    if skip is None:
        skip = jnp.zeros((n_heads, len(pairs)), jnp.int32)
    in_specs = [
        pl.BlockSpec((tq, HEAD_DIM), lambda h, p, qt, kt, sk: (qt[p], q_col + h)),
        pl.BlockSpec((tk, HEAD_DIM), lambda h, p, qt, kt, sk: (kt[p], k_col + h)),
        pl.BlockSpec((tk, HEAD_DIM), lambda h, p, qt, kt, sk: (kt[p], v_col + h)),
    ]
    if fox:
        in_specs += [pl.BlockSpec((None, tq, 1), lambda h, p, qt, kt, sk: (h, qt[p], 0)),
                     pl.BlockSpec((None, 1, tk), lambda h, p, qt, kt, sk: (h, 0, kt[p]))]
        operands = (qk, qk, v, fcol, frow)
    else:
        in_specs += [pl.BlockSpec((tq, tk), lambda h, p, qt, kt, sk: (qt[p], kt[p]))]
        operands = (qk, qk, v, bias)
    grid_spec = pltpu.PrefetchScalarGridSpec(
        num_scalar_prefetch=3,
        grid=(n_heads, len(pairs)),
        in_specs=in_specs,
        out_specs=pl.BlockSpec((tq, HEAD_DIM), lambda h, p, qt, kt, sk: (qt[p], h)),
        scratch_shapes=[pltpu.VMEM((tq, 2 * HEAD_DIM), jnp.float32)],
    )
    return pl.pallas_call(
        functools.partial(_attn_shift_kernel, fox=fox),
        grid_spec=grid_spec,
        out_shape=jax.ShapeDtypeStruct((s, n_heads * HEAD_DIM), jnp.bfloat16),
        compiler_params=_cparams(2),
        name="fox_attention_shift" if fox else "dsa_attention_shift",
    )(qt, kt, skip, *operands)


def _cast_kernel(x_ref, o_ref):
    o_ref[...] = x_ref[...].astype(o_ref.dtype)


def _cast_bf16(w):
    l, k, n = w.shape
    rows = math.gcd(CAST_ROWS, k)
    spec = pl.BlockSpec((None, rows, n), lambda i, j: (i, j, 0))
    return pl.pallas_call(
        _cast_kernel,
        grid=(l, k // rows),
        in_specs=[spec],
        out_specs=spec,
        out_shape=jax.ShapeDtypeStruct((l, k, n), jnp.bfloat16),
        compiler_params=_cparams(2),
        name="cast_bf16",
    )(w)


def _rope_angles(s, rot):
    half = rot // 2
    pos = jnp.arange(s, dtype=jnp.float32)
    inv = jnp.power(jnp.float32(ROPE_THETA), -2.0 * jnp.arange(half, dtype=jnp.float32) / rot)
    ang = pos[:, None] * inv[None, :]
    return jnp.cos(ang), jnp.sin(ang)


def _rope_tables(s, rot, period):
    half = rot // 2
    cos, sin = _rope_angles(s, rot)
    ones = jnp.ones((s, period - rot), jnp.float32)
    zeros = lambda n: jnp.zeros((s, n), jnp.float32)
    c = jnp.concatenate([cos, cos, ones], axis=1)
    s_up = jnp.concatenate([-sin, zeros(period - half)], axis=1)
    s_dn = jnp.concatenate([zeros(half), sin, zeros(period - rot)], axis=1)
    reps = LANES // period
    return tuple(jnp.tile(t, (1, reps)) for t in (c, s_up, s_dn))


def _pad_row(v, n):
    return jnp.pad(v, (0, n - v.shape[0])).reshape(1, n)


def kernel(x, p, g_ffn1, w1_gate, w1_up, w1_down, g_mix, w_in, b_f, g_qa, g_ka, g_qb, g_kb, g_ik,
           w_br_fox, w_br_dsa, w_o, g_ffn2, w2_gate, w2_up, w2_down, g_ple, w_ple_gate, w_ple_proj):
    b, s, d = x.shape
    assert b == 1
    bf16 = jnp.bfloat16
    topk = min(INDEX_TOPK_MAX, s // 4)
    idx_scale = (N_IDX_HEADS ** -0.5) * (IDX_DIM ** -0.5)
    rope_b = _rope_tables(s, HEAD_DIM // ROT_FRAC_DIV, HEAD_DIM)
    half_b = HEAD_DIM // ROT_FRAC_DIV // 2
    rope_i = _rope_tables(s, IDX_DIM // ROT_FRAC_DIV, IDX_DIM)
    half_i = IDX_DIM // ROT_FRAC_DIV // 2
    row_tab = lambda t: (t, (MM_TM, LANES), lambda i, j: (i, 0))
    res_tile = lambda h: (h, (MM_TM, MM_TN), lambda i, j: (i, j))
    blk = lambda cols: cols // MM_TN

    w1g, w1u, w1d = _cast_bf16(w1_gate), _cast_bf16(w1_up), _cast_bf16(w1_down)
    w2g, w2u, w2d = _cast_bf16(w2_gate), _cast_bf16(w2_up), _cast_bf16(w2_down)
    wbf, wbd, wo = _cast_bf16(w_br_fox), _cast_bf16(w_br_dsa), _cast_bf16(w_o)
    wpg, wpp = _cast_bf16(w_ple_gate), _cast_bf16(w_ple_proj)
    off_fa = 3 * FOX_W
    off_b = off_fa + N_FOX_HEADS
    off_ki = off_b + 3 * DSA_W + IDX_Q_W
    off_wi = off_ki + IDX_DIM
    off_g = off_wi + N_IDX_HEADS
    w_in_bf = w_in.astype(bf16)
    w_a = w_in_bf
    w_b = w_in_bf[:, :, off_b:off_ki]
    w_g = w_in_bf[:, :, off_g:]
    assert off_fa % LANES == 0 and off_g <= (off_ki // LANES + 1) * LANES
    kw_window = off_ki // LANES
    ki_lane, wi_lane = off_ki - kw_window * LANES, off_wi - kw_window * LANES
    p_bf = p.reshape(DEPTH, s, PLE_DIM).astype(bf16)

    def ffn(h, g, wg, wu, wd, layer, tag):
        u = _rmsnorm(h, g)
        a = _matmul([u], [wg, wu], [(0, 0), (0, 1)], [], _ep_swiglu, bf16,
                    layer=layer, name=tag + "_swiglu")
        return _matmul([a], [wd], [(0, 0)], [res_tile(h)],
                       functools.partial(_ep_residual, scale=0.5), jnp.float32,
                       layer=layer, name=tag + "_down")

    h = x.reshape(s, d)
    for i in range(DEPTH):
        h = ffn(h, g_ffn1[i], w1g, w1u, w1d, i, "ffn1")

        u = _rmsnorm(h, g_mix[i])
        gain_a = jnp.concatenate([jnp.tile(g_qa[i], N_FOX_HEADS), jnp.tile(g_ka[i], N_FOX_HEADS)])
        gain_b = jnp.concatenate([jnp.tile(g_qb[i], N_DSA_HEADS), jnp.tile(g_kb[i], N_DSA_HEADS)])
        col_vec = lambda v: (v.reshape(1, -1), (1, MM_TN), lambda i, j: (0, j))

        qk_a = _matmul([u], [w_a], [(0, 0)], [col_vec(gain_a)],
                       functools.partial(_ep_headnorm, rope_half=0), bf16,
                       layer=i, name="proj_qk_fox", n=2 * FOX_W)
        v_a = _matmul([u], [w_a], [(0, 0)], [], _ep_cast, bf16,
                      layer=i, name="proj_v_fox", n=FOX_W, rhs_off=[blk(2 * FOX_W)])
        qk_b = _matmul([u], [w_b], [(0, 0)], [col_vec(gain_b)] + [row_tab(t) for t in rope_b],
                       functools.partial(_ep_headnorm, rope_half=half_b), bf16,
                       layer=i, name="proj_qk_dsa", n=2 * DSA_W)
        v_b = _matmul([u], [w_b], [(0, 0)], [], _ep_cast, bf16,
                      layer=i, name="proj_v_dsa", n=DSA_W, rhs_off=[blk(2 * DSA_W)])
        q_i = _matmul([u], [w_b], [(0, 0)], [row_tab(t) for t in rope_i],
                      functools.partial(_ep_rope, rope_half=half_i), bf16,
                      layer=i, name="proj_qidx", n=IDX_Q_W, rhs_off=[blk(3 * DSA_W)])
        gates = _matmul([u], [w_g], [(0, 0)], [], _ep_sigmoid, bf16, layer=i, name="proj_gates")
        z_f = _matmul([u], [w_in_bf], [(0, 0)], [], _ep_cast, jnp.float32, layer=i,
                      name="proj_forget", tn=LANES, n=LANES, rhs_off=[off_fa // LANES])
        z_kw = _matmul([u], [w_in_bf], [(0, 0)], [], _ep_cast, jnp.float32, layer=i,
                       name="proj_idx_key", tn=LANES, n=LANES, rhs_off=[kw_window])

        f_cum, ki_e, ki_o, wi_s = _small(z_f, z_kw, _pad_row(b_f[i], LANES),
                                         _pad_row(g_ik[i], LANES), *rope_i, idx_scale,
                                         ki_lane, wi_lane)

        f_t = f_cum[:, :N_FOX_HEADS].T
        skip = _fox_skip_flags(f_t, g_qa[i], g_ka[i])
        shift_a = _logit_bound2(g_qa[i], g_ka[i])
        o_a = lax.cond(
            2.0 * shift_a < FIXED_SHIFT_MAX_RANGE,
            lambda: _attention_shift(qk_a, v_a, 0, N_FOX_HEADS, 0,
                                     fcol=(f_t * LOG2E - shift_a)[:, :, None],
                                     frow=(f_t * LOG2E)[:, None, :], skip=skip),
            lambda: _attention(qk_a, v_a, 0, N_FOX_HEADS, 0, frow=f_t[:, None, :], skip=skip))

        shift_b = _logit_bound2(g_qb[i], g_kb[i])
        fixed_b = 2.0 * shift_b < FIXED_SHIFT_MAX_RANGE
        bias = _index_bias(q_i, ki_e.T, ki_o.T, wi_s, topk, jnp.where(fixed_b, -shift_b, 0.0))
        o_b = lax.cond(
            fixed_b,
            lambda: _attention_shift(qk_b, v_b, 0, N_DSA_HEADS, 0, bias=bias),
            lambda: _attention(qk_b, v_b, 0, N_DSA_HEADS, 0, bias=bias))

        n_gate_blocks = D_MODEL // MM_TN
        y = _matmul([o_a, o_b], [wbf, wbd], [(0, 0), (1, 1)],
                    [(gates, (MM_TM, MM_TN), lambda i, j: (i, j)),
                     (gates, (MM_TM, MM_TN), lambda i, j: (i, j + n_gate_blocks))],
                    _ep_merge, bf16, layer=i, name="branch_merge")
        h = _matmul([y], [wo], [(0, 0)], [res_tile(h)],
                    functools.partial(_ep_residual, scale=1.0), jnp.float32,
                    layer=i, name="out_proj")

        h = ffn(h, g_ffn2[i], w2g, w2u, w2d, i, "ffn2")

        u = _rmsnorm(h, g_ple[i])
        h = _matmul([u, p_bf[i]], [wpg, wpp], [(0, 0), (1, 1)], [res_tile(h)],
                    _ep_ple, jnp.float32, layer=i, name="ple")
    return h.reshape(b, s, d)
```

```python
import functools
import math

import jax
import jax.numpy as jnp
from jax import lax
from jax.experimental import pallas as pl
from jax.experimental.pallas import tpu as pltpu

D_MODEL = 4096
SEQ = 8192
DEPTH = 4
HEAD_DIM = 128
N_FOX_HEADS = 16
N_DSA_HEADS = 16
N_IDX_HEADS = 32
IDX_DIM = 64
INDEX_TOPK_MAX = 256
D_FF = 3584
PLE_DIM = 256
ROPE_THETA = 500000.0
ROT_FRAC_DIV = 4
NORM_EPS = 1e-6
NEG_BIG = -1e30
LOG2E = 1.4426950408889634
EXP2_ZERO_BELOW = -160.0
FIXED_SHIFT_MAX_RANGE = 100.0
BF16_NORM_SLACK = 1.02
FOX_W = N_FOX_HEADS * HEAD_DIM
DSA_W = N_DSA_HEADS * HEAD_DIM
IDX_Q_W = N_IDX_HEADS * IDX_DIM

LANES = 128
VMEM_LIMIT = 56 * 1024 * 1024

MM_TM = 1024
MM_TN = 512
MM_TN_WIDE = 1024
NORM_TM = 512
CAST_ROWS = 512
ATT_TQ = 1024
ATT_TK = 1024
ATT_RC = 256
IDX_TQ = 256
IDX_TC = 512
IDX_SG = 128

_INT_MIN = -(2 ** 31)
COL_BITS = 14


def _cparams(n_axes):
    return pltpu.CompilerParams(
        dimension_semantics=("arbitrary",) * n_axes,
        vmem_limit_bytes=VMEM_LIMIT,
    )


def _rmsnorm_kernel(x_ref, g_ref, o_ref):
    x = x_ref[...]
    ms = jnp.mean(x * x, axis=-1, keepdims=True)
    o_ref[...] = (x * lax.rsqrt(ms + NORM_EPS) * g_ref[...]).astype(o_ref.dtype)


def _rmsnorm(x, g):
    m, d = x.shape
    return pl.pallas_call(
        _rmsnorm_kernel,
        grid=(m // NORM_TM,),
        in_specs=[pl.BlockSpec((NORM_TM, d), lambda i: (i, 0)),
                  pl.BlockSpec((1, d), lambda i: (0, 0))],
        out_specs=pl.BlockSpec((NORM_TM, d), lambda i: (i, 0)),
        out_shape=jax.ShapeDtypeStruct((m, d), jnp.bfloat16),
        compiler_params=_cparams(1),
        name="rmsnorm",
    )(x, g.reshape(1, d))


def _mm_kernel(*refs, n_lhs, n_rhs, pairs, n_extra, epilogue):
    lhs = refs[:n_lhs]
    rhs = refs[n_lhs:n_lhs + n_rhs]
    extra = refs[n_lhs + n_rhs:n_lhs + n_rhs + n_extra]
    out = refs[n_lhs + n_rhs + n_extra]
    accs = [jnp.dot(lhs[a][...], rhs[b][...], preferred_element_type=jnp.float32)
            for a, b in pairs]
    epilogue(accs, extra, out)


def _matmul(lhs, rhs, pairs, extras, epilogue, out_dtype, *, layer, name, tn=MM_TN, n=None,
            rhs_off=None):
    m = lhs[0].shape[0]
    n = rhs[0].shape[2] if n is None else n
    rhs_off = [0] * len(rhs) if rhs_off is None else rhs_off
    tm = MM_TM
    in_specs = []
    for a in lhs:
        in_specs.append(pl.BlockSpec((tm, a.shape[1]), lambda i, j: (i, 0)))
    for w, off in zip(rhs, rhs_off):
        in_specs.append(pl.BlockSpec((None, w.shape[1], tn),
                                     lambda i, j, off=off: (layer, 0, j + off)))
    for _, bs, im in extras:
        in_specs.append(pl.BlockSpec(bs, im))
    kern = functools.partial(_mm_kernel, n_lhs=len(lhs), n_rhs=len(rhs), pairs=pairs,
                             n_extra=len(extras), epilogue=epilogue)
    return pl.pallas_call(
        kern,
        grid=(m // tm, n // tn),
        in_specs=in_specs,
        out_specs=pl.BlockSpec((tm, tn), lambda i, j: (i, j)),
        out_shape=jax.ShapeDtypeStruct((m, n), out_dtype),
        compiler_params=_cparams(2),
        name=name,
    )(*lhs, *rhs, *[e[0] for e in extras])


def _ep_cast(accs, extra, out):
    out[...] = accs[0].astype(out.dtype)


def _ep_sigmoid(accs, extra, out):
    out[...] = jax.nn.sigmoid(accs[0]).astype(out.dtype)


def _ep_swiglu(accs, extra, out):
    out[...] = (jax.nn.silu(accs[0]) * accs[1]).astype(out.dtype)


def _ep_residual(accs, extra, out, *, scale):
    out[...] = extra[0][...] + scale * accs[0]


def _ep_merge(accs, extra, out):
    ga = extra[0][...].astype(jnp.float32)
    gb = extra[1][...].astype(jnp.float32)
    out[...] = (ga * accs[0] + gb * accs[1]).astype(out.dtype)


def _ep_ple(accs, extra, out):
    out[...] = extra[0][...] + jax.nn.sigmoid(accs[0]) * accs[1]


def _rope_lanes(y, cos, s_up, s_dn, half):
    up = pltpu.roll(y, LANES - half, 1)
    dn = pltpu.roll(y, half, 1)
    return y * cos + up * s_up + dn * s_dn


def _ep_headnorm(accs, extra, out, *, rope_half):
    acc = accs[0]
    gain = extra[0][...]
    groups = [slice(g * LANES, (g + 1) * LANES) for g in range(acc.shape[1] // LANES)]
    normed = []
    for sl in groups:
        x = acc[:, sl]
        ms = jnp.mean(x * x, axis=-1, keepdims=True)
        normed.append(x * lax.rsqrt(ms + NORM_EPS) * gain[:, sl])
    for sl, y in zip(groups, normed):
        if rope_half:
            y = _rope_lanes(y, extra[1][...], extra[2][...], extra[3][...], rope_half)
        out[:, sl] = y.astype(out.dtype)


def _ep_rope(accs, extra, out, *, rope_half):
    acc = accs[0]
    for g in range(acc.shape[1] // LANES):
        sl = slice(g * LANES, (g + 1) * LANES)
        y = _rope_lanes(acc[:, sl], extra[0][...], extra[1][...], extra[2][...], rope_half)
        out[:, sl] = y.astype(out.dtype)


def _small_kernel(zf_ref, zkw_ref, bf_ref, gik_ref, cos_ref, sup_ref, sdn_ref,
                  f_ref, kie_ref, kio_ref, wi_ref, carry_sc, *, idx_scale, ki_lane, wi_lane):
    s = zf_ref.shape[0]
    lane = lax.broadcasted_iota(jnp.int32, (s, LANES), 1)

    @pl.when(pl.program_id(0) == 0)
    def _():
        carry_sc[...] = jnp.zeros(carry_sc.shape, jnp.float32)

    x = jnp.where(lane < N_FOX_HEADS, zf_ref[...], 0.0) + bf_ref[...]
    logf = jnp.minimum(x, 0.0) - jnp.log1p(jnp.exp(-jnp.abs(x)))
    row = lax.broadcasted_iota(jnp.int32, (s, LANES), 0)
    d = 1
    while d < s:
        logf = logf + jnp.where(row >= d, pltpu.roll(logf, d, 0), 0.0)
        d *= 2
    f_ref[...] = logf + carry_sc[0:1, :]
    carry_sc[0:1, :] = f_ref[s - 1:s, :]
    zkw = zkw_ref[...]
    k = jnp.where(lane < IDX_DIM, pltpu.roll(zkw, (LANES - ki_lane) % LANES, 1), 0.0)
    ms = jnp.sum(k * k, axis=-1, keepdims=True) * (1.0 / IDX_DIM)
    k = k * lax.rsqrt(ms + NORM_EPS) * gik_ref[...]
    k = _rope_lanes(k, cos_ref[...], sup_ref[...], sdn_ref[...], IDX_DIM // ROT_FRAC_DIV // 2)
    k = jnp.where(lane < IDX_DIM, k, 0.0)
    kie_ref[...] = k.astype(kie_ref.dtype)
    kio_ref[...] = pltpu.roll(k, IDX_DIM, 1).astype(kio_ref.dtype)
    wi = pltpu.roll(zkw, (LANES - wi_lane) % LANES, 1)
    wi_ref[...] = jnp.where(lane < N_IDX_HEADS, wi, 0.0) * idx_scale


def _small(zf, zkw, bf, gik, cos, sup, sdn, idx_scale, ki_lane, wi_lane):
    s = zf.shape[0]
    tm = MM_TM
    rows = lambda w: pl.BlockSpec((tm, w), lambda i: (i, 0))
    vec = pl.BlockSpec((1, LANES), lambda i: (0, 0))
    return pl.pallas_call(
        functools.partial(_small_kernel, idx_scale=idx_scale, ki_lane=ki_lane, wi_lane=wi_lane),
        grid=(s // tm,),
        in_specs=[rows(LANES), rows(LANES), vec, vec, rows(LANES), rows(LANES), rows(LANES)],
        out_specs=[rows(LANES)] * 4,
        out_shape=[jax.ShapeDtypeStruct((s, LANES), jnp.float32),
                   jax.ShapeDtypeStruct((s, LANES), jnp.bfloat16),
                   jax.ShapeDtypeStruct((s, LANES), jnp.bfloat16),
                   jax.ShapeDtypeStruct((s, LANES), jnp.float32)],
        scratch_shapes=[pltpu.VMEM((8, LANES), jnp.float32)],
        compiler_params=_cparams(1),
        name="small_heads",
    )(zf, zkw, bf, gik, cos, sup, sdn)


def _sortable(x):
    b = pltpu.bitcast(x, jnp.int32)
    return jnp.where(b < 0, b ^ jnp.int32(0x7FFFFFFF), b)


def _index_kernel(qi_ref, kte_ref, kto_ref, wi_ref, keep_ref, bias_ref, keys_sc, wb_sc, *, topk):
    tq, s = bias_ref.shape
    tc = IDX_TC
    i = pl.program_id(0)
    row0 = i * tq
    n_chunks = (row0 + tq + tc - 1) // tc
    n_tiles = tc // LANES

    for j in range(N_IDX_HEADS):
        wb_sc[j] = jnp.broadcast_to(wi_ref[:, j:j + 1], (tq, LANES))

    row_ids = row0 + lax.broadcasted_iota(jnp.int32, (tq, tc), 0)
    col_iota = lax.broadcasted_iota(jnp.int32, (tq, tc), 1)

    def score_chunk(c, carry):
        col0 = pl.multiple_of(c * tc, tc)
        kte = kte_ref[:, pl.ds(col0, tc)]
        kto = kto_ref[:, pl.ds(col0, tc)]
        score = jnp.zeros((tq, tc), jnp.float32)
        for p in range(N_IDX_HEADS // 2):
            qp = qi_ref[:, p * LANES:(p + 1) * LANES]
            re = jnp.maximum(jnp.dot(qp, kte, preferred_element_type=jnp.float32), 0.0)
            ro = jnp.maximum(jnp.dot(qp, kto, preferred_element_type=jnp.float32), 0.0)
            we = jnp.concatenate([wb_sc[2 * p]] * n_tiles, axis=1)
            wo = jnp.concatenate([wb_sc[2 * p + 1]] * n_tiles, axis=1)
            score = score + re * we + ro * wo
        causal = (col0 + col_iota) <= row_ids
        score = jnp.where(score == 0.0, 0.0, score)
        score = jnp.where(causal, score, NEG_BIG)
        keys_sc[:, pl.ds(col0, tc)] = _sortable(score)
        return carry

    lax.fori_loop(0, n_chunks, score_chunk, 0)

    @pl.when(n_chunks % 2 == 1)
    def _():
        col0 = pl.multiple_of(n_chunks * tc, tc)
        keys_sc[:, pl.ds(col0, tc)] = _sortable(jnp.full((tq, tc), NEG_BIG, jnp.float32))

    tcc = 2 * tc

    def search(rows):
        n_rows = rows.stop - rows.start

        def one_bit(b, thr, kept):
            cand = thr + lax.shift_left(jnp.int32(1), 31 - b)

            def count_chunk(c, cnt):
                col0 = pl.multiple_of(c * tcc, tcc)
                k = keys_sc[rows, pl.ds(col0, tcc)]
                for t in range(tcc // LANES):
                    ge = k[:, t * LANES:(t + 1) * LANES] >= cand
                    cnt = cnt + jnp.where(ge, 1.0, 0.0)
                return cnt

            cnt = lax.fori_loop(0, (n_chunks + 1) // 2, count_chunk,
                                jnp.zeros((n_rows, LANES), jnp.float32))
            total = jnp.sum(cnt, axis=1, keepdims=True)
            accept = total >= topk
            return jnp.where(accept, cand, thr), jnp.where(accept, total, kept)

        def bit_step(state):
            b, thr, kept, _ = state
            thr, kept = one_bit(b, thr, kept)
            thr, kept = one_bit(b + 1, thr, kept)
            unresolved = jnp.max(jnp.where(kept == topk, 0.0, 1.0))
            return b + 2, thr, kept, unresolved

        def search_on(state):
            b, _, _, unresolved = state
            return jnp.logical_and(b < 32, unresolved > 0.5)

        _, thr, kept, _ = lax.while_loop(
            search_on, bit_step,
            (jnp.int32(0), jnp.full((n_rows, LANES), _INT_MIN, jnp.int32),
             jnp.full((n_rows, LANES), float(2 * s), jnp.float32), jnp.float32(1.0)))

        def tie_bound():
            def count(pred):
                def chunk(c, cnt):
                    col0 = pl.multiple_of(c * tcc, tcc)
                    k = keys_sc[rows, pl.ds(col0, tcc)]
                    for t in range(tcc // LANES):
                        hit = pred(k[:, t * LANES:(t + 1) * LANES], col0 + t * LANES)
                        cnt = cnt + jnp.where(hit, 1.0, 0.0)
                    return cnt
                cnt = lax.fori_loop(0, (n_chunks + 1) // 2, chunk,
                                    jnp.zeros((n_rows, LANES), jnp.float32))
                return jnp.sum(cnt, axis=1, keepdims=True)

            lane = lax.broadcasted_iota(jnp.int32, (n_rows, LANES), 1)
            need = topk - count(lambda k, col: k > thr)

            def col_bit(j, bound):
                cand = bound + lax.shift_left(jnp.int32(1), COL_BITS - 1 - j)
                below = count(lambda k, col: jnp.logical_and(k == thr, col + lane < cand))
                return jnp.where(below <= need, cand, bound)

            return lax.fori_loop(0, COL_BITS, col_bit, jnp.zeros((n_rows, LANES), jnp.int32))

        ties = jnp.max(jnp.where(kept > topk, 1.0, 0.0)) > 0.5
        bound = lax.cond(ties, tie_bound, lambda: jnp.full((n_rows, LANES), s, jnp.int32))
        return thr, bound

    found = [search(slice(g * IDX_SG, (g + 1) * IDX_SG)) for g in range(tq // IDX_SG)]
    thr = jnp.concatenate([f[0] for f in found], axis=0)
    bound = jnp.concatenate([f[1] for f in found], axis=0)

    def bias_chunk(c, carry):
        col0 = pl.multiple_of(c * tc, tc)
        k = keys_sc[:, pl.ds(col0, tc)]
        cols = col0 + col_iota
        thr_w = jnp.concatenate([thr] * n_tiles, axis=1)
        bound_w = jnp.concatenate([bound] * n_tiles, axis=1)
        keep = jnp.logical_or(k > thr_w, jnp.logical_and(k == thr_w, cols < bound_w))
        keep = jnp.logical_and(keep, cols <= row_ids)
        bias_ref[:, pl.ds(col0, tc)] = jnp.where(keep, keep_ref[...], NEG_BIG).astype(bias_ref.dtype)
        return carry

    lax.fori_loop(0, n_chunks, bias_chunk, 0)

    def fill_chunk(c, carry):
        col0 = pl.multiple_of(c * tc, tc)
        bias_ref[:, pl.ds(col0, tc)] = jnp.full((tq, tc), NEG_BIG, bias_ref.dtype)
        return carry

    lax.fori_loop(n_chunks, s // tc, fill_chunk, 0)


def _index_bias(qi, kte, kto, wi, topk, keep_value):
    s = qi.shape[0]
    tq = IDX_TQ
    return pl.pallas_call(
        functools.partial(_index_kernel, topk=topk),
        grid=(s // tq,),
        in_specs=[pl.BlockSpec((tq, IDX_Q_W), lambda i: (i, 0)),
                  pl.BlockSpec((LANES, s), lambda i: (0, 0)),
                  pl.BlockSpec((LANES, s), lambda i: (0, 0)),
                  pl.BlockSpec((tq, LANES), lambda i: (i, 0)),
                  pl.BlockSpec((1, IDX_TC), lambda i: (0, 0))],
        out_specs=pl.BlockSpec((tq, s), lambda i: (i, 0)),
        out_shape=jax.ShapeDtypeStruct((s, s), jnp.bfloat16),
        scratch_shapes=[pltpu.VMEM((tq, s), jnp.int32),
                        pltpu.VMEM((N_IDX_HEADS, tq, LANES), jnp.float32)],
        compiler_params=_cparams(1),
        name="index_topk_bias",
    )(qi, kte, kto, wi, jnp.full((1, IDX_TC), keep_value, jnp.float32))


def _attn_kernel(qt_ref, kt_ref, skip_ref, q_ref, k_ref, v_ref, *rest, fox):
    if fox:
        fq_ref, fk_ref, o_ref, m_sc, l_sc, acc_sc, s_sc, p_sc, alpha_sc = rest
    else:
        bias_ref, o_ref, m_sc, l_sc, acc_sc, s_sc, p_sc, alpha_sc = rest
    p_id = pl.program_id(1)
    qb = qt_ref[p_id]
    kb = kt_ref[p_id]
    tq, tk = q_ref.shape[0], k_ref.shape[0]
    rc = ATT_RC
    qk_scale = (HEAD_DIM ** -0.5) * LOG2E

    @pl.when(kb == qb)
    def _():
        m_sc[...] = jnp.full(m_sc.shape, NEG_BIG, jnp.float32)
        l_sc[...] = jnp.zeros(l_sc.shape, jnp.float32)
        acc_sc[...] = jnp.zeros(acc_sc.shape, jnp.float32)

    if fox:
        key_bias = (fq_ref[:, 0:1] - fk_ref[...]) * LOG2E
        col_minus_row = (lax.broadcasted_iota(jnp.int32, (rc, tk), 1)
                         - lax.broadcasted_iota(jnp.int32, (rc, tk), 0))

    def key_cols(t, diag):
        return (t + 1) * rc if diag else tk

    def logits_stage(t, diag):
        rows = slice(t * rc, (t + 1) * rc)
        kc = key_cols(t, diag)
        s = lax.dot_general(q_ref[rows, :], k_ref[0:kc, :], (((1,), (1,)), ((), ())),
                            preferred_element_type=jnp.float32) * qk_scale
        if fox:
            s = s + key_bias[:, 0:kc]
            if diag:
                s = jnp.where(col_minus_row[:, 0:kc] <= t * rc, s, NEG_BIG)
        else:
            s = s + bias_ref[rows, 0:kc].astype(jnp.float32)
        s_sc[t % 2, :, 0:kc] = s

    def softmax_stage(t, diag):
        rows = slice(t * rc, (t + 1) * rc)
        kc = key_cols(t, diag)
        s = s_sc[t % 2, :, 0:kc]
        m_prev = m_sc[rows, :]
        m_new = jnp.maximum(m_prev, jnp.max(s, axis=1, keepdims=True))
        alpha = jnp.exp2(m_prev - m_new)
        p = jnp.exp2(s - m_new)
        l_sc[rows, :] = alpha * l_sc[rows, :] + jnp.sum(p, axis=1, keepdims=True)
        m_sc[rows, :] = m_new
        alpha_sc[t % 2] = alpha
        p_sc[t % 2, :, 0:kc] = p.astype(p_sc.dtype)

    def value_stage(t, diag):
        rows = slice(t * rc, (t + 1) * rc)
        kc = key_cols(t, diag)
        acc_sc[rows, :] = alpha_sc[t % 2] * acc_sc[rows, :] + jnp.dot(
            p_sc[t % 2, :, 0:kc], v_ref[0:kc, :], preferred_element_type=jnp.float32)

    def all_chunks(diag):
        n_rc = tq // rc
        for t in range(n_rc + 2):
            if t < n_rc:
                logits_stage(t, diag)
            if 1 <= t <= n_rc:
                softmax_stage(t - 1, diag)
            if t >= 2:
                value_stage(t - 2, diag)

    pl.when(kb == qb)(functools.partial(all_chunks, True))
    live = skip_ref[pl.program_id(0), p_id] == 0
    pl.when(jnp.logical_and(kb != qb, live))(functools.partial(all_chunks, False))

    @pl.when(kb == 0)
    def _():
        o_ref[...] = (acc_sc[...] / l_sc[...]).astype(o_ref.dtype)


def _block_pairs(s):
    nq = s // ATT_TQ
    return [(a, b) for a in range(nq) for b in range(a, -1, -1)]


def _logit_bound2(g_q, g_k):
    return (HEAD_DIM * jnp.max(jnp.abs(g_q)) * jnp.max(jnp.abs(g_k)) * BF16_NORM_SLACK
            * (HEAD_DIM ** -0.5) * LOG2E)


def _fox_skip_flags(f_t, g_q, g_k):
    s = f_t.shape[1]
    bound2 = _logit_bound2(g_q, g_k)
    pairs = _block_pairs(s)
    qrow = jnp.asarray([a * ATT_TQ for a, _ in pairs], jnp.int32)
    kcol = jnp.asarray([b * ATT_TK + ATT_TK - 1 for _, b in pairs], jnp.int32)
    gap2 = (f_t[:, qrow] - f_t[:, kcol]) * LOG2E
    return (gap2 + 2.0 * bound2 < EXP2_ZERO_BELOW).astype(jnp.int32)


def _attention(qk, v, q_col, k_col, v_col, *, frow=None, skip=None, bias=None):
    s = qk.shape[0]
    n_heads = N_FOX_HEADS
    tq, tk = ATT_TQ, ATT_TK
    assert tq == tk
    pairs = _block_pairs(s)
    qt = jnp.asarray([a for a, _ in pairs], jnp.int32)
    kt = jnp.asarray([b for _, b in pairs], jnp.int32)
    fox = bias is None
    if skip is None:
        skip = jnp.zeros((n_heads, len(pairs)), jnp.int32)
    in_specs = [
        pl.BlockSpec((tq, HEAD_DIM), lambda h, p, qt, kt, sk: (qt[p], q_col + h)),
        pl.BlockSpec((tk, HEAD_DIM), lambda h, p, qt, kt, sk: (kt[p], k_col + h)),
        pl.BlockSpec((tk, HEAD_DIM), lambda h, p, qt, kt, sk: (kt[p], v_col + h)),
    ]
    if fox:
        in_specs += [pl.BlockSpec((None, 1, tq), lambda h, p, qt, kt, sk: (h, 0, qt[p])),
                     pl.BlockSpec((None, 1, tk), lambda h, p, qt, kt, sk: (h, 0, kt[p]))]
        operands = (qk, qk, v, frow, frow)
    else:
        in_specs += [pl.BlockSpec((tq, tk), lambda h, p, qt, kt, sk: (qt[p], kt[p]))]
        operands = (qk, qk, v, bias)
    grid_spec = pltpu.PrefetchScalarGridSpec(
        num_scalar_prefetch=3,
        grid=(n_heads, len(pairs)),
        in_specs=in_specs,
        out_specs=pl.BlockSpec((tq, HEAD_DIM), lambda h, p, qt, kt, sk: (qt[p], h)),
        scratch_shapes=[pltpu.VMEM((tq, 1), jnp.float32),
                        pltpu.VMEM((tq, 1), jnp.float32),
                        pltpu.VMEM((tq, HEAD_DIM), jnp.float32),
                        pltpu.VMEM((2, ATT_RC, tk), jnp.float32),
                        pltpu.VMEM((2, ATT_RC, tk), jnp.bfloat16),
                        pltpu.VMEM((2, ATT_RC, 1), jnp.float32)],
    )
    return pl.pallas_call(
        functools.partial(_attn_kernel, fox=fox),
        grid_spec=grid_spec,
        out_shape=jax.ShapeDtypeStruct((s, n_heads * HEAD_DIM), jnp.bfloat16),
        compiler_params=_cparams(2),
        name="fox_attention" if fox else "dsa_attention",
    )(qt, kt, skip, *operands)


def _attn_shift_kernel(qt_ref, kt_ref, skip_ref, q_ref, k_ref, v_ref, *rest, fox):
    if fox:
        fcol_ref, frow_ref, o_ref, acc_sc = rest
    else:
        bias_ref, o_ref, acc_sc = rest
    p_id = pl.program_id(1)
    qb = qt_ref[p_id]
    kb = kt_ref[p_id]
    tq, tk = q_ref.shape[0], k_ref.shape[0]
    rc = ATT_RC
    qk_scale = (HEAD_DIM ** -0.5) * LOG2E

    @pl.when(kb == qb)
    def _():
        acc_sc[...] = jnp.zeros(acc_sc.shape, jnp.float32)

    def chunks(diag):
        v1 = jnp.concatenate([v_ref[...], jnp.ones(v_ref.shape, v_ref.dtype)], axis=1)
        if fox:
            col_minus_row = (lax.broadcasted_iota(jnp.int32, (rc, tk), 1)
                             - lax.broadcasted_iota(jnp.int32, (rc, tk), 0))
        for t in range(tq // rc):
            rows = slice(t * rc, (t + 1) * rc)
            kc = (t + 1) * rc if diag else tk
            s = lax.dot_general(q_ref[rows, :], k_ref[0:kc, :], (((1,), (1,)), ((), ())),
                                preferred_element_type=jnp.float32) * qk_scale
            if fox:
                s = s + fcol_ref[rows, :] - frow_ref[:, 0:kc]
                if diag:
                    s = jnp.where(col_minus_row[:, 0:kc] <= t * rc, s, NEG_BIG)
            else:
                s = s + bias_ref[rows, 0:kc].astype(jnp.float32)
            p = jnp.exp2(s).astype(v_ref.dtype)
            acc_sc[rows, :] += jnp.dot(p, v1[0:kc, :], preferred_element_type=jnp.float32)

    pl.when(kb == qb)(functools.partial(chunks, True))
    live = skip_ref[pl.program_id(0), p_id] == 0
    pl.when(jnp.logical_and(kb != qb, live))(functools.partial(chunks, False))

    @pl.when(kb == 0)
    def _():
        o_ref[...] = (acc_sc[:, 0:HEAD_DIM] / acc_sc[:, HEAD_DIM:]).astype(o_ref.dtype)


def _attention_shift(qk, v, q_col, k_col, v_col, *, fcol=None, frow=None, skip=None, bias=None):
    s = qk.shape[0]
    n_heads = N_FOX_HEADS
    tq, tk = ATT_TQ, ATT_TK
    assert tq == tk
    pairs = _block_pairs(s)
    qt = jnp.asarray([a for a, _ in pairs], jnp.int32)
    kt = jnp.asarray([b for _, b in pairs], jnp.int32)
    fox = bias is None
    if skip is None:
        skip = jnp.zeros((n_heads, len(pairs)), jnp.int32)
    in_specs = [
        pl.BlockSpec((tq, HEAD_DIM), lambda h, p, qt, kt, sk: (qt[p], q_col + h)),
        pl.BlockSpec((tk, HEAD_DIM), lambda h, p, qt, kt, sk: (kt[p], k_col + h)),
        pl.BlockSpec((tk, HEAD_DIM), lambda h, p, qt, kt, sk: (kt[p], v_col + h)),
    ]
    if fox:
        in_specs += [pl.BlockSpec((None, tq, 1), lambda h, p, qt, kt, sk: (h, qt[p], 0)),
                     pl.BlockSpec((None, 1, tk), lambda h, p, qt, kt, sk: (h, 0, kt[p]))]
        operands = (qk, qk, v, fcol, frow)
    else:
        in_specs += [pl.BlockSpec((tq, tk), lambda h, p, qt, kt, sk: (qt[p], kt[p]))]
        operands = (qk, qk, v, bias)
    grid_spec = pltpu.PrefetchScalarGridSpec(
        num_scalar_prefetch=3,
        grid=(n_heads, len(pairs)),
        in_specs=in_specs,
        out_specs=pl.BlockSpec((tq, HEAD_DIM), lambda h, p, qt, kt, sk: (qt[p], h)),
        scratch_shapes=[pltpu.VMEM((tq, 2 * HEAD_DIM), jnp.float32)],
    )
    return pl.pallas_call(
        functools.partial(_attn_shift_kernel, fox=fox),
        grid_spec=grid_spec,
        out_shape=jax.ShapeDtypeStruct((s, n_heads * HEAD_DIM), jnp.bfloat16),
        compiler_params=_cparams(2),
        name="fox_attention_shift" if fox else "dsa_attention_shift",
    )(qt, kt, skip, *operands)


def _cast_kernel(x_ref, o_ref):
    o_ref[...] = x_ref[...].astype(o_ref.dtype)


def _cast_bf16(w):
    l, k, n = w.shape
    rows = math.gcd(CAST_ROWS, k)
    spec = pl.BlockSpec((None, rows, n), lambda i, j: (i, j, 0))
    return pl.pallas_call(
        _cast_kernel,
        grid=(l, k // rows),
        in_specs=[spec],
        out_specs=spec,
        out_shape=jax.ShapeDtypeStruct((l, k, n), jnp.bfloat16),
        compiler_params=_cparams(2),
        name="cast_bf16",
    )(w)


def _rope_angles(s, rot):
    half = rot // 2
    pos = jnp.arange(s, dtype=jnp.float32)
    inv = jnp.power(jnp.float32(ROPE_THETA), -2.0 * jnp.arange(half, dtype=jnp.float32) / rot)
    ang = pos[:, None] * inv[None, :]
    return jnp.cos(ang), jnp.sin(ang)


def _rope_tables(s, rot, period):
    half = rot // 2
    cos, sin = _rope_angles(s, rot)
    ones = jnp.ones((s, period - rot), jnp.float32)
    zeros = lambda n: jnp.zeros((s, n), jnp.float32)
    c = jnp.concatenate([cos, cos, ones], axis=1)
    s_up = jnp.concatenate([-sin, zeros(period - half)], axis=1)
    s_dn = jnp.concatenate([zeros(half), sin, zeros(period - rot)], axis=1)
    reps = LANES // period
    return tuple(jnp.tile(t, (1, reps)) for t in (c, s_up, s_dn))


def _pad_row(v, n):
    return jnp.pad(v, (0, n - v.shape[0])).reshape(1, n)


def kernel(x, p, g_ffn1, w1_gate, w1_up, w1_down, g_mix, w_in, b_f, g_qa, g_ka, g_qb, g_kb, g_ik,
           w_br_fox, w_br_dsa, w_o, g_ffn2, w2_gate, w2_up, w2_down, g_ple, w_ple_gate, w_ple_proj):
    b, s, d = x.shape
    assert b == 1
    bf16 = jnp.bfloat16
    topk = min(INDEX_TOPK_MAX, s // 4)
    idx_scale = (N_IDX_HEADS ** -0.5) * (IDX_DIM ** -0.5)
    rope_b = _rope_tables(s, HEAD_DIM // ROT_FRAC_DIV, HEAD_DIM)
    half_b = HEAD_DIM // ROT_FRAC_DIV // 2
    rope_i = _rope_tables(s, IDX_DIM // ROT_FRAC_DIV, IDX_DIM)
    half_i = IDX_DIM // ROT_FRAC_DIV // 2
    row_tab = lambda t: (t, (MM_TM, LANES), lambda i, j: (i, 0))
    res_tile = lambda h: (h, (MM_TM, MM_TN), lambda i, j: (i, j))
    blk = lambda cols: cols // MM_TN_WIDE

    w1g, w1u, w1d = _cast_bf16(w1_gate), _cast_bf16(w1_up), _cast_bf16(w1_down)
    w2g, w2u, w2d = _cast_bf16(w2_gate), _cast_bf16(w2_up), _cast_bf16(w2_down)
    wbf, wbd, wo = _cast_bf16(w_br_fox), _cast_bf16(w_br_dsa), _cast_bf16(w_o)
    wpg, wpp = _cast_bf16(w_ple_gate), _cast_bf16(w_ple_proj)
    off_fa = 3 * FOX_W
    off_b = off_fa + N_FOX_HEADS
    off_ki = off_b + 3 * DSA_W + IDX_Q_W
    off_wi = off_ki + IDX_DIM
    off_g = off_wi + N_IDX_HEADS
    w_in_bf = w_in.astype(bf16)
    w_a = w_in_bf
    w_b = w_in_bf[:, :, off_b:off_ki]
    w_g = w_in_bf[:, :, off_g:]
    assert off_fa % LANES == 0 and off_g <= (off_ki // LANES + 1) * LANES
    kw_window = off_ki // LANES
    ki_lane, wi_lane = off_ki - kw_window * LANES, off_wi - kw_window * LANES
    p_bf = p.reshape(DEPTH, s, PLE_DIM).astype(bf16)

    def ffn(h, g, wg, wu, wd, layer, tag):
        u = _rmsnorm(h, g)
        a = _matmul([u], [wg, wu], [(0, 0), (0, 1)], [], _ep_swiglu, bf16,
                    layer=layer, name=tag + "_swiglu")
        return _matmul([a], [wd], [(0, 0)], [res_tile(h)],
                       functools.partial(_ep_residual, scale=0.5), jnp.float32,
                       layer=layer, name=tag + "_down")

    h = x.reshape(s, d)
    for i in range(DEPTH):
        h = ffn(h, g_ffn1[i], w1g, w1u, w1d, i, "ffn1")

        u = _rmsnorm(h, g_mix[i])
        gain_a = jnp.concatenate([jnp.tile(g_qa[i], N_FOX_HEADS), jnp.tile(g_ka[i], N_FOX_HEADS)])
        gain_b = jnp.concatenate([jnp.tile(g_qb[i], N_DSA_HEADS), jnp.tile(g_kb[i], N_DSA_HEADS)])
        col_vec = lambda v: (v.reshape(1, -1), (1, MM_TN_WIDE), lambda i, j: (0, j))

        qk_a = _matmul([u], [w_a], [(0, 0)], [col_vec(gain_a)],
                       functools.partial(_ep_headnorm, rope_half=0), bf16,
                       layer=i, name="proj_qk_fox", tn=MM_TN_WIDE, n=2 * FOX_W)
        v_a = _matmul([u], [w_a], [(0, 0)], [], _ep_cast, bf16,
                      layer=i, name="proj_v_fox", tn=MM_TN_WIDE, n=FOX_W, rhs_off=[blk(2 * FOX_W)])
        qk_b = _matmul([u], [w_b], [(0, 0)], [col_vec(gain_b)] + [row_tab(t) for t in rope_b],
                       functools.partial(_ep_headnorm, rope_half=half_b), bf16,
                       layer=i, name="proj_qk_dsa", tn=MM_TN_WIDE, n=2 * DSA_W)
        v_b = _matmul([u], [w_b], [(0, 0)], [], _ep_cast, bf16,
                      layer=i, name="proj_v_dsa", tn=MM_TN_WIDE, n=DSA_W, rhs_off=[blk(2 * DSA_W)])
        q_i = _matmul([u], [w_b], [(0, 0)], [row_tab(t) for t in rope_i],
                      functools.partial(_ep_rope, rope_half=half_i), bf16,
                      layer=i, name="proj_qidx", tn=MM_TN_WIDE, n=IDX_Q_W, rhs_off=[blk(3 * DSA_W)])
        gates = _matmul([u], [w_g], [(0, 0)], [], _ep_sigmoid, bf16, layer=i, name="proj_gates", tn=MM_TN_WIDE)
        z_f = _matmul([u], [w_in_bf], [(0, 0)], [], _ep_cast, jnp.float32, layer=i,
                      name="proj_forget", tn=LANES, n=LANES, rhs_off=[off_fa // LANES])
        z_kw = _matmul([u], [w_in_bf], [(0, 0)], [], _ep_cast, jnp.float32, layer=i,
                       name="proj_idx_key", tn=LANES, n=LANES, rhs_off=[kw_window])

        f_cum, ki_e, ki_o, wi_s = _small(z_f, z_kw, _pad_row(b_f[i], LANES),
                                         _pad_row(g_ik[i], LANES), *rope_i, idx_scale,
                                         ki_lane, wi_lane)

        f_t = f_cum[:, :N_FOX_HEADS].T
        skip = _fox_skip_flags(f_t, g_qa[i], g_ka[i])
        shift_a = _logit_bound2(g_qa[i], g_ka[i])
        o_a = lax.cond(
            2.0 * shift_a < FIXED_SHIFT_MAX_RANGE,
            lambda: _attention_shift(qk_a, v_a, 0, N_FOX_HEADS, 0,
                                     fcol=(f_t * LOG2E - shift_a)[:, :, None],
                                     frow=(f_t * LOG2E)[:, None, :], skip=skip),
            lambda: _attention(qk_a, v_a, 0, N_FOX_HEADS, 0, frow=f_t[:, None, :], skip=skip))

        shift_b = _logit_bound2(g_qb[i], g_kb[i])
        fixed_b = 2.0 * shift_b < FIXED_SHIFT_MAX_RANGE
        bias = _index_bias(q_i, ki_e.T, ki_o.T, wi_s, topk, jnp.where(fixed_b, -shift_b, 0.0))
        o_b = lax.cond(
            fixed_b,
            lambda: _attention_shift(qk_b, v_b, 0, N_DSA_HEADS, 0, bias=bias),
            lambda: _attention(qk_b, v_b, 0, N_DSA_HEADS, 0, bias=bias))

        n_gate_blocks = D_MODEL // MM_TN
        y = _matmul([o_a, o_b], [wbf, wbd], [(0, 0), (1, 1)],
                    [(gates, (MM_TM, MM_TN), lambda i, j: (i, j)),
                     (gates, (MM_TM, MM_TN), lambda i, j: (i, j + n_gate_blocks))],
                    _ep_merge, bf16, layer=i, name="branch_merge")
        h = _matmul([y], [wo], [(0, 0)], [res_tile(h)],
                    functools.partial(_ep_residual, scale=1.0), jnp.float32,
                    layer=i, name="out_proj")

        h = ffn(h, g_ffn2[i], w2g, w2u, w2d, i, "ffn2")

        u = _rmsnorm(h, g_ple[i])
        h = _matmul([u, p_bf[i]], [wpg, wpp], [(0, 0), (1, 1)], [res_tile(h)],
                    _ep_ple, jnp.float32, layer=i, name="ple")
    return h.reshape(b, s, d)
```

```python
import functools
import math

import jax
import jax.numpy as jnp
from jax import lax
from jax.experimental import pallas as pl
from jax.experimental.pallas import tpu as pltpu

D_MODEL = 4096
SEQ = 8192
DEPTH = 4
HEAD_DIM = 128
N_FOX_HEADS = 16
N_DSA_HEADS = 16
N_IDX_HEADS = 32
IDX_DIM = 64
INDEX_TOPK_MAX = 256
D_FF = 3584
PLE_DIM = 256
ROPE_THETA = 500000.0
ROT_FRAC_DIV = 4
NORM_EPS = 1e-6
NEG_BIG = -1e30
LOG2E = 1.4426950408889634
EXP2_ZERO_BELOW = -160.0
FIXED_SHIFT_MAX_RANGE = 100.0
BF16_NORM_SLACK = 1.02
FOX_W = N_FOX_HEADS * HEAD_DIM
DSA_W = N_DSA_HEADS * HEAD_DIM
IDX_Q_W = N_IDX_HEADS * IDX_DIM

LANES = 128
VMEM_LIMIT = 56 * 1024 * 1024

MM_TM = 1024
MM_TN = 512
MM_TN_WIDE = 1024
NORM_TM = 512
CAST_ROWS = 512
ATT_TQ = 1024
ATT_TK = 1024
ATT_RC = 256
IDX_TQ = 256
IDX_TC = 512
IDX_SG = 128

_INT_MIN = -(2 ** 31)
COL_BITS = 14


def _cparams(n_axes):
    return pltpu.CompilerParams(
        dimension_semantics=("arbitrary",) * n_axes,
        vmem_limit_bytes=VMEM_LIMIT,
    )


def _rmsnorm_kernel(x_ref, g_ref, o_ref):
    x = x_ref[...]
    ms = jnp.mean(x * x, axis=-1, keepdims=True)
    o_ref[...] = (x * lax.rsqrt(ms + NORM_EPS) * g_ref[...]).astype(o_ref.dtype)


def _rmsnorm(x, g):
    m, d = x.shape
    return pl.pallas_call(
        _rmsnorm_kernel,
        grid=(m // NORM_TM,),
        in_specs=[pl.BlockSpec((NORM_TM, d), lambda i: (i, 0)),
                  pl.BlockSpec((1, d), lambda i: (0, 0))],
        out_specs=pl.BlockSpec((NORM_TM, d), lambda i: (i, 0)),
        out_shape=jax.ShapeDtypeStruct((m, d), jnp.bfloat16),
        compiler_params=_cparams(1),
        name="rmsnorm",
    )(x, g.reshape(1, d))


def _mm_kernel(*refs, n_lhs, n_rhs, pairs, n_extra, epilogue):
    lhs = refs[:n_lhs]
    rhs = refs[n_lhs:n_lhs + n_rhs]
    extra = refs[n_lhs + n_rhs:n_lhs + n_rhs + n_extra]
    out = refs[n_lhs + n_rhs + n_extra]
    accs = [jnp.dot(lhs[a][...], rhs[b][...], preferred_element_type=jnp.float32)
            for a, b in pairs]
    epilogue(accs, extra, out)


def _matmul(lhs, rhs, pairs, extras, epilogue, out_dtype, *, layer, name, tn=MM_TN, n=None,
            rhs_off=None):
    m = lhs[0].shape[0]
    n = rhs[0].shape[2] if n is None else n
    rhs_off = [0] * len(rhs) if rhs_off is None else rhs_off
    tm = MM_TM
    in_specs = []
    for a in lhs:
        in_specs.append(pl.BlockSpec((tm, a.shape[1]), lambda i, j: (i, 0)))
    for w, off in zip(rhs, rhs_off):
        in_specs.append(pl.BlockSpec((None, w.shape[1], tn),
                                     lambda i, j, off=off: (layer, 0, j + off)))
    for _, bs, im in extras:
        in_specs.append(pl.BlockSpec(bs, im))
    kern = functools.partial(_mm_kernel, n_lhs=len(lhs), n_rhs=len(rhs), pairs=pairs,
                             n_extra=len(extras), epilogue=epilogue)
    return pl.pallas_call(
        kern,
        grid=(m // tm, n // tn),
        in_specs=in_specs,
        out_specs=pl.BlockSpec((tm, tn), lambda i, j: (i, j)),
        out_shape=jax.ShapeDtypeStruct((m, n), out_dtype),
        compiler_params=_cparams(2),
        name=name,
    )(*lhs, *rhs, *[e[0] for e in extras])


def _ep_cast(accs, extra, out):
    out[...] = accs[0].astype(out.dtype)


def _ep_sigmoid(accs, extra, out):
    out[...] = jax.nn.sigmoid(accs[0]).astype(out.dtype)


def _ep_swiglu(accs, extra, out):
    out[...] = (jax.nn.silu(accs[0]) * accs[1]).astype(out.dtype)


def _ep_residual(accs, extra, out, *, scale):
    out[...] = extra[0][...] + scale * accs[0]


def _ep_merge(accs, extra, out):
    ga = extra[0][...].astype(jnp.float32)
    gb = extra[1][...].astype(jnp.float32)
    out[...] = (ga * accs[0] + gb * accs[1]).astype(out.dtype)


def _ep_ple(accs, extra, out):
    out[...] = extra[0][...] + jax.nn.sigmoid(accs[0]) * accs[1]


def _rope_lanes(y, cos, s_up, s_dn, half):
    up = pltpu.roll(y, LANES - half, 1)
    dn = pltpu.roll(y, half, 1)
    return y * cos + up * s_up + dn * s_dn


def _ep_headnorm(accs, extra, out, *, rope_half):
    acc = accs[0]
    gain = extra[0][...]
    groups = [slice(g * LANES, (g + 1) * LANES) for g in range(acc.shape[1] // LANES)]
    normed = []
    for sl in groups:
        x = acc[:, sl]
        ms = jnp.mean(x * x, axis=-1, keepdims=True)
        normed.append(x * lax.rsqrt(ms + NORM_EPS) * gain[:, sl])
    for sl, y in zip(groups, normed):
        if rope_half:
            y = _rope_lanes(y, extra[1][...], extra[2][...], extra[3][...], rope_half)
        out[:, sl] = y.astype(out.dtype)


def _ep_rope(accs, extra, out, *, rope_half):
    acc = accs[0]
    for g in range(acc.shape[1] // LANES):
        sl = slice(g * LANES, (g + 1) * LANES)
        y = _rope_lanes(acc[:, sl], extra[0][...], extra[1][...], extra[2][...], rope_half)
        out[:, sl] = y.astype(out.dtype)


def _small_kernel(zf_ref, zkw_ref, bf_ref, gik_ref, cos_ref, sup_ref, sdn_ref,
                  f_ref, kie_ref, kio_ref, wi_ref, carry_sc, *, idx_scale, ki_lane, wi_lane):
    s = zf_ref.shape[0]
    lane = lax.broadcasted_iota(jnp.int32, (s, LANES), 1)

    @pl.when(pl.program_id(0) == 0)
    def _():
        carry_sc[...] = jnp.zeros(carry_sc.shape, jnp.float32)

    x = jnp.where(lane < N_FOX_HEADS, zf_ref[...], 0.0) + bf_ref[...]
    logf = jnp.minimum(x, 0.0) - jnp.log1p(jnp.exp(-jnp.abs(x)))
    row = lax.broadcasted_iota(jnp.int32, (s, LANES), 0)
    d = 1
    while d < s:
        logf = logf + jnp.where(row >= d, pltpu.roll(logf, d, 0), 0.0)
        d *= 2
    f_ref[...] = logf + carry_sc[0:1, :]
    carry_sc[0:1, :] = f_ref[s - 1:s, :]
    zkw = zkw_ref[...]
    k = jnp.where(lane < IDX_DIM, pltpu.roll(zkw, (LANES - ki_lane) % LANES, 1), 0.0)
    ms = jnp.sum(k * k, axis=-1, keepdims=True) * (1.0 / IDX_DIM)
    k = k * lax.rsqrt(ms + NORM_EPS) * gik_ref[...]
    k = _rope_lanes(k, cos_ref[...], sup_ref[...], sdn_ref[...], IDX_DIM // ROT_FRAC_DIV // 2)
    k = jnp.where(lane < IDX_DIM, k, 0.0)
    kie_ref[...] = k.astype(kie_ref.dtype)
    kio_ref[...] = pltpu.roll(k, IDX_DIM, 1).astype(kio_ref.dtype)
    wi = pltpu.roll(zkw, (LANES - wi_lane) % LANES, 1)
    wi_ref[...] = jnp.where(lane < N_IDX_HEADS, wi, 0.0) * idx_scale


def _small(zf, zkw, bf, gik, cos, sup, sdn, idx_scale, ki_lane, wi_lane):
    s = zf.shape[0]
    tm = MM_TM
    rows = lambda w: pl.BlockSpec((tm, w), lambda i: (i, 0))
    vec = pl.BlockSpec((1, LANES), lambda i: (0, 0))
    return pl.pallas_call(
        functools.partial(_small_kernel, idx_scale=idx_scale, ki_lane=ki_lane, wi_lane=wi_lane),
        grid=(s // tm,),
        in_specs=[rows(LANES), rows(LANES), vec, vec, rows(LANES), rows(LANES), rows(LANES)],
        out_specs=[rows(LANES)] * 4,
        out_shape=[jax.ShapeDtypeStruct((s, LANES), jnp.float32),
                   jax.ShapeDtypeStruct((s, LANES), jnp.bfloat16),
                   jax.ShapeDtypeStruct((s, LANES), jnp.bfloat16),
                   jax.ShapeDtypeStruct((s, LANES), jnp.float32)],
        scratch_shapes=[pltpu.VMEM((8, LANES), jnp.float32)],
        compiler_params=_cparams(1),
        name="small_heads",
    )(zf, zkw, bf, gik, cos, sup, sdn)


def _sortable(x):
    b = pltpu.bitcast(x, jnp.int32)
    return jnp.where(b < 0, b ^ jnp.int32(0x7FFFFFFF), b)


def _index_kernel(qi_ref, kte_ref, kto_ref, wi_ref, keep_ref, bias_ref, keys_sc, wb_sc, *, topk):
    tq, s = bias_ref.shape
    tc = IDX_TC
    i = pl.program_id(0)
    row0 = i * tq
    n_chunks = (row0 + tq + tc - 1) // tc
    n_tiles = tc // LANES

    for j in range(N_IDX_HEADS):
        wb_sc[j] = jnp.broadcast_to(wi_ref[:, j:j + 1], (tq, LANES))

    row_ids = row0 + lax.broadcasted_iota(jnp.int32, (tq, tc), 0)
    col_iota = lax.broadcasted_iota(jnp.int32, (tq, tc), 1)

    def score_chunk(c, carry):
        col0 = pl.multiple_of(c * tc, tc)
        kte = kte_ref[:, pl.ds(col0, tc)]
        kto = kto_ref[:, pl.ds(col0, tc)]
        score = jnp.zeros((tq, tc), jnp.float32)
        for p in range(N_IDX_HEADS // 2):
            qp = qi_ref[:, p * LANES:(p + 1) * LANES]
            re = jnp.maximum(jnp.dot(qp, kte, preferred_element_type=jnp.float32), 0.0)
            ro = jnp.maximum(jnp.dot(qp, kto, preferred_element_type=jnp.float32), 0.0)
            we = jnp.concatenate([wb_sc[2 * p]] * n_tiles, axis=1)
            wo = jnp.concatenate([wb_sc[2 * p + 1]] * n_tiles, axis=1)
            score = score + re * we + ro * wo
        causal = (col0 + col_iota) <= row_ids
        score = jnp.where(score == 0.0, 0.0, score)
        score = jnp.where(causal, score, NEG_BIG)
        keys_sc[:, pl.ds(col0, tc)] = _sortable(score)
        return carry

    lax.fori_loop(0, n_chunks, score_chunk, 0)

    @pl.when(n_chunks % 2 == 1)
    def _():
        col0 = pl.multiple_of(n_chunks * tc, tc)
        keys_sc[:, pl.ds(col0, tc)] = _sortable(jnp.full((tq, tc), NEG_BIG, jnp.float32))

    tcc = 2 * tc

    def search(rows):
        n_rows = rows.stop - rows.start

        def one_bit(b, thr, kept):
            cand = thr + lax.shift_left(jnp.int32(1), 31 - b)

            def count_chunk(c, cnt):
                col0 = pl.multiple_of(c * tcc, tcc)
                k = keys_sc[rows, pl.ds(col0, tcc)]
                for t in range(tcc // LANES):
                    ge = k[:, t * LANES:(t + 1) * LANES] >= cand
                    cnt = cnt + jnp.where(ge, 1.0, 0.0)
                return cnt

            cnt = lax.fori_loop(0, (n_chunks + 1) // 2, count_chunk,
                                jnp.zeros((n_rows, LANES), jnp.float32))
            total = jnp.sum(cnt, axis=1, keepdims=True)
            accept = total >= topk
            return jnp.where(accept, cand, thr), jnp.where(accept, total, kept)

        def bit_step(state):
            b, thr, kept, _ = state
            thr, kept = one_bit(b, thr, kept)
            thr, kept = one_bit(b + 1, thr, kept)
            unresolved = jnp.max(jnp.where(kept == topk, 0.0, 1.0))
            return b + 2, thr, kept, unresolved

        def search_on(state):
            b, _, _, unresolved = state
            return jnp.logical_and(b < 32, unresolved > 0.5)

        _, thr, kept, _ = lax.while_loop(
            search_on, bit_step,
            (jnp.int32(0), jnp.full((n_rows, LANES), _INT_MIN, jnp.int32),
             jnp.full((n_rows, LANES), float(2 * s), jnp.float32), jnp.float32(1.0)))

        def tie_bound():
            def count(pred):
                def chunk(c, cnt):
                    col0 = pl.multiple_of(c * tcc, tcc)
                    k = keys_sc[rows, pl.ds(col0, tcc)]
                    for t in range(tcc // LANES):
                        hit = pred(k[:, t * LANES:(t + 1) * LANES], col0 + t * LANES)
                        cnt = cnt + jnp.where(hit, 1.0, 0.0)
                    return cnt
                cnt = lax.fori_loop(0, (n_chunks + 1) // 2, chunk,
                                    jnp.zeros((n_rows, LANES), jnp.float32))
                return jnp.sum(cnt, axis=1, keepdims=True)

            lane = lax.broadcasted_iota(jnp.int32, (n_rows, LANES), 1)
            need = topk - count(lambda k, col: k > thr)

            def col_bit(j, bound):
                cand = bound + lax.shift_left(jnp.int32(1), COL_BITS - 1 - j)
                below = count(lambda k, col: jnp.logical_and(k == thr, col + lane < cand))
                return jnp.where(below <= need, cand, bound)

            return lax.fori_loop(0, COL_BITS, col_bit, jnp.zeros((n_rows, LANES), jnp.int32))

        ties = jnp.max(jnp.where(kept > topk, 1.0, 0.0)) > 0.5
        bound = lax.cond(ties, tie_bound, lambda: jnp.full((n_rows, LANES), s, jnp.int32))
        return thr, bound

    found = [search(slice(g * IDX_SG, (g + 1) * IDX_SG)) for g in range(tq // IDX_SG)]
    thr = jnp.concatenate([f[0] for f in found], axis=0)
    bound = jnp.concatenate([f[1] for f in found], axis=0)

    def bias_chunk(c, carry):
        col0 = pl.multiple_of(c * tc, tc)
        k = keys_sc[:, pl.ds(col0, tc)]
        cols = col0 + col_iota
        thr_w = jnp.concatenate([thr] * n_tiles, axis=1)
        bound_w = jnp.concatenate([bound] * n_tiles, axis=1)
        keep = jnp.logical_or(k > thr_w, jnp.logical_and(k == thr_w, cols < bound_w))
        keep = jnp.logical_and(keep, cols <= row_ids)
        bias_ref[:, pl.ds(col0, tc)] = jnp.where(keep, keep_ref[...], NEG_BIG).astype(bias_ref.dtype)
        return carry

    lax.fori_loop(0, n_chunks, bias_chunk, 0)

    def fill_chunk(c, carry):
        col0 = pl.multiple_of(c * tc, tc)
        bias_ref[:, pl.ds(col0, tc)] = jnp.full((tq, tc), NEG_BIG, bias_ref.dtype)
        return carry

    lax.fori_loop(n_chunks, s // tc, fill_chunk, 0)


def _index_bias(qi, kte, kto, wi, topk, keep_value):
    s = qi.shape[0]
    tq = IDX_TQ
    return pl.pallas_call(
        functools.partial(_index_kernel, topk=topk),
        grid=(s // tq,),
        in_specs=[pl.BlockSpec((tq, IDX_Q_W), lambda i: (i, 0)),
                  pl.BlockSpec((LANES, s), lambda i: (0, 0)),
                  pl.BlockSpec((LANES, s), lambda i: (0, 0)),
                  pl.BlockSpec((tq, LANES), lambda i: (i, 0)),
                  pl.BlockSpec((1, IDX_TC), lambda i: (0, 0))],
        out_specs=pl.BlockSpec((tq, s), lambda i: (i, 0)),
        out_shape=jax.ShapeDtypeStruct((s, s), jnp.bfloat16),
        scratch_shapes=[pltpu.VMEM((tq, s), jnp.int32),
                        pltpu.VMEM((N_IDX_HEADS, tq, LANES), jnp.float32)],
        compiler_params=_cparams(1),
        name="index_topk_bias",
    )(qi, kte, kto, wi, jnp.full((1, IDX_TC), keep_value, jnp.float32))


def _attn_kernel(qt_ref, kt_ref, skip_ref, q_ref, k_ref, v_ref, *rest, fox):
    if fox:
        fq_ref, fk_ref, o_ref, m_sc, l_sc, acc_sc, s_sc, p_sc, alpha_sc = rest
    else:
        bias_ref, o_ref, m_sc, l_sc, acc_sc, s_sc, p_sc, alpha_sc = rest
    p_id = pl.program_id(1)
    qb = qt_ref[p_id]
    kb = kt_ref[p_id]
    tq, tk = q_ref.shape[0], k_ref.shape[0]
    rc = ATT_RC
    qk_scale = (HEAD_DIM ** -0.5) * LOG2E

    @pl.when(kb == qb)
    def _():
        m_sc[...] = jnp.full(m_sc.shape, NEG_BIG, jnp.float32)
        l_sc[...] = jnp.zeros(l_sc.shape, jnp.float32)
        acc_sc[...] = jnp.zeros(acc_sc.shape, jnp.float32)

    if fox:
        key_bias = (fq_ref[:, 0:1] - fk_ref[...]) * LOG2E
        col_minus_row = (lax.broadcasted_iota(jnp.int32, (rc, tk), 1)
                         - lax.broadcasted_iota(jnp.int32, (rc, tk), 0))

    def key_cols(t, diag):
        return (t + 1) * rc if diag else tk

    def logits_stage(t, diag):
        rows = slice(t * rc, (t + 1) * rc)
        kc = key_cols(t, diag)
        s = lax.dot_general(q_ref[rows, :], k_ref[0:kc, :], (((1,), (1,)), ((), ())),
                            preferred_element_type=jnp.float32) * qk_scale
        if fox:
            s = s + key_bias[:, 0:kc]
            if diag:
                s = jnp.where(col_minus_row[:, 0:kc] <= t * rc, s, NEG_BIG)
        else:
            s = s + bias_ref[rows, 0:kc].astype(jnp.float32)
        s_sc[t % 2, :, 0:kc] = s

    def softmax_stage(t, diag):
        rows = slice(t * rc, (t + 1) * rc)
        kc = key_cols(t, diag)
        s = s_sc[t % 2, :, 0:kc]
        m_prev = m_sc[rows, :]
        m_new = jnp.maximum(m_prev, jnp.max(s, axis=1, keepdims=True))
        alpha = jnp.exp2(m_prev - m_new)
        p = jnp.exp2(s - m_new)
        l_sc[rows, :] = alpha * l_sc[rows, :] + jnp.sum(p, axis=1, keepdims=True)
        m_sc[rows, :] = m_new
        alpha_sc[t % 2] = alpha
        p_sc[t % 2, :, 0:kc] = p.astype(p_sc.dtype)

    def value_stage(t, diag):
        rows = slice(t * rc, (t + 1) * rc)
        kc = key_cols(t, diag)
        acc_sc[rows, :] = alpha_sc[t % 2] * acc_sc[rows, :] + jnp.dot(
            p_sc[t % 2, :, 0:kc], v_ref[0:kc, :], preferred_element_type=jnp.float32)

    def all_chunks(diag):
        n_rc = tq // rc
        for t in range(n_rc + 2):
            if t < n_rc:
                logits_stage(t, diag)
            if 1 <= t <= n_rc:
                softmax_stage(t - 1, diag)
            if t >= 2:
                value_stage(t - 2, diag)

    pl.when(kb == qb)(functools.partial(all_chunks, True))
    live = skip_ref[pl.program_id(0), p_id] == 0
    pl.when(jnp.logical_and(kb != qb, live))(functools.partial(all_chunks, False))

    @pl.when(kb == 0)
    def _():
        o_ref[...] = (acc_sc[...] / l_sc[...]).astype(o_ref.dtype)


def _block_pairs(s):
    nq = s // ATT_TQ
    return [(a, b) for a in range(nq) for b in range(a, -1, -1)]


def _logit_bound2(g_q, g_k):
    return (HEAD_DIM * jnp.max(jnp.abs(g_q)) * jnp.max(jnp.abs(g_k)) * BF16_NORM_SLACK
            * (HEAD_DIM ** -0.5) * LOG2E)


def _fox_skip_flags(f_t, g_q, g_k):
    s = f_t.shape[1]
    bound2 = _logit_bound2(g_q, g_k)
    pairs = _block_pairs(s)
    qrow = jnp.asarray([a * ATT_TQ for a, _ in pairs], jnp.int32)
    kcol = jnp.asarray([b * ATT_TK + ATT_TK - 1 for _, b in pairs], jnp.int32)
    gap2 = (f_t[:, qrow] - f_t[:, kcol]) * LOG2E
    return (gap2 + 2.0 * bound2 < EXP2_ZERO_BELOW).astype(jnp.int32)


def _attention(qk, v, q_col, k_col, v_col, *, frow=None, skip=None, bias=None):
    s = qk.shape[0]
    n_heads = N_FOX_HEADS
    tq, tk = ATT_TQ, ATT_TK
    assert tq == tk
    pairs = _block_pairs(s)
    qt = jnp.asarray([a for a, _ in pairs], jnp.int32)
    kt = jnp.asarray([b for _, b in pairs], jnp.int32)
    fox = bias is None
    if skip is None:
        skip = jnp.zeros((n_heads, len(pairs)), jnp.int32)
    in_specs = [
        pl.BlockSpec((tq, HEAD_DIM), lambda h, p, qt, kt, sk: (qt[p], q_col + h)),
        pl.BlockSpec((tk, HEAD_DIM), lambda h, p, qt, kt, sk: (kt[p], k_col + h)),
        pl.BlockSpec((tk, HEAD_DIM), lambda h, p, qt, kt, sk: (kt[p], v_col + h)),
    ]
    if fox:
        in_specs += [pl.BlockSpec((None, 1, tq), lambda h, p, qt, kt, sk: (h, 0, qt[p])),
                     pl.BlockSpec((None, 1, tk), lambda h, p, qt, kt, sk: (h, 0, kt[p]))]
        operands = (qk, qk, v, frow, frow)
    else:
        in_specs += [pl.BlockSpec((tq, tk), lambda h, p, qt, kt, sk: (qt[p], kt[p]))]
        operands = (qk, qk, v, bias)
    grid_spec = pltpu.PrefetchScalarGridSpec(
        num_scalar_prefetch=3,
        grid=(n_heads, len(pairs)),
        in_specs=in_specs,
        out_specs=pl.BlockSpec((tq, HEAD_DIM), lambda h, p, qt, kt, sk: (qt[p], h)),
        scratch_shapes=[pltpu.VMEM((tq, 1), jnp.float32),
                        pltpu.VMEM((tq, 1), jnp.float32),
                        pltpu.VMEM((tq, HEAD_DIM), jnp.float32),
                        pltpu.VMEM((2, ATT_RC, tk), jnp.float32),
                        pltpu.VMEM((2, ATT_RC, tk), jnp.bfloat16),
                        pltpu.VMEM((2, ATT_RC, 1), jnp.float32)],
    )
    return pl.pallas_call(
        functools.partial(_attn_kernel, fox=fox),
        grid_spec=grid_spec,
        out_shape=jax.ShapeDtypeStruct((s, n_heads * HEAD_DIM), jnp.bfloat16),
        compiler_params=_cparams(2),
        name="fox_attention" if fox else "dsa_attention",
    )(qt, kt, skip, *operands)


def _attn_shift_kernel(qt_ref, kt_ref, skip_ref, q_ref, k_ref, v_ref, *rest, fox):
    if fox:
        fcol_ref, frow_ref, o_ref, acc_sc = rest
    else:
        bias_ref, o_ref, acc_sc = rest
    p_id = pl.program_id(1)
    qb = qt_ref[p_id]
    kb = kt_ref[p_id]
    tq, tk = q_ref.shape[0], k_ref.shape[0]
    rc = ATT_RC
    qk_scale = (HEAD_DIM ** -0.5) * LOG2E

    @pl.when(kb == qb)
    def _():
        acc_sc[...] = jnp.zeros(acc_sc.shape, jnp.float32)

    def chunks(diag):
        v1 = jnp.concatenate([v_ref[...], jnp.ones(v_ref.shape, v_ref.dtype)], axis=1)
        if fox:
            col_minus_row = (lax.broadcasted_iota(jnp.int32, (rc, tk), 1)
                             - lax.broadcasted_iota(jnp.int32, (rc, tk), 0))
        for t in range(tq // rc):
            rows = slice(t * rc, (t + 1) * rc)
            kc = (t + 1) * rc if diag else tk
            s = lax.dot_general(q_ref[rows, :], k_ref[0:kc, :], (((1,), (1,)), ((), ())),
                                preferred_element_type=jnp.float32) * qk_scale
            if fox:
                s = s + fcol_ref[rows, :] - frow_ref[:, 0:kc]
                if diag:
                    s = jnp.where(col_minus_row[:, 0:kc] <= t * rc, s, NEG_BIG)
            else:
                s = s + bias_ref[rows, 0:kc].astype(jnp.float32)
            p = jnp.exp2(s).astype(v_ref.dtype)
            acc_sc[rows, :] += jnp.dot(p, v1[0:kc, :], preferred_element_type=jnp.float32)

    pl.when(kb == qb)(functools.partial(chunks, True))
    live = skip_ref[pl.program_id(0), p_id] == 0
    pl.when(jnp.logical_and(kb != qb, live))(functools.partial(chunks, False))

    @pl.when(kb == 0)
    def _():
        o_ref[...] = (acc_sc[:, 0:HEAD_DIM] / acc_sc[:, HEAD_DIM:]).astype(o_ref.dtype)


def _attention_shift(qk, v, q_col, k_col, v_col, *, fcol=None, frow=None, skip=None, bias=None):
    s = qk.shape[0]
    n_heads = N_FOX_HEADS
    tq, tk = ATT_TQ, ATT_TK
    assert tq == tk
    pairs = _block_pairs(s)
    qt = jnp.asarray([a for a, _ in pairs], jnp.int32)
    kt = jnp.asarray([b for _, b in pairs], jnp.int32)
    fox = bias is None
    if skip is None:
        skip = jnp.zeros((n_heads, len(pairs)), jnp.int32)
    in_specs = [
        pl.BlockSpec((tq, HEAD_DIM), lambda h, p, qt, kt, sk: (qt[p], q_col + h)),
        pl.BlockSpec((tk, HEAD_DIM), lambda h, p, qt, kt, sk: (kt[p], k_col + h)),
        pl.BlockSpec((tk, HEAD_DIM), lambda h, p, qt, kt, sk: (kt[p], v_col + h)),
    ]
    if fox:
        in_specs += [pl.BlockSpec((None, tq, 1), lambda h, p, qt, kt, sk: (h, qt[p], 0)),
                     pl.BlockSpec((None, 1, tk), lambda h, p, qt, kt, sk: (h, 0, kt[p]))]
        operands = (qk, qk, v, fcol, frow)
    else:
        in_specs += [pl.BlockSpec((tq, tk), lambda h, p, qt, kt, sk: (qt[p], kt[p]))]
        operands = (qk, qk, v, bias)
    grid_spec = pltpu.PrefetchScalarGridSpec(
        num_scalar_prefetch=3,
        grid=(n_heads, len(pairs)),
        in_specs=in_specs,
        out_specs=pl.BlockSpec((tq, HEAD_DIM), lambda h, p, qt, kt, sk: (qt[p], h)),
        scratch_shapes=[pltpu.VMEM((tq, 2 * HEAD_DIM), jnp.float32)],
    )
    return pl.pallas_call(
        functools.partial(_attn_shift_kernel, fox=fox),
        grid_spec=grid_spec,
        out_shape=jax.ShapeDtypeStruct((s, n_heads * HEAD_DIM), jnp.bfloat16),
        compiler_params=_cparams(2),
        name="fox_attention_shift" if fox else "dsa_attention_shift",
    )(qt, kt, skip, *operands)


def _cast_kernel(x_ref, o_ref):
    o_ref[...] = x_ref[...].astype(o_ref.dtype)


def _cast_bf16(w):
    l, k, n = w.shape
    rows = math.gcd(CAST_ROWS, k)
    spec = pl.BlockSpec((None, rows, n), lambda i, j: (i, j, 0))
    return pl.pallas_call(
        _cast_kernel,
        grid=(l, k // rows),
        in_specs=[spec],
        out_specs=spec,
        out_shape=jax.ShapeDtypeStruct((l, k, n), jnp.bfloat16),
        compiler_params=_cparams(2),
        name="cast_bf16",
    )(w)


def _rope_angles(s, rot):
    half = rot // 2
    pos = jnp.arange(s, dtype=jnp.float32)
    inv = jnp.power(jnp.float32(ROPE_THETA), -2.0 * jnp.arange(half, dtype=jnp.float32) / rot)
    ang = pos[:, None] * inv[None, :]
    return jnp.cos(ang), jnp.sin(ang)


def _rope_tables(s, rot, period):
    half = rot // 2
    cos, sin = _rope_angles(s, rot)
    ones = jnp.ones((s, period - rot), jnp.float32)
    zeros = lambda n: jnp.zeros((s, n), jnp.float32)
    c = jnp.concatenate([cos, cos, ones], axis=1)
    s_up = jnp.concatenate([-sin, zeros(period - half)], axis=1)
    s_dn = jnp.concatenate([zeros(half), sin, zeros(period - rot)], axis=1)
    reps = LANES // period
    return tuple(jnp.tile(t, (1, reps)) for t in (c, s_up, s_dn))


def _pad_row(v, n):
    return jnp.pad(v, (0, n - v.shape[0])).reshape(1, n)


def kernel(x, p, g_ffn1, w1_gate, w1_up, w1_down, g_mix, w_in, b_f, g_qa, g_ka, g_qb, g_kb, g_ik,
           w_br_fox, w_br_dsa, w_o, g_ffn2, w2_gate, w2_up, w2_down, g_ple, w_ple_gate, w_ple_proj):
    b, s, d = x.shape
    assert b == 1
    bf16 = jnp.bfloat16
    topk = min(INDEX_TOPK_MAX, s // 4)
    idx_scale = (N_IDX_HEADS ** -0.5) * (IDX_DIM ** -0.5)
    rope_b = _rope_tables(s, HEAD_DIM // ROT_FRAC_DIV, HEAD_DIM)
    half_b = HEAD_DIM // ROT_FRAC_DIV // 2
    rope_i = _rope_tables(s, IDX_DIM // ROT_FRAC_DIV, IDX_DIM)
    half_i = IDX_DIM // ROT_FRAC_DIV // 2
    row_tab = lambda t: (t, (MM_TM, LANES), lambda i, j: (i, 0))
    res_tile = lambda h, tn=MM_TN: (h, (MM_TM, tn), lambda i, j: (i, j))
    blk = lambda cols: cols // MM_TN_WIDE
    tn_res = min(MM_TN_WIDE, d)

    w1g, w1u, w1d = _cast_bf16(w1_gate), _cast_bf16(w1_up), _cast_bf16(w1_down)
    w2g, w2u, w2d = _cast_bf16(w2_gate), _cast_bf16(w2_up), _cast_bf16(w2_down)
    wbf, wbd, wo = _cast_bf16(w_br_fox), _cast_bf16(w_br_dsa), _cast_bf16(w_o)
    wpg, wpp = _cast_bf16(w_ple_gate), _cast_bf16(w_ple_proj)
    off_fa = 3 * FOX_W
    off_b = off_fa + N_FOX_HEADS
    off_ki = off_b + 3 * DSA_W + IDX_Q_W
    off_wi = off_ki + IDX_DIM
    off_g = off_wi + N_IDX_HEADS
    w_in_bf = w_in.astype(bf16)
    w_a = w_in_bf
    w_b = w_in_bf[:, :, off_b:off_ki]
    w_g = w_in_bf[:, :, off_g:]
    assert off_fa % LANES == 0 and off_g <= (off_ki // LANES + 1) * LANES
    kw_window = off_ki // LANES
    ki_lane, wi_lane = off_ki - kw_window * LANES, off_wi - kw_window * LANES
    p_bf = p.reshape(DEPTH, s, PLE_DIM).astype(bf16)

    def ffn(h, g, wg, wu, wd, layer, tag):
        u = _rmsnorm(h, g)
        a = _matmul([u], [wg, wu], [(0, 0), (0, 1)], [], _ep_swiglu, bf16,
                    layer=layer, name=tag + "_swiglu")
        return _matmul([a], [wd], [(0, 0)], [res_tile(h, tn_res)],
                       functools.partial(_ep_residual, scale=0.5), jnp.float32,
                       layer=layer, name=tag + "_down", tn=tn_res)

    h = x.reshape(s, d)
    for i in range(DEPTH):
        h = ffn(h, g_ffn1[i], w1g, w1u, w1d, i, "ffn1")

        u = _rmsnorm(h, g_mix[i])
        gain_a = jnp.concatenate([jnp.tile(g_qa[i], N_FOX_HEADS), jnp.tile(g_ka[i], N_FOX_HEADS)])
        gain_b = jnp.concatenate([jnp.tile(g_qb[i], N_DSA_HEADS), jnp.tile(g_kb[i], N_DSA_HEADS)])
        col_vec = lambda v: (v.reshape(1, -1), (1, MM_TN_WIDE), lambda i, j: (0, j))

        qk_a = _matmul([u], [w_a], [(0, 0)], [col_vec(gain_a)],
                       functools.partial(_ep_headnorm, rope_half=0), bf16,
                       layer=i, name="proj_qk_fox", tn=MM_TN_WIDE, n=2 * FOX_W)
        v_a = _matmul([u], [w_a], [(0, 0)], [], _ep_cast, bf16,
                      layer=i, name="proj_v_fox", tn=MM_TN_WIDE, n=FOX_W, rhs_off=[blk(2 * FOX_W)])
        qk_b = _matmul([u], [w_b], [(0, 0)], [col_vec(gain_b)] + [row_tab(t) for t in rope_b],
                       functools.partial(_ep_headnorm, rope_half=half_b), bf16,
                       layer=i, name="proj_qk_dsa", tn=MM_TN_WIDE, n=2 * DSA_W)
        v_b = _matmul([u], [w_b], [(0, 0)], [], _ep_cast, bf16,
                      layer=i, name="proj_v_dsa", tn=MM_TN_WIDE, n=DSA_W, rhs_off=[blk(2 * DSA_W)])
        q_i = _matmul([u], [w_b], [(0, 0)], [row_tab(t) for t in rope_i],
                      functools.partial(_ep_rope, rope_half=half_i), bf16,
                      layer=i, name="proj_qidx", tn=MM_TN_WIDE, n=IDX_Q_W, rhs_off=[blk(3 * DSA_W)])
        gates = _matmul([u], [w_g], [(0, 0)], [], _ep_sigmoid, bf16, layer=i, name="proj_gates", tn=MM_TN_WIDE)
        z_f = _matmul([u], [w_in_bf], [(0, 0)], [], _ep_cast, jnp.float32, layer=i,
                      name="proj_forget", tn=LANES, n=LANES, rhs_off=[off_fa // LANES])
        z_kw = _matmul([u], [w_in_bf], [(0, 0)], [], _ep_cast, jnp.float32, layer=i,
                       name="proj_idx_key", tn=LANES, n=LANES, rhs_off=[kw_window])

        f_cum, ki_e, ki_o, wi_s = _small(z_f, z_kw, _pad_row(b_f[i], LANES),
                                         _pad_row(g_ik[i], LANES), *rope_i, idx_scale,
                                         ki_lane, wi_lane)

        f_t = f_cum[:, :N_FOX_HEADS].T
        skip = _fox_skip_flags(f_t, g_qa[i], g_ka[i])
        shift_a = _logit_bound2(g_qa[i], g_ka[i])
        o_a = lax.cond(
            2.0 * shift_a < FIXED_SHIFT_MAX_RANGE,
            lambda: _attention_shift(qk_a, v_a, 0, N_FOX_HEADS, 0,
                                     fcol=(f_t * LOG2E - shift_a)[:, :, None],
                                     frow=(f_t * LOG2E)[:, None, :], skip=skip),
            lambda: _attention(qk_a, v_a, 0, N_FOX_HEADS, 0, frow=f_t[:, None, :], skip=skip))

        shift_b = _logit_bound2(g_qb[i], g_kb[i])
        fixed_b = 2.0 * shift_b < FIXED_SHIFT_MAX_RANGE
        bias = _index_bias(q_i, ki_e.T, ki_o.T, wi_s, topk, jnp.where(fixed_b, -shift_b, 0.0))
        o_b = lax.cond(
            fixed_b,
            lambda: _attention_shift(qk_b, v_b, 0, N_DSA_HEADS, 0, bias=bias),
            lambda: _attention(qk_b, v_b, 0, N_DSA_HEADS, 0, bias=bias))

        n_gate_blocks = D_MODEL // tn_res
        y = _matmul([o_a, o_b], [wbf, wbd], [(0, 0), (1, 1)],
                    [(gates, (MM_TM, tn_res), lambda i, j: (i, j)),
                     (gates, (MM_TM, tn_res), lambda i, j: (i, j + n_gate_blocks))],
                    _ep_merge, bf16, layer=i, name="branch_merge", tn=tn_res)
        h = _matmul([y], [wo], [(0, 0)], [res_tile(h, tn_res)],
                    functools.partial(_ep_residual, scale=1.0), jnp.float32,
                    layer=i, name="out_proj", tn=tn_res)

        h = ffn(h, g_ffn2[i], w2g, w2u, w2d, i, "ffn2")

        u = _rmsnorm(h, g_ple[i])
        h = _matmul([u, p_bf[i]], [wpg, wpp], [(0, 0), (1, 1)], [res_tile(h)],
                    _ep_ple, jnp.float32, layer=i, name="ple")
    return h.reshape(b, s, d)
```

```python
import functools
import math

import jax
import jax.numpy as jnp
from jax import lax
from jax.experimental import pallas as pl
from jax.experimental.pallas import tpu as pltpu

D_MODEL = 4096
SEQ = 8192
DEPTH = 4
HEAD_DIM = 128
N_FOX_HEADS = 16
N_DSA_HEADS = 16
N_IDX_HEADS = 32
IDX_DIM = 64
INDEX_TOPK_MAX = 256
D_FF = 3584
PLE_DIM = 256
ROPE_THETA = 500000.0
ROT_FRAC_DIV = 4
NORM_EPS = 1e-6
NEG_BIG = -1e30
LOG2E = 1.4426950408889634
EXP2_ZERO_BELOW = -160.0
FIXED_SHIFT_MAX_RANGE = 100.0
BF16_NORM_SLACK = 1.02
FOX_W = N_FOX_HEADS * HEAD_DIM
DSA_W = N_DSA_HEADS * HEAD_DIM
IDX_Q_W = N_IDX_HEADS * IDX_DIM

LANES = 128
VMEM_LIMIT = 56 * 1024 * 1024

MM_TM = 1024
MM_TN = 512
MM_TN_WIDE = 1024
NORM_TM = 512
CAST_ROWS = 512
ATT_T = 1024
ATT_T_DSA = 2048
ATT_RC = 256
IDX_TQ = 256
IDX_TC = 512
IDX_SG = 128

_INT_MIN = -(2 ** 31)
COL_BITS = 14


def _cparams(n_axes):
    return pltpu.CompilerParams(
        dimension_semantics=("arbitrary",) * n_axes,
        vmem_limit_bytes=VMEM_LIMIT,
    )


def _rmsnorm_kernel(x_ref, g_ref, o_ref):
    x = x_ref[...]
    ms = jnp.mean(x * x, axis=-1, keepdims=True)
    o_ref[...] = (x * lax.rsqrt(ms + NORM_EPS) * g_ref[...]).astype(o_ref.dtype)


def _rmsnorm(x, g):
    m, d = x.shape
    return pl.pallas_call(
        _rmsnorm_kernel,
        grid=(m // NORM_TM,),
        in_specs=[pl.BlockSpec((NORM_TM, d), lambda i: (i, 0)),
                  pl.BlockSpec((1, d), lambda i: (0, 0))],
        out_specs=pl.BlockSpec((NORM_TM, d), lambda i: (i, 0)),
        out_shape=jax.ShapeDtypeStruct((m, d), jnp.bfloat16),
        compiler_params=_cparams(1),
        name="rmsnorm",
    )(x, g.reshape(1, d))


def _mm_kernel(*refs, n_lhs, n_rhs, pairs, n_extra, epilogue):
    lhs = refs[:n_lhs]
    rhs = refs[n_lhs:n_lhs + n_rhs]
    extra = refs[n_lhs + n_rhs:n_lhs + n_rhs + n_extra]
    out = refs[n_lhs + n_rhs + n_extra]
    accs = [jnp.dot(lhs[a][...], rhs[b][...], preferred_element_type=jnp.float32)
            for a, b in pairs]
    epilogue(accs, extra, out)


def _matmul(lhs, rhs, pairs, extras, epilogue, out_dtype, *, layer, name, tn=MM_TN, n=None,
            rhs_off=None):
    m = lhs[0].shape[0]
    n = rhs[0].shape[2] if n is None else n
    rhs_off = [0] * len(rhs) if rhs_off is None else rhs_off
    tm = MM_TM
    in_specs = []
    for a in lhs:
        in_specs.append(pl.BlockSpec((tm, a.shape[1]), lambda i, j: (i, 0)))
    for w, off in zip(rhs, rhs_off):
        in_specs.append(pl.BlockSpec((None, w.shape[1], tn),
                                     lambda i, j, off=off: (layer, 0, j + off)))
    for _, bs, im in extras:
        in_specs.append(pl.BlockSpec(bs, im))
    kern = functools.partial(_mm_kernel, n_lhs=len(lhs), n_rhs=len(rhs), pairs=pairs,
                             n_extra=len(extras), epilogue=epilogue)
    return pl.pallas_call(
        kern,
        grid=(m // tm, n // tn),
        in_specs=in_specs,
        out_specs=pl.BlockSpec((tm, tn), lambda i, j: (i, j)),
        out_shape=jax.ShapeDtypeStruct((m, n), out_dtype),
        compiler_params=_cparams(2),
        name=name,
    )(*lhs, *rhs, *[e[0] for e in extras])


def _ep_cast(accs, extra, out):
    out[...] = accs[0].astype(out.dtype)


def _ep_sigmoid(accs, extra, out):
    out[...] = jax.nn.sigmoid(accs[0]).astype(out.dtype)


def _ep_swiglu(accs, extra, out):
    out[...] = (jax.nn.silu(accs[0]) * accs[1]).astype(out.dtype)


def _ep_residual(accs, extra, out, *, scale):
    out[...] = extra[0][...] + scale * accs[0]


def _ep_merge(accs, extra, out):
    ga = extra[0][...].astype(jnp.float32)
    gb = extra[1][...].astype(jnp.float32)
    out[...] = (ga * accs[0] + gb * accs[1]).astype(out.dtype)


def _ep_ple(accs, extra, out):
    out[...] = extra[0][...] + jax.nn.sigmoid(accs[0]) * accs[1]


def _rope_lanes(y, cos, s_up, s_dn, half):
    up = pltpu.roll(y, LANES - half, 1)
    dn = pltpu.roll(y, half, 1)
    return y * cos + up * s_up + dn * s_dn


def _ep_headnorm(accs, extra, out, *, rope_half):
    acc = accs[0]
    gain = extra[0][...]
    groups = [slice(g * LANES, (g + 1) * LANES) for g in range(acc.shape[1] // LANES)]
    normed = []
    for sl in groups:
        x = acc[:, sl]
        ms = jnp.mean(x * x, axis=-1, keepdims=True)
        normed.append(x * lax.rsqrt(ms + NORM_EPS) * gain[:, sl])
    for sl, y in zip(groups, normed):
        if rope_half:
            y = _rope_lanes(y, extra[1][...], extra[2][...], extra[3][...], rope_half)
        out[:, sl] = y.astype(out.dtype)


def _ep_rope(accs, extra, out, *, rope_half):
    acc = accs[0]
    for g in range(acc.shape[1] // LANES):
        sl = slice(g * LANES, (g + 1) * LANES)
        y = _rope_lanes(acc[:, sl], extra[0][...], extra[1][...], extra[2][...], rope_half)
        out[:, sl] = y.astype(out.dtype)


def _small_kernel(zf_ref, zkw_ref, bf_ref, gik_ref, cos_ref, sup_ref, sdn_ref,
                  f_ref, kie_ref, kio_ref, wi_ref, carry_sc, *, idx_scale, ki_lane, wi_lane):
    s = zf_ref.shape[0]
    lane = lax.broadcasted_iota(jnp.int32, (s, LANES), 1)

    @pl.when(pl.program_id(0) == 0)
    def _():
        carry_sc[...] = jnp.zeros(carry_sc.shape, jnp.float32)

    x = jnp.where(lane < N_FOX_HEADS, zf_ref[...], 0.0) + bf_ref[...]
    logf = jnp.minimum(x, 0.0) - jnp.log1p(jnp.exp(-jnp.abs(x)))
    row = lax.broadcasted_iota(jnp.int32, (s, LANES), 0)
    d = 1
    while d < s:
        logf = logf + jnp.where(row >= d, pltpu.roll(logf, d, 0), 0.0)
        d *= 2
    f_ref[...] = logf + carry_sc[0:1, :]
    carry_sc[0:1, :] = f_ref[s - 1:s, :]
    zkw = zkw_ref[...]
    k = jnp.where(lane < IDX_DIM, pltpu.roll(zkw, (LANES - ki_lane) % LANES, 1), 0.0)
    ms = jnp.sum(k * k, axis=-1, keepdims=True) * (1.0 / IDX_DIM)
    k = k * lax.rsqrt(ms + NORM_EPS) * gik_ref[...]
    k = _rope_lanes(k, cos_ref[...], sup_ref[...], sdn_ref[...], IDX_DIM // ROT_FRAC_DIV // 2)
    k = jnp.where(lane < IDX_DIM, k, 0.0)
    kie_ref[...] = k.astype(kie_ref.dtype)
    kio_ref[...] = pltpu.roll(k, IDX_DIM, 1).astype(kio_ref.dtype)
    wi = pltpu.roll(zkw, (LANES - wi_lane) % LANES, 1)
    wi_ref[...] = jnp.where(lane < N_IDX_HEADS, wi, 0.0) * idx_scale


def _small(zf, zkw, bf, gik, cos, sup, sdn, idx_scale, ki_lane, wi_lane):
    s = zf.shape[0]
    tm = MM_TM
    rows = lambda w: pl.BlockSpec((tm, w), lambda i: (i, 0))
    vec = pl.BlockSpec((1, LANES), lambda i: (0, 0))
    return pl.pallas_call(
        functools.partial(_small_kernel, idx_scale=idx_scale, ki_lane=ki_lane, wi_lane=wi_lane),
        grid=(s // tm,),
        in_specs=[rows(LANES), rows(LANES), vec, vec, rows(LANES), rows(LANES), rows(LANES)],
        out_specs=[rows(LANES)] * 4,
        out_shape=[jax.ShapeDtypeStruct((s, LANES), jnp.float32),
                   jax.ShapeDtypeStruct((s, LANES), jnp.bfloat16),
                   jax.ShapeDtypeStruct((s, LANES), jnp.bfloat16),
                   jax.ShapeDtypeStruct((s, LANES), jnp.float32)],
        scratch_shapes=[pltpu.VMEM((8, LANES), jnp.float32)],
        compiler_params=_cparams(1),
        name="small_heads",
    )(zf, zkw, bf, gik, cos, sup, sdn)


def _sortable(x):
    b = pltpu.bitcast(x, jnp.int32)
    return jnp.where(b < 0, b ^ jnp.int32(0x7FFFFFFF), b)


def _index_kernel(qi_ref, kte_ref, kto_ref, wi_ref, keep_ref, bias_ref, keys_sc, wb_sc, *, topk):
    tq, s = bias_ref.shape
    tc = IDX_TC
    i = pl.program_id(0)
    row0 = i * tq
    n_chunks = (row0 + tq + tc - 1) // tc
    n_tiles = tc // LANES

    for j in range(N_IDX_HEADS):
        wb_sc[j] = jnp.broadcast_to(wi_ref[:, j:j + 1], (tq, LANES))

    row_ids = row0 + lax.broadcasted_iota(jnp.int32, (tq, tc), 0)
    col_iota = lax.broadcasted_iota(jnp.int32, (tq, tc), 1)

    def score_chunk(c, carry):
        col0 = pl.multiple_of(c * tc, tc)
        kte = kte_ref[:, pl.ds(col0, tc)]
        kto = kto_ref[:, pl.ds(col0, tc)]
        score = jnp.zeros((tq, tc), jnp.float32)
        for p in range(N_IDX_HEADS // 2):
            qp = qi_ref[:, p * LANES:(p + 1) * LANES]
            re = jnp.maximum(jnp.dot(qp, kte, preferred_element_type=jnp.float32), 0.0)
            ro = jnp.maximum(jnp.dot(qp, kto, preferred_element_type=jnp.float32), 0.0)
            we = jnp.concatenate([wb_sc[2 * p]] * n_tiles, axis=1)
            wo = jnp.concatenate([wb_sc[2 * p + 1]] * n_tiles, axis=1)
            score = score + re * we + ro * wo
        causal = (col0 + col_iota) <= row_ids
        score = jnp.where(score == 0.0, 0.0, score)
        score = jnp.where(causal, score, NEG_BIG)
        keys_sc[:, pl.ds(col0, tc)] = _sortable(score)
        return carry

    lax.fori_loop(0, n_chunks, score_chunk, 0)

    @pl.when(n_chunks % 2 == 1)
    def _():
        col0 = pl.multiple_of(n_chunks * tc, tc)
        keys_sc[:, pl.ds(col0, tc)] = _sortable(jnp.full((tq, tc), NEG_BIG, jnp.float32))

    tcc = 2 * tc

    def search(rows):
        n_rows = rows.stop - rows.start

        def one_bit(b, thr, kept):
            cand = thr + lax.shift_left(jnp.int32(1), 31 - b)

            def count_chunk(c, cnt):
                col0 = pl.multiple_of(c * tcc, tcc)
                k = keys_sc[rows, pl.ds(col0, tcc)]
                for t in range(tcc // LANES):
                    ge = k[:, t * LANES:(t + 1) * LANES] >= cand
                    cnt = cnt + jnp.where(ge, 1.0, 0.0)
                return cnt

            cnt = lax.fori_loop(0, (n_chunks + 1) // 2, count_chunk,
                                jnp.zeros((n_rows, LANES), jnp.float32))
            total = jnp.sum(cnt, axis=1, keepdims=True)
            accept = total >= topk
            return jnp.where(accept, cand, thr), jnp.where(accept, total, kept)

        def bit_step(state):
            b, thr, kept, _ = state
            thr, kept = one_bit(b, thr, kept)
            thr, kept = one_bit(b + 1, thr, kept)
            unresolved = jnp.max(jnp.where(kept == topk, 0.0, 1.0))
            return b + 2, thr, kept, unresolved

        def search_on(state):
            b, _, _, unresolved = state
            return jnp.logical_and(b < 32, unresolved > 0.5)

        _, thr, kept, _ = lax.while_loop(
            search_on, bit_step,
            (jnp.int32(0), jnp.full((n_rows, LANES), _INT_MIN, jnp.int32),
             jnp.full((n_rows, LANES), float(2 * s), jnp.float32), jnp.float32(1.0)))

        def tie_bound():
            def count(pred):
                def chunk(c, cnt):
                    col0 = pl.multiple_of(c * tcc, tcc)
                    k = keys_sc[rows, pl.ds(col0, tcc)]
                    for t in range(tcc // LANES):
                        hit = pred(k[:, t * LANES:(t + 1) * LANES], col0 + t * LANES)
                        cnt = cnt + jnp.where(hit, 1.0, 0.0)
                    return cnt
                cnt = lax.fori_loop(0, (n_chunks + 1) // 2, chunk,
                                    jnp.zeros((n_rows, LANES), jnp.float32))
                return jnp.sum(cnt, axis=1, keepdims=True)

            lane = lax.broadcasted_iota(jnp.int32, (n_rows, LANES), 1)
            need = topk - count(lambda k, col: k > thr)

            def col_bit(j, bound):
                cand = bound + lax.shift_left(jnp.int32(1), COL_BITS - 1 - j)
                below = count(lambda k, col: jnp.logical_and(k == thr, col + lane < cand))
                return jnp.where(below <= need, cand, bound)

            return lax.fori_loop(0, COL_BITS, col_bit, jnp.zeros((n_rows, LANES), jnp.int32))

        ties = jnp.max(jnp.where(kept > topk, 1.0, 0.0)) > 0.5
        bound = lax.cond(ties, tie_bound, lambda: jnp.full((n_rows, LANES), s, jnp.int32))
        return thr, bound

    found = [search(slice(g * IDX_SG, (g + 1) * IDX_SG)) for g in range(tq // IDX_SG)]
    thr = jnp.concatenate([f[0] for f in found], axis=0)
    bound = jnp.concatenate([f[1] for f in found], axis=0)

    def bias_chunk(c, carry):
        col0 = pl.multiple_of(c * tc, tc)
        k = keys_sc[:, pl.ds(col0, tc)]
        cols = col0 + col_iota
        thr_w = jnp.concatenate([thr] * n_tiles, axis=1)
        bound_w = jnp.concatenate([bound] * n_tiles, axis=1)
        keep = jnp.logical_or(k > thr_w, jnp.logical_and(k == thr_w, cols < bound_w))
        keep = jnp.logical_and(keep, cols <= row_ids)
        bias_ref[:, pl.ds(col0, tc)] = jnp.where(keep, keep_ref[...], NEG_BIG).astype(bias_ref.dtype)
        return carry

    lax.fori_loop(0, n_chunks, bias_chunk, 0)

    def fill_chunk(c, carry):
        col0 = pl.multiple_of(c * tc, tc)
        bias_ref[:, pl.ds(col0, tc)] = jnp.full((tq, tc), NEG_BIG, bias_ref.dtype)
        return carry

    lax.fori_loop(n_chunks, s // tc, fill_chunk, 0)


def _index_bias(qi, kte, kto, wi, topk, keep_value):
    s = qi.shape[0]
    tq = IDX_TQ
    return pl.pallas_call(
        functools.partial(_index_kernel, topk=topk),
        grid=(s // tq,),
        in_specs=[pl.BlockSpec((tq, IDX_Q_W), lambda i: (i, 0)),
                  pl.BlockSpec((LANES, s), lambda i: (0, 0)),
                  pl.BlockSpec((LANES, s), lambda i: (0, 0)),
                  pl.BlockSpec((tq, LANES), lambda i: (i, 0)),
                  pl.BlockSpec((1, IDX_TC), lambda i: (0, 0))],
        out_specs=pl.BlockSpec((tq, s), lambda i: (i, 0)),
        out_shape=jax.ShapeDtypeStruct((s, s), jnp.bfloat16),
        scratch_shapes=[pltpu.VMEM((tq, s), jnp.int32),
                        pltpu.VMEM((N_IDX_HEADS, tq, LANES), jnp.float32)],
        compiler_params=_cparams(1),
        name="index_topk_bias",
    )(qi, kte, kto, wi, jnp.full((1, IDX_TC), keep_value, jnp.float32))


def _attn_kernel(qt_ref, kt_ref, skip_ref, q_ref, k_ref, v_ref, *rest, fox):
    if fox:
        fq_ref, fk_ref, o_ref, m_sc, l_sc, acc_sc, s_sc, p_sc, alpha_sc = rest
    else:
        bias_ref, o_ref, m_sc, l_sc, acc_sc, s_sc, p_sc, alpha_sc = rest
    p_id = pl.program_id(1)
    qb = qt_ref[p_id]
    kb = kt_ref[p_id]
    tq, tk = q_ref.shape[0], k_ref.shape[0]
    rc = ATT_RC
    qk_scale = (HEAD_DIM ** -0.5) * LOG2E

    @pl.when(kb == qb)
    def _():
        m_sc[...] = jnp.full(m_sc.shape, NEG_BIG, jnp.float32)
        l_sc[...] = jnp.zeros(l_sc.shape, jnp.float32)
        acc_sc[...] = jnp.zeros(acc_sc.shape, jnp.float32)

    if fox:
        key_bias = (fq_ref[:, 0:1] - fk_ref[...]) * LOG2E
        col_minus_row = (lax.broadcasted_iota(jnp.int32, (rc, tk), 1)
                         - lax.broadcasted_iota(jnp.int32, (rc, tk), 0))

    def key_cols(t, diag):
        return (t + 1) * rc if diag else tk

    def logits_stage(t, diag):
        rows = slice(t * rc, (t + 1) * rc)
        kc = key_cols(t, diag)
        s = lax.dot_general(q_ref[rows, :], k_ref[0:kc, :], (((1,), (1,)), ((), ())),
                            preferred_element_type=jnp.float32) * qk_scale
        if fox:
            s = s + key_bias[:, 0:kc]
            if diag:
                s = jnp.where(col_minus_row[:, 0:kc] <= t * rc, s, NEG_BIG)
        else:
            s = s + bias_ref[rows, 0:kc].astype(jnp.float32)
        s_sc[t % 2, :, 0:kc] = s

    def softmax_stage(t, diag):
        rows = slice(t * rc, (t + 1) * rc)
        kc = key_cols(t, diag)
        s = s_sc[t % 2, :, 0:kc]
        m_prev = m_sc[rows, :]
        m_new = jnp.maximum(m_prev, jnp.max(s, axis=1, keepdims=True))
        alpha = jnp.exp2(m_prev - m_new)
        p = jnp.exp2(s - m_new)
        l_sc[rows, :] = alpha * l_sc[rows, :] + jnp.sum(p, axis=1, keepdims=True)
        m_sc[rows, :] = m_new
        alpha_sc[t % 2] = alpha
        p_sc[t % 2, :, 0:kc] = p.astype(p_sc.dtype)

    def value_stage(t, diag):
        rows = slice(t * rc, (t + 1) * rc)
        kc = key_cols(t, diag)
        acc_sc[rows, :] = alpha_sc[t % 2] * acc_sc[rows, :] + jnp.dot(
            p_sc[t % 2, :, 0:kc], v_ref[0:kc, :], preferred_element_type=jnp.float32)

    def all_chunks(diag):
        n_rc = tq // rc
        for t in range(n_rc + 2):
            if t < n_rc:
                logits_stage(t, diag)
            if 1 <= t <= n_rc:
                softmax_stage(t - 1, diag)
            if t >= 2:
                value_stage(t - 2, diag)

    pl.when(kb == qb)(functools.partial(all_chunks, True))
    live = skip_ref[pl.program_id(0), p_id] == 0
    pl.when(jnp.logical_and(kb != qb, live))(functools.partial(all_chunks, False))

    @pl.when(kb == 0)
    def _():
        o_ref[...] = (acc_sc[...] / l_sc[...]).astype(o_ref.dtype)


def _block_pairs(s, t):
    nq = s // t
    return [(a, b) for a in range(nq) for b in range(a, -1, -1)]


def _logit_bound2(g_q, g_k):
    return (HEAD_DIM * jnp.max(jnp.abs(g_q)) * jnp.max(jnp.abs(g_k)) * BF16_NORM_SLACK
            * (HEAD_DIM ** -0.5) * LOG2E)


def _fox_skip_flags(f_t, g_q, g_k):
    s = f_t.shape[1]
    bound2 = _logit_bound2(g_q, g_k)
    pairs = _block_pairs(s, ATT_T)
    qrow = jnp.asarray([a * ATT_T for a, _ in pairs], jnp.int32)
    kcol = jnp.asarray([b * ATT_T + ATT_T - 1 for _, b in pairs], jnp.int32)
    gap2 = (f_t[:, qrow] - f_t[:, kcol]) * LOG2E
    return (gap2 + 2.0 * bound2 < EXP2_ZERO_BELOW).astype(jnp.int32)


def _attention(qk, v, q_col, k_col, v_col, *, frow=None, skip=None, bias=None):
    s = qk.shape[0]
    n_heads = N_FOX_HEADS
    fox = bias is None
    tq = tk = min(ATT_T if fox else ATT_T_DSA, s)
    pairs = _block_pairs(s, tq)
    qt = jnp.asarray([a for a, _ in pairs], jnp.int32)
    kt = jnp.asarray([b for _, b in pairs], jnp.int32)
    fox = bias is None
    if skip is None:
        skip = jnp.zeros((n_heads, len(pairs)), jnp.int32)
    in_specs = [
        pl.BlockSpec((tq, HEAD_DIM), lambda h, p, qt, kt, sk: (qt[p], q_col + h)),
        pl.BlockSpec((tk, HEAD_DIM), lambda h, p, qt, kt, sk: (kt[p], k_col + h)),
        pl.BlockSpec((tk, HEAD_DIM), lambda h, p, qt, kt, sk: (kt[p], v_col + h)),
    ]
    if fox:
        in_specs += [pl.BlockSpec((None, 1, tq), lambda h, p, qt, kt, sk: (h, 0, qt[p])),
                     pl.BlockSpec((None, 1, tk), lambda h, p, qt, kt, sk: (h, 0, kt[p]))]
        operands = (qk, qk, v, frow, frow)
    else:
        in_specs += [pl.BlockSpec((tq, tk), lambda h, p, qt, kt, sk: (qt[p], kt[p]))]
        operands = (qk, qk, v, bias)
    grid_spec = pltpu.PrefetchScalarGridSpec(
        num_scalar_prefetch=3,
        grid=(n_heads, len(pairs)),
        in_specs=in_specs,
        out_specs=pl.BlockSpec((tq, HEAD_DIM), lambda h, p, qt, kt, sk: (qt[p], h)),
        scratch_shapes=[pltpu.VMEM((tq, 1), jnp.float32),
                        pltpu.VMEM((tq, 1), jnp.float32),
                        pltpu.VMEM((tq, HEAD_DIM), jnp.float32),
                        pltpu.VMEM((2, ATT_RC, tk), jnp.float32),
                        pltpu.VMEM((2, ATT_RC, tk), jnp.bfloat16),
                        pltpu.VMEM((2, ATT_RC, 1), jnp.float32)],
    )
    return pl.pallas_call(
        functools.partial(_attn_kernel, fox=fox),
        grid_spec=grid_spec,
        out_shape=jax.ShapeDtypeStruct((s, n_heads * HEAD_DIM), jnp.bfloat16),
        compiler_params=_cparams(2),
        name="fox_attention" if fox else "dsa_attention",
    )(qt, kt, skip, *operands)


def _attn_shift_kernel(qt_ref, kt_ref, skip_ref, q_ref, k_ref, v_ref, *rest, fox):
    if fox:
        fcol_ref, frow_ref, o_ref, acc_sc = rest
    else:
        bias_ref, o_ref, acc_sc = rest
    p_id = pl.program_id(1)
    qb = qt_ref[p_id]
    kb = kt_ref[p_id]
    tq, tk = q_ref.shape[0], k_ref.shape[0]
    rc = ATT_RC
    qk_scale = (HEAD_DIM ** -0.5) * LOG2E

    @pl.when(kb == qb)
    def _():
        acc_sc[...] = jnp.zeros(acc_sc.shape, jnp.float32)

    def chunks(diag):
        v1 = jnp.concatenate([v_ref[...], jnp.ones(v_ref.shape, v_ref.dtype)], axis=1)
        if fox:
            col_minus_row = (lax.broadcasted_iota(jnp.int32, (rc, tk), 1)
                             - lax.broadcasted_iota(jnp.int32, (rc, tk), 0))
        for t in range(tq // rc):
            rows = slice(t * rc, (t + 1) * rc)
            kc = (t + 1) * rc if diag else tk
            s = lax.dot_general(q_ref[rows, :], k_ref[0:kc, :], (((1,), (1,)), ((), ())),
                                preferred_element_type=jnp.float32) * qk_scale
            if fox:
                s = s + fcol_ref[rows, :] - frow_ref[:, 0:kc]
                if diag:
                    s = jnp.where(col_minus_row[:, 0:kc] <= t * rc, s, NEG_BIG)
            else:
                s = s + bias_ref[rows, 0:kc].astype(jnp.float32)
            p = jnp.exp2(s).astype(v_ref.dtype)
            acc_sc[rows, :] += jnp.dot(p, v1[0:kc, :], preferred_element_type=jnp.float32)

    pl.when(kb == qb)(functools.partial(chunks, True))
    live = skip_ref[pl.program_id(0), p_id] == 0
    pl.when(jnp.logical_and(kb != qb, live))(functools.partial(chunks, False))

    @pl.when(kb == 0)
    def _():
        o_ref[...] = (acc_sc[:, 0:HEAD_DIM] / acc_sc[:, HEAD_DIM:]).astype(o_ref.dtype)


def _attention_shift(qk, v, q_col, k_col, v_col, *, fcol=None, frow=None, skip=None, bias=None):
    s = qk.shape[0]
    n_heads = N_FOX_HEADS
    fox = bias is None
    tq = tk = min(ATT_T if fox else ATT_T_DSA, s)
    pairs = _block_pairs(s, tq)
    qt = jnp.asarray([a for a, _ in pairs], jnp.int32)
    kt = jnp.asarray([b for _, b in pairs], jnp.int32)
    fox = bias is None
    if skip is None:
        skip = jnp.zeros((n_heads, len(pairs)), jnp.int32)
    in_specs = [
        pl.BlockSpec((tq, HEAD_DIM), lambda h, p, qt, kt, sk: (qt[p], q_col + h)),
        pl.BlockSpec((tk, HEAD_DIM), lambda h, p, qt, kt, sk: (kt[p], k_col + h)),
        pl.BlockSpec((tk, HEAD_DIM), lambda h, p, qt, kt, sk: (kt[p], v_col + h)),
    ]
    if fox:
        in_specs += [pl.BlockSpec((None, tq, 1), lambda h, p, qt, kt, sk: (h, qt[p], 0)),
                     pl.BlockSpec((None, 1, tk), lambda h, p, qt, kt, sk: (h, 0, kt[p]))]
        operands = (qk, qk, v, fcol, frow)
    else:
        in_specs += [pl.BlockSpec((tq, tk), lambda h, p, qt, kt, sk: (qt[p], kt[p]))]
        operands = (qk, qk, v, bias)
    grid_spec = pltpu.PrefetchScalarGridSpec(
        num_scalar_prefetch=3,
        grid=(n_heads, len(pairs)),
        in_specs=in_specs,
        out_specs=pl.BlockSpec((tq, HEAD_DIM), lambda h, p, qt, kt, sk: (qt[p], h)),
        scratch_shapes=[pltpu.VMEM((tq, 2 * HEAD_DIM), jnp.float32)],
    )
    return pl.pallas_call(
        functools.partial(_attn_shift_kernel, fox=fox),
        grid_spec=grid_spec,
        out_shape=jax.ShapeDtypeStruct((s, n_heads * HEAD_DIM), jnp.bfloat16),
        compiler_params=_cparams(2),
        name="fox_attention_shift" if fox else "dsa_attention_shift",
    )(qt, kt, skip, *operands)


def _cast_kernel(x_ref, o_ref):
    o_ref[...] = x_ref[...].astype(o_ref.dtype)


def _cast_bf16(w):
    l, k, n = w.shape
    rows = math.gcd(CAST_ROWS, k)
    spec = pl.BlockSpec((None, rows, n), lambda i, j: (i, j, 0))
    return pl.pallas_call(
        _cast_kernel,
        grid=(l, k // rows),
        in_specs=[spec],
        out_specs=spec,
        out_shape=jax.ShapeDtypeStruct((l, k, n), jnp.bfloat16),
        compiler_params=_cparams(2),
        name="cast_bf16",
    )(w)


def _rope_angles(s, rot):
    half = rot // 2
    pos = jnp.arange(s, dtype=jnp.float32)
    inv = jnp.power(jnp.float32(ROPE_THETA), -2.0 * jnp.arange(half, dtype=jnp.float32) / rot)
    ang = pos[:, None] * inv[None, :]
    return jnp.cos(ang), jnp.sin(ang)


def _rope_tables(s, rot, period):
    half = rot // 2
    cos, sin = _rope_angles(s, rot)
    ones = jnp.ones((s, period - rot), jnp.float32)
    zeros = lambda n: jnp.zeros((s, n), jnp.float32)
    c = jnp.concatenate([cos, cos, ones], axis=1)
    s_up = jnp.concatenate([-sin, zeros(period - half)], axis=1)
    s_dn = jnp.concatenate([zeros(half), sin, zeros(period - rot)], axis=1)
    reps = LANES // period
    return tuple(jnp.tile(t, (1, reps)) for t in (c, s_up, s_dn))


def _pad_row(v, n):
    return jnp.pad(v, (0, n - v.shape[0])).reshape(1, n)


def kernel(x, p, g_ffn1, w1_gate, w1_up, w1_down, g_mix, w_in, b_f, g_qa, g_ka, g_qb, g_kb, g_ik,
           w_br_fox, w_br_dsa, w_o, g_ffn2, w2_gate, w2_up, w2_down, g_ple, w_ple_gate, w_ple_proj):
    b, s, d = x.shape
    assert b == 1
    bf16 = jnp.bfloat16
    topk = min(INDEX_TOPK_MAX, s // 4)
    idx_scale = (N_IDX_HEADS ** -0.5) * (IDX_DIM ** -0.5)
    rope_b = _rope_tables(s, HEAD_DIM // ROT_FRAC_DIV, HEAD_DIM)
    half_b = HEAD_DIM // ROT_FRAC_DIV // 2
    rope_i = _rope_tables(s, IDX_DIM // ROT_FRAC_DIV, IDX_DIM)
    half_i = IDX_DIM // ROT_FRAC_DIV // 2
    row_tab = lambda t: (t, (MM_TM, LANES), lambda i, j: (i, 0))
    res_tile = lambda h, tn=MM_TN: (h, (MM_TM, tn), lambda i, j: (i, j))
    blk = lambda cols: cols // MM_TN_WIDE
    tn_res = min(MM_TN_WIDE, d)

    w1g, w1u, w1d = _cast_bf16(w1_gate), _cast_bf16(w1_up), _cast_bf16(w1_down)
    w2g, w2u, w2d = _cast_bf16(w2_gate), _cast_bf16(w2_up), _cast_bf16(w2_down)
    wbf, wbd, wo = _cast_bf16(w_br_fox), _cast_bf16(w_br_dsa), _cast_bf16(w_o)
    wpg, wpp = _cast_bf16(w_ple_gate), _cast_bf16(w_ple_proj)
    off_fa = 3 * FOX_W
    off_b = off_fa + N_FOX_HEADS
    off_ki = off_b + 3 * DSA_W + IDX_Q_W
    off_wi = off_ki + IDX_DIM
    off_g = off_wi + N_IDX_HEADS
    w_in_bf = w_in.astype(bf16)
    w_a = w_in_bf
    w_b = w_in_bf[:, :, off_b:off_ki]
    w_g = w_in_bf[:, :, off_g:]
    assert off_fa % LANES == 0 and off_g <= (off_ki // LANES + 1) * LANES
    kw_window = off_ki // LANES
    ki_lane, wi_lane = off_ki - kw_window * LANES, off_wi - kw_window * LANES
    p_bf = p.reshape(DEPTH, s, PLE_DIM).astype(bf16)

    def ffn(h, g, wg, wu, wd, layer, tag):
        u = _rmsnorm(h, g)
        a = _matmul([u], [wg, wu], [(0, 0), (0, 1)], [], _ep_swiglu, bf16,
                    layer=layer, name=tag + "_swiglu")
        return _matmul([a], [wd], [(0, 0)], [res_tile(h, tn_res)],
                       functools.partial(_ep_residual, scale=0.5), jnp.float32,
                       layer=layer, name=tag + "_down", tn=tn_res)

    h = x.reshape(s, d)
    for i in range(DEPTH):
        h = ffn(h, g_ffn1[i], w1g, w1u, w1d, i, "ffn1")

        u = _rmsnorm(h, g_mix[i])
        gain_a = jnp.concatenate([jnp.tile(g_qa[i], N_FOX_HEADS), jnp.tile(g_ka[i], N_FOX_HEADS)])
        gain_b = jnp.concatenate([jnp.tile(g_qb[i], N_DSA_HEADS), jnp.tile(g_kb[i], N_DSA_HEADS)])
        col_vec = lambda v: (v.reshape(1, -1), (1, MM_TN_WIDE), lambda i, j: (0, j))

        qk_a = _matmul([u], [w_a], [(0, 0)], [col_vec(gain_a)],
                       functools.partial(_ep_headnorm, rope_half=0), bf16,
                       layer=i, name="proj_qk_fox", tn=MM_TN_WIDE, n=2 * FOX_W)
        v_a = _matmul([u], [w_a], [(0, 0)], [], _ep_cast, bf16,
                      layer=i, name="proj_v_fox", tn=MM_TN_WIDE, n=FOX_W, rhs_off=[blk(2 * FOX_W)])
        qk_b = _matmul([u], [w_b], [(0, 0)], [col_vec(gain_b)] + [row_tab(t) for t in rope_b],
                       functools.partial(_ep_headnorm, rope_half=half_b), bf16,
                       layer=i, name="proj_qk_dsa", tn=MM_TN_WIDE, n=2 * DSA_W)
        v_b = _matmul([u], [w_b], [(0, 0)], [], _ep_cast, bf16,
                      layer=i, name="proj_v_dsa", tn=MM_TN_WIDE, n=DSA_W, rhs_off=[blk(2 * DSA_W)])
        q_i = _matmul([u], [w_b], [(0, 0)], [row_tab(t) for t in rope_i],
                      functools.partial(_ep_rope, rope_half=half_i), bf16,
                      layer=i, name="proj_qidx", tn=MM_TN_WIDE, n=IDX_Q_W, rhs_off=[blk(3 * DSA_W)])
        gates = _matmul([u], [w_g], [(0, 0)], [], _ep_sigmoid, bf16, layer=i, name="proj_gates", tn=MM_TN_WIDE)
        z_f = _matmul([u], [w_in_bf], [(0, 0)], [], _ep_cast, jnp.float32, layer=i,
                      name="proj_forget", tn=LANES, n=LANES, rhs_off=[off_fa // LANES])
        z_kw = _matmul([u], [w_in_bf], [(0, 0)], [], _ep_cast, jnp.float32, layer=i,
                       name="proj_idx_key", tn=LANES, n=LANES, rhs_off=[kw_window])

        f_cum, ki_e, ki_o, wi_s = _small(z_f, z_kw, _pad_row(b_f[i], LANES),
                                         _pad_row(g_ik[i], LANES), *rope_i, idx_scale,
                                         ki_lane, wi_lane)

        f_t = f_cum[:, :N_FOX_HEADS].T
        skip = _fox_skip_flags(f_t, g_qa[i], g_ka[i])
        shift_a = _logit_bound2(g_qa[i], g_ka[i])
        o_a = lax.cond(
            2.0 * shift_a < FIXED_SHIFT_MAX_RANGE,
            lambda: _attention_shift(qk_a, v_a, 0, N_FOX_HEADS, 0,
                                     fcol=(f_t * LOG2E - shift_a)[:, :, None],
                                     frow=(f_t * LOG2E)[:, None, :], skip=skip),
            lambda: _attention(qk_a, v_a, 0, N_FOX_HEADS, 0, frow=f_t[:, None, :], skip=skip))

        shift_b = _logit_bound2(g_qb[i], g_kb[i])
        fixed_b = 2.0 * shift_b < FIXED_SHIFT_MAX_RANGE
        bias = _index_bias(q_i, ki_e.T, ki_o.T, wi_s, topk, jnp.where(fixed_b, -shift_b, 0.0))
        o_b = lax.cond(
            fixed_b,
            lambda: _attention_shift(qk_b, v_b, 0, N_DSA_HEADS, 0, bias=bias),
            lambda: _attention(qk_b, v_b, 0, N_DSA_HEADS, 0, bias=bias))

        n_gate_blocks = D_MODEL // tn_res
        y = _matmul([o_a, o_b], [wbf, wbd], [(0, 0), (1, 1)],
                    [(gates, (MM_TM, tn_res), lambda i, j: (i, j)),
                     (gates, (MM_TM, tn_res), lambda i, j: (i, j + n_gate_blocks))],
                    _ep_merge, bf16, layer=i, name="branch_merge", tn=tn_res)
        h = _matmul([y], [wo], [(0, 0)], [res_tile(h, tn_res)],
                    functools.partial(_ep_residual, scale=1.0), jnp.float32,
                    layer=i, name="out_proj", tn=tn_res)

        h = ffn(h, g_ffn2[i], w2g, w2u, w2d, i, "ffn2")

        u = _rmsnorm(h, g_ple[i])
        h = _matmul([u, p_bf[i]], [wpg, wpp], [(0, 0), (1, 1)], [res_tile(h)],
                    _ep_ple, jnp.float32, layer=i, name="ple")
    return h.reshape(b, s, d)
```

```python
import functools
import math

import jax
import jax.numpy as jnp
from jax import lax
from jax.experimental import pallas as pl
from jax.experimental.pallas import tpu as pltpu

D_MODEL = 4096
SEQ = 8192
DEPTH = 4
HEAD_DIM = 128
N_FOX_HEADS = 16
N_DSA_HEADS = 16
N_IDX_HEADS = 32
IDX_DIM = 64
INDEX_TOPK_MAX = 256
D_FF = 3584
PLE_DIM = 256
ROPE_THETA = 500000.0
ROT_FRAC_DIV = 4
NORM_EPS = 1e-6
NEG_BIG = -1e30
LOG2E = 1.4426950408889634
EXP2_ZERO_BELOW = -160.0
FIXED_SHIFT_MAX_RANGE = 100.0
BF16_NORM_SLACK = 1.02
FOX_W = N_FOX_HEADS * HEAD_DIM
DSA_W = N_DSA_HEADS * HEAD_DIM
IDX_Q_W = N_IDX_HEADS * IDX_DIM

LANES = 128
VMEM_LIMIT = 56 * 1024 * 1024

MM_TM = 1024
MM_TN = 512
MM_TN_WIDE = 1024
NORM_TM = 512
CAST_ROWS = 512
ATT_T = 1024
ATT_T_DSA = 2048
ATT_RC = 256
ATT_HP = 2
IDX_TQ = 256
IDX_TC = 512
IDX_SG = 128

_INT_MIN = -(2 ** 31)
COL_BITS = 14


def _cparams(n_axes):
    return pltpu.CompilerParams(
        dimension_semantics=("arbitrary",) * n_axes,
        vmem_limit_bytes=VMEM_LIMIT,
    )


def _rmsnorm_kernel(x_ref, g_ref, o_ref):
    x = x_ref[...]
    ms = jnp.mean(x * x, axis=-1, keepdims=True)
    o_ref[...] = (x * lax.rsqrt(ms + NORM_EPS) * g_ref[...]).astype(o_ref.dtype)


def _rmsnorm(x, g):
    m, d = x.shape
    return pl.pallas_call(
        _rmsnorm_kernel,
        grid=(m // NORM_TM,),
        in_specs=[pl.BlockSpec((NORM_TM, d), lambda i: (i, 0)),
                  pl.BlockSpec((1, d), lambda i: (0, 0))],
        out_specs=pl.BlockSpec((NORM_TM, d), lambda i: (i, 0)),
        out_shape=jax.ShapeDtypeStruct((m, d), jnp.bfloat16),
        compiler_params=_cparams(1),
        name="rmsnorm",
    )(x, g.reshape(1, d))


def _mm_kernel(*refs, n_lhs, n_rhs, pairs, n_extra, epilogue):
    lhs = refs[:n_lhs]
    rhs = refs[n_lhs:n_lhs + n_rhs]
    extra = refs[n_lhs + n_rhs:n_lhs + n_rhs + n_extra]
    out = refs[n_lhs + n_rhs + n_extra]
    accs = [jnp.dot(lhs[a][...], rhs[b][...], preferred_element_type=jnp.float32)
            for a, b in pairs]
    epilogue(accs, extra, out)


def _matmul(lhs, rhs, pairs, extras, epilogue, out_dtype, *, layer, name, tn=MM_TN, n=None,
            rhs_off=None):
    m = lhs[0].shape[0]
    n = rhs[0].shape[2] if n is None else n
    rhs_off = [0] * len(rhs) if rhs_off is None else rhs_off
    tm = MM_TM
    in_specs = []
    for a in lhs:
        in_specs.append(pl.BlockSpec((tm, a.shape[1]), lambda i, j: (i, 0)))
    for w, off in zip(rhs, rhs_off):
        in_specs.append(pl.BlockSpec((None, w.shape[1], tn),
                                     lambda i, j, off=off: (layer, 0, j + off)))
    for _, bs, im in extras:
        in_specs.append(pl.BlockSpec(bs, im))
    kern = functools.partial(_mm_kernel, n_lhs=len(lhs), n_rhs=len(rhs), pairs=pairs,
                             n_extra=len(extras), epilogue=epilogue)
    return pl.pallas_call(
        kern,
        grid=(m // tm, n // tn),
        in_specs=in_specs,
        out_specs=pl.BlockSpec((tm, tn), lambda i, j: (i, j)),
        out_shape=jax.ShapeDtypeStruct((m, n), out_dtype),
        compiler_params=_cparams(2),
        name=name,
    )(*lhs, *rhs, *[e[0] for e in extras])


def _ep_cast(accs, extra, out):
    out[...] = accs[0].astype(out.dtype)


def _ep_sigmoid(accs, extra, out):
    out[...] = jax.nn.sigmoid(accs[0]).astype(out.dtype)


def _ep_swiglu(accs, extra, out):
    out[...] = (jax.nn.silu(accs[0]) * accs[1]).astype(out.dtype)


def _ep_residual(accs, extra, out, *, scale):
    out[...] = extra[0][...] + scale * accs[0]


def _ep_merge(accs, extra, out):
    ga = extra[0][...].astype(jnp.float32)
    gb = extra[1][...].astype(jnp.float32)
    out[...] = (ga * accs[0] + gb * accs[1]).astype(out.dtype)


def _ep_ple(accs, extra, out):
    out[...] = extra[0][...] + jax.nn.sigmoid(accs[0]) * accs[1]


def _rope_lanes(y, cos, s_up, s_dn, half):
    up = pltpu.roll(y, LANES - half, 1)
    dn = pltpu.roll(y, half, 1)
    return y * cos + up * s_up + dn * s_dn


def _ep_headnorm(accs, extra, out, *, rope_half):
    acc = accs[0]
    gain = extra[0][...]
    groups = [slice(g * LANES, (g + 1) * LANES) for g in range(acc.shape[1] // LANES)]
    normed = []
    for sl in groups:
        x = acc[:, sl]
        ms = jnp.mean(x * x, axis=-1, keepdims=True)
        normed.append(x * lax.rsqrt(ms + NORM_EPS) * gain[:, sl])
    for sl, y in zip(groups, normed):
        if rope_half:
            y = _rope_lanes(y, extra[1][...], extra[2][...], extra[3][...], rope_half)
        out[:, sl] = y.astype(out.dtype)


def _ep_rope(accs, extra, out, *, rope_half):
    acc = accs[0]
    for g in range(acc.shape[1] // LANES):
        sl = slice(g * LANES, (g + 1) * LANES)
        y = _rope_lanes(acc[:, sl], extra[0][...], extra[1][...], extra[2][...], rope_half)
        out[:, sl] = y.astype(out.dtype)


def _small_kernel(zf_ref, zkw_ref, bf_ref, gik_ref, cos_ref, sup_ref, sdn_ref,
                  f_ref, kie_ref, kio_ref, wi_ref, carry_sc, *, idx_scale, ki_lane, wi_lane):
    s = zf_ref.shape[0]
    lane = lax.broadcasted_iota(jnp.int32, (s, LANES), 1)

    @pl.when(pl.program_id(0) == 0)
    def _():
        carry_sc[...] = jnp.zeros(carry_sc.shape, jnp.float32)

    x = jnp.where(lane < N_FOX_HEADS, zf_ref[...], 0.0) + bf_ref[...]
    logf = jnp.minimum(x, 0.0) - jnp.log1p(jnp.exp(-jnp.abs(x)))
    row = lax.broadcasted_iota(jnp.int32, (s, LANES), 0)
    d = 1
    while d < s:
        logf = logf + jnp.where(row >= d, pltpu.roll(logf, d, 0), 0.0)
        d *= 2
    f_ref[...] = logf + carry_sc[0:1, :]
    carry_sc[0:1, :] = f_ref[s - 1:s, :]
    zkw = zkw_ref[...]
    k = jnp.where(lane < IDX_DIM, pltpu.roll(zkw, (LANES - ki_lane) % LANES, 1), 0.0)
    ms = jnp.sum(k * k, axis=-1, keepdims=True) * (1.0 / IDX_DIM)
    k = k * lax.rsqrt(ms + NORM_EPS) * gik_ref[...]
    k = _rope_lanes(k, cos_ref[...], sup_ref[...], sdn_ref[...], IDX_DIM // ROT_FRAC_DIV // 2)
    k = jnp.where(lane < IDX_DIM, k, 0.0)
    kie_ref[...] = k.astype(kie_ref.dtype)
    kio_ref[...] = pltpu.roll(k, IDX_DIM, 1).astype(kio_ref.dtype)
    wi = pltpu.roll(zkw, (LANES - wi_lane) % LANES, 1)
    wi_ref[...] = jnp.where(lane < N_IDX_HEADS, wi, 0.0) * idx_scale


def _small(zf, zkw, bf, gik, cos, sup, sdn, idx_scale, ki_lane, wi_lane):
    s = zf.shape[0]
    tm = MM_TM
    rows = lambda w: pl.BlockSpec((tm, w), lambda i: (i, 0))
    vec = pl.BlockSpec((1, LANES), lambda i: (0, 0))
    return pl.pallas_call(
        functools.partial(_small_kernel, idx_scale=idx_scale, ki_lane=ki_lane, wi_lane=wi_lane),
        grid=(s // tm,),
        in_specs=[rows(LANES), rows(LANES), vec, vec, rows(LANES), rows(LANES), rows(LANES)],
        out_specs=[rows(LANES)] * 4,
        out_shape=[jax.ShapeDtypeStruct((s, LANES), jnp.float32),
                   jax.ShapeDtypeStruct((s, LANES), jnp.bfloat16),
                   jax.ShapeDtypeStruct((s, LANES), jnp.bfloat16),
                   jax.ShapeDtypeStruct((s, LANES), jnp.float32)],
        scratch_shapes=[pltpu.VMEM((8, LANES), jnp.float32)],
        compiler_params=_cparams(1),
        name="small_heads",
    )(zf, zkw, bf, gik, cos, sup, sdn)


def _sortable(x):
    b = pltpu.bitcast(x, jnp.int32)
    return jnp.where(b < 0, b ^ jnp.int32(0x7FFFFFFF), b)


def _index_kernel(qi_ref, kte_ref, kto_ref, wi_ref, keep_ref, bias_ref, keys_sc, wb_sc, *, topk):
    tq, s = bias_ref.shape
    tc = IDX_TC
    i = pl.program_id(0)
    row0 = i * tq
    n_chunks = (row0 + tq + tc - 1) // tc
    n_tiles = tc // LANES

    for j in range(N_IDX_HEADS):
        wb_sc[j] = jnp.broadcast_to(wi_ref[:, j:j + 1], (tq, LANES))

    row_ids = row0 + lax.broadcasted_iota(jnp.int32, (tq, tc), 0)
    col_iota = lax.broadcasted_iota(jnp.int32, (tq, tc), 1)

    def score_chunk(c, carry):
        col0 = pl.multiple_of(c * tc, tc)
        kte = kte_ref[:, pl.ds(col0, tc)]
        kto = kto_ref[:, pl.ds(col0, tc)]
        score = jnp.zeros((tq, tc), jnp.float32)
        for p in range(N_IDX_HEADS // 2):
            qp = qi_ref[:, p * LANES:(p + 1) * LANES]
            re = jnp.maximum(jnp.dot(qp, kte, preferred_element_type=jnp.float32), 0.0)
            ro = jnp.maximum(jnp.dot(qp, kto, preferred_element_type=jnp.float32), 0.0)
            we = jnp.concatenate([wb_sc[2 * p]] * n_tiles, axis=1)
            wo = jnp.concatenate([wb_sc[2 * p + 1]] * n_tiles, axis=1)
            score = score + re * we + ro * wo
        causal = (col0 + col_iota) <= row_ids
        score = jnp.where(score == 0.0, 0.0, score)
        score = jnp.where(causal, score, NEG_BIG)
        keys_sc[:, pl.ds(col0, tc)] = _sortable(score)
        return carry

    lax.fori_loop(0, n_chunks, score_chunk, 0)

    @pl.when(n_chunks % 2 == 1)
    def _():
        col0 = pl.multiple_of(n_chunks * tc, tc)
        keys_sc[:, pl.ds(col0, tc)] = _sortable(jnp.full((tq, tc), NEG_BIG, jnp.float32))

    tcc = 2 * tc

    def search(rows):
        n_rows = rows.stop - rows.start

        def one_bit(b, thr, kept):
            cand = thr + lax.shift_left(jnp.int32(1), 31 - b)

            def count_chunk(c, cnt):
                col0 = pl.multiple_of(c * tcc, tcc)
                k = keys_sc[rows, pl.ds(col0, tcc)]
                for t in range(tcc // LANES):
                    ge = k[:, t * LANES:(t + 1) * LANES] >= cand
                    cnt = cnt + jnp.where(ge, 1.0, 0.0)
                return cnt

            cnt = lax.fori_loop(0, (n_chunks + 1) // 2, count_chunk,
                                jnp.zeros((n_rows, LANES), jnp.float32))
            total = jnp.sum(cnt, axis=1, keepdims=True)
            accept = total >= topk
            return jnp.where(accept, cand, thr), jnp.where(accept, total, kept)

        def bit_step(state):
            b, thr, kept, _ = state
            thr, kept = one_bit(b, thr, kept)
            thr, kept = one_bit(b + 1, thr, kept)
            unresolved = jnp.max(jnp.where(kept == topk, 0.0, 1.0))
            return b + 2, thr, kept, unresolved

        def search_on(state):
            b, _, _, unresolved = state
            return jnp.logical_and(b < 32, unresolved > 0.5)

        _, thr, kept, _ = lax.while_loop(
            search_on, bit_step,
            (jnp.int32(0), jnp.full((n_rows, LANES), _INT_MIN, jnp.int32),
             jnp.full((n_rows, LANES), float(2 * s), jnp.float32), jnp.float32(1.0)))

        def tie_bound():
            def count(pred):
                def chunk(c, cnt):
                    col0 = pl.multiple_of(c * tcc, tcc)
                    k = keys_sc[rows, pl.ds(col0, tcc)]
                    for t in range(tcc // LANES):
                        hit = pred(k[:, t * LANES:(t + 1) * LANES], col0 + t * LANES)
                        cnt = cnt + jnp.where(hit, 1.0, 0.0)
                    return cnt
                cnt = lax.fori_loop(0, (n_chunks + 1) // 2, chunk,
                                    jnp.zeros((n_rows, LANES), jnp.float32))
                return jnp.sum(cnt, axis=1, keepdims=True)

            lane = lax.broadcasted_iota(jnp.int32, (n_rows, LANES), 1)
            need = topk - count(lambda k, col: k > thr)

            def col_bit(j, bound):
                cand = bound + lax.shift_left(jnp.int32(1), COL_BITS - 1 - j)
                below = count(lambda k, col: jnp.logical_and(k == thr, col + lane < cand))
                return jnp.where(below <= need, cand, bound)

            return lax.fori_loop(0, COL_BITS, col_bit, jnp.zeros((n_rows, LANES), jnp.int32))

        ties = jnp.max(jnp.where(kept > topk, 1.0, 0.0)) > 0.5
        bound = lax.cond(ties, tie_bound, lambda: jnp.full((n_rows, LANES), s, jnp.int32))
        return thr, bound

    found = [search(slice(g * IDX_SG, (g + 1) * IDX_SG)) for g in range(tq // IDX_SG)]
    thr = jnp.concatenate([f[0] for f in found], axis=0)
    bound = jnp.concatenate([f[1] for f in found], axis=0)

    def bias_chunk(c, carry):
        col0 = pl.multiple_of(c * tc, tc)
        k = keys_sc[:, pl.ds(col0, tc)]
        cols = col0 + col_iota
        thr_w = jnp.concatenate([thr] * n_tiles, axis=1)
        bound_w = jnp.concatenate([bound] * n_tiles, axis=1)
        keep = jnp.logical_or(k > thr_w, jnp.logical_and(k == thr_w, cols < bound_w))
        keep = jnp.logical_and(keep, cols <= row_ids)
        bias_ref[:, pl.ds(col0, tc)] = jnp.where(keep, keep_ref[...], NEG_BIG).astype(bias_ref.dtype)
        return carry

    lax.fori_loop(0, n_chunks, bias_chunk, 0)

    def fill_chunk(c, carry):
        col0 = pl.multiple_of(c * tc, tc)
        bias_ref[:, pl.ds(col0, tc)] = jnp.full((tq, tc), NEG_BIG, bias_ref.dtype)
        return carry

    lax.fori_loop(n_chunks, s // tc, fill_chunk, 0)


def _index_bias(qi, kte, kto, wi, topk, keep_value):
    s = qi.shape[0]
    tq = IDX_TQ
    return pl.pallas_call(
        functools.partial(_index_kernel, topk=topk),
        grid=(s // tq,),
        in_specs=[pl.BlockSpec((tq, IDX_Q_W), lambda i: (i, 0)),
                  pl.BlockSpec((LANES, s), lambda i: (0, 0)),
                  pl.BlockSpec((LANES, s), lambda i: (0, 0)),
                  pl.BlockSpec((tq, LANES), lambda i: (i, 0)),
                  pl.BlockSpec((1, IDX_TC), lambda i: (0, 0))],
        out_specs=pl.BlockSpec((tq, s), lambda i: (i, 0)),
        out_shape=jax.ShapeDtypeStruct((s, s), jnp.bfloat16),
        scratch_shapes=[pltpu.VMEM((tq, s), jnp.int32),
                        pltpu.VMEM((N_IDX_HEADS, tq, LANES), jnp.float32)],
        compiler_params=_cparams(1),
        name="index_topk_bias",
    )(qi, kte, kto, wi, jnp.full((1, IDX_TC), keep_value, jnp.float32))


def _attn_kernel(qt_ref, kt_ref, skip_ref, q_ref, k_ref, v_ref, *rest, fox):
    if fox:
        fq_ref, fk_ref, o_ref, m_sc, l_sc, acc_sc, s_sc, p_sc, alpha_sc = rest
    else:
        bias_ref, o_ref, m_sc, l_sc, acc_sc, s_sc, p_sc, alpha_sc = rest
    p_id = pl.program_id(1)
    qb = qt_ref[p_id]
    kb = kt_ref[p_id]
    tq, tk = q_ref.shape[0], k_ref.shape[0]
    rc = ATT_RC
    qk_scale = (HEAD_DIM ** -0.5) * LOG2E

    @pl.when(kb == qb)
    def _():
        m_sc[...] = jnp.full(m_sc.shape, NEG_BIG, jnp.float32)
        l_sc[...] = jnp.zeros(l_sc.shape, jnp.float32)
        acc_sc[...] = jnp.zeros(acc_sc.shape, jnp.float32)

    if fox:
        key_bias = (fq_ref[:, 0:1] - fk_ref[...]) * LOG2E
        col_minus_row = (lax.broadcasted_iota(jnp.int32, (rc, tk), 1)
                         - lax.broadcasted_iota(jnp.int32, (rc, tk), 0))

    def key_cols(t, diag):
        return (t + 1) * rc if diag else tk

    def logits_stage(t, diag):
        rows = slice(t * rc, (t + 1) * rc)
        kc = key_cols(t, diag)
        s = lax.dot_general(q_ref[rows, :], k_ref[0:kc, :], (((1,), (1,)), ((), ())),
                            preferred_element_type=jnp.float32) * qk_scale
        if fox:
            s = s + key_bias[:, 0:kc]
            if diag:
                s = jnp.where(col_minus_row[:, 0:kc] <= t * rc, s, NEG_BIG)
        else:
            s = s + bias_ref[rows, 0:kc].astype(jnp.float32)
        s_sc[t % 2, :, 0:kc] = s

    def softmax_stage(t, diag):
        rows = slice(t * rc, (t + 1) * rc)
        kc = key_cols(t, diag)
        s = s_sc[t % 2, :, 0:kc]
        m_prev = m_sc[rows, :]
        m_new = jnp.maximum(m_prev, jnp.max(s, axis=1, keepdims=True))
        alpha = jnp.exp2(m_prev - m_new)
        p = jnp.exp2(s - m_new)
        l_sc[rows, :] = alpha * l_sc[rows, :] + jnp.sum(p, axis=1, keepdims=True)
        m_sc[rows, :] = m_new
        alpha_sc[t % 2] = alpha
        p_sc[t % 2, :, 0:kc] = p.astype(p_sc.dtype)

    def value_stage(t, diag):
        rows = slice(t * rc, (t + 1) * rc)
        kc = key_cols(t, diag)
        acc_sc[rows, :] = alpha_sc[t % 2] * acc_sc[rows, :] + jnp.dot(
            p_sc[t % 2, :, 0:kc], v_ref[0:kc, :], preferred_element_type=jnp.float32)

    def all_chunks(diag):
        n_rc = tq // rc
        for t in range(n_rc + 2):
            if t < n_rc:
                logits_stage(t, diag)
            if 1 <= t <= n_rc:
                softmax_stage(t - 1, diag)
            if t >= 2:
                value_stage(t - 2, diag)

    pl.when(kb == qb)(functools.partial(all_chunks, True))
    live = skip_ref[pl.program_id(0), p_id] == 0
    pl.when(jnp.logical_and(kb != qb, live))(functools.partial(all_chunks, False))

    @pl.when(kb == 0)
    def _():
        o_ref[...] = (acc_sc[...] / l_sc[...]).astype(o_ref.dtype)


def _block_pairs(s, t):
    nq = s // t
    return [(a, b) for a in range(nq) for b in range(a, -1, -1)]


def _logit_bound2(g_q, g_k):
    return (HEAD_DIM * jnp.max(jnp.abs(g_q)) * jnp.max(jnp.abs(g_k)) * BF16_NORM_SLACK
            * (HEAD_DIM ** -0.5) * LOG2E)


def _fox_skip_flags(f_t, g_q, g_k):
    s = f_t.shape[1]
    bound2 = _logit_bound2(g_q, g_k)
    pairs = _block_pairs(s, ATT_T)
    qrow = jnp.asarray([a * ATT_T for a, _ in pairs], jnp.int32)
    kcol = jnp.asarray([b * ATT_T + ATT_T - 1 for _, b in pairs], jnp.int32)
    gap2 = (f_t[:, qrow] - f_t[:, kcol]) * LOG2E
    return (gap2 + 2.0 * bound2 < EXP2_ZERO_BELOW).astype(jnp.int32)


def _attention(qk, v, q_col, k_col, v_col, *, frow=None, skip=None, bias=None):
    s = qk.shape[0]
    n_heads = N_FOX_HEADS
    fox = bias is None
    tq = tk = min(ATT_T if fox else ATT_T_DSA, s)
    pairs = _block_pairs(s, tq)
    qt = jnp.asarray([a for a, _ in pairs], jnp.int32)
    kt = jnp.asarray([b for _, b in pairs], jnp.int32)
    fox = bias is None
    if skip is None:
        skip = jnp.zeros((n_heads, len(pairs)), jnp.int32)
    in_specs = [
        pl.BlockSpec((tq, HEAD_DIM), lambda h, p, qt, kt, sk: (qt[p], q_col + h)),
        pl.BlockSpec((tk, HEAD_DIM), lambda h, p, qt, kt, sk: (kt[p], k_col + h)),
        pl.BlockSpec((tk, HEAD_DIM), lambda h, p, qt, kt, sk: (kt[p], v_col + h)),
    ]
    if fox:
        in_specs += [pl.BlockSpec((None, 1, tq), lambda h, p, qt, kt, sk: (h, 0, qt[p])),
                     pl.BlockSpec((None, 1, tk), lambda h, p, qt, kt, sk: (h, 0, kt[p]))]
        operands = (qk, qk, v, frow, frow)
    else:
        in_specs += [pl.BlockSpec((tq, tk), lambda h, p, qt, kt, sk: (qt[p], kt[p]))]
        operands = (qk, qk, v, bias)
    grid_spec = pltpu.PrefetchScalarGridSpec(
        num_scalar_prefetch=3,
        grid=(n_heads, len(pairs)),
        in_specs=in_specs,
        out_specs=pl.BlockSpec((tq, HEAD_DIM), lambda h, p, qt, kt, sk: (qt[p], h)),
        scratch_shapes=[pltpu.VMEM((tq, 1), jnp.float32),
                        pltpu.VMEM((tq, 1), jnp.float32),
                        pltpu.VMEM((tq, HEAD_DIM), jnp.float32),
                        pltpu.VMEM((2, ATT_RC, tk), jnp.float32),
                        pltpu.VMEM((2, ATT_RC, tk), jnp.bfloat16),
                        pltpu.VMEM((2, ATT_RC, 1), jnp.float32)],
    )
    return pl.pallas_call(
        functools.partial(_attn_kernel, fox=fox),
        grid_spec=grid_spec,
        out_shape=jax.ShapeDtypeStruct((s, n_heads * HEAD_DIM), jnp.bfloat16),
        compiler_params=_cparams(2),
        name="fox_attention" if fox else "dsa_attention",
    )(qt, kt, skip, *operands)


def _attn_shift_kernel(qt_ref, kt_ref, skip_ref, q_ref, k_ref, v_ref, *rest, fox):
    if fox:
        fcol_ref, frow_ref, o_ref, acc_sc = rest
    else:
        bias_ref, o_ref, acc_sc = rest
    p_id = pl.program_id(1)
    qb = qt_ref[p_id]
    kb = kt_ref[p_id]
    tq, tk = q_ref.shape[0], k_ref.shape[0]
    rc = ATT_RC
    qk_scale = (HEAD_DIM ** -0.5) * LOG2E

    @pl.when(kb == qb)
    def _():
        acc_sc[...] = jnp.zeros(acc_sc.shape, jnp.float32)

    def chunks(diag, heads):
        if fox:
            col_minus_row = (lax.broadcasted_iota(jnp.int32, (rc, tk), 1)
                             - lax.broadcasted_iota(jnp.int32, (rc, tk), 0))
        for hh in heads:
            lanes = slice(hh * HEAD_DIM, (hh + 1) * HEAD_DIM)
            v = v_ref[:, lanes]
            v1 = jnp.concatenate([v, jnp.ones(v.shape, v.dtype)], axis=1)
            for t in range(tq // rc):
                rows = slice(t * rc, (t + 1) * rc)
                kc = (t + 1) * rc if diag else tk
                s = lax.dot_general(q_ref[rows, lanes], k_ref[0:kc, lanes],
                                    (((1,), (1,)), ((), ())),
                                    preferred_element_type=jnp.float32) * qk_scale
                if fox:
                    s = s + fcol_ref[hh, rows, :] - frow_ref[hh, :, 0:kc]
                    if diag:
                        s = jnp.where(col_minus_row[:, 0:kc] <= t * rc, s, NEG_BIG)
                else:
                    s = s + bias_ref[rows, 0:kc].astype(jnp.float32)
                p = jnp.exp2(s).astype(v.dtype)
                acc_sc[hh, rows, :] += jnp.dot(p, v1[0:kc, :], preferred_element_type=jnp.float32)

    all_heads = tuple(range(ATT_HP))
    pl.when(kb == qb)(functools.partial(chunks, True, all_heads))
    if fox:
        for hh in all_heads:
            live = skip_ref[pl.program_id(0) * ATT_HP + hh, p_id] == 0
            pl.when(jnp.logical_and(kb != qb, live))(functools.partial(chunks, False, (hh,)))
    else:
        pl.when(kb != qb)(functools.partial(chunks, False, all_heads))

    @pl.when(kb == 0)
    def _():
        for hh in all_heads:
            lanes = slice(hh * HEAD_DIM, (hh + 1) * HEAD_DIM)
            o_ref[:, lanes] = (acc_sc[hh, :, 0:HEAD_DIM] / acc_sc[hh, :, HEAD_DIM:]).astype(o_ref.dtype)


def _attention_shift(qk, v, q_col, k_col, v_col, *, fcol=None, frow=None, skip=None, bias=None):
    s = qk.shape[0]
    n_heads = N_FOX_HEADS
    hp = ATT_HP
    assert n_heads % hp == 0 and q_col % hp == 0 and k_col % hp == 0 and v_col % hp == 0
    fox = bias is None
    tq = tk = min(ATT_T if fox else ATT_T_DSA, s)
    pairs = _block_pairs(s, tq)
    qt = jnp.asarray([a for a, _ in pairs], jnp.int32)
    kt = jnp.asarray([b for _, b in pairs], jnp.int32)
    if skip is None:
        skip = jnp.zeros((n_heads, len(pairs)), jnp.int32)
    w = hp * HEAD_DIM
    in_specs = [
        pl.BlockSpec((tq, w), lambda h, p, qt, kt, sk: (qt[p], q_col // hp + h)),
        pl.BlockSpec((tk, w), lambda h, p, qt, kt, sk: (kt[p], k_col // hp + h)),
        pl.BlockSpec((tk, w), lambda h, p, qt, kt, sk: (kt[p], v_col // hp + h)),
    ]
    if fox:
        in_specs += [pl.BlockSpec((hp, tq, 1), lambda h, p, qt, kt, sk: (h, qt[p], 0)),
                     pl.BlockSpec((hp, 1, tk), lambda h, p, qt, kt, sk: (h, 0, kt[p]))]
        operands = (qk, qk, v, fcol, frow)
    else:
        in_specs += [pl.BlockSpec((tq, tk), lambda h, p, qt, kt, sk: (qt[p], kt[p]))]
        operands = (qk, qk, v, bias)
    grid_spec = pltpu.PrefetchScalarGridSpec(
        num_scalar_prefetch=3,
        grid=(n_heads // hp, len(pairs)),
        in_specs=in_specs,
        out_specs=pl.BlockSpec((tq, w), lambda h, p, qt, kt, sk: (qt[p], h)),
        scratch_shapes=[pltpu.VMEM((hp, tq, 2 * HEAD_DIM), jnp.float32)],
    )
    return pl.pallas_call(
        functools.partial(_attn_shift_kernel, fox=fox),
        grid_spec=grid_spec,
        out_shape=jax.ShapeDtypeStruct((s, n_heads * HEAD_DIM), jnp.bfloat16),
        compiler_params=_cparams(2),
        name="fox_attention_shift" if fox else "dsa_attention_shift",
    )(qt, kt, skip, *operands)


def _cast_kernel(x_ref, o_ref):
    o_ref[...] = x_ref[...].astype(o_ref.dtype)


def _cast_bf16(w):
    l, k, n = w.shape
    rows = math.gcd(CAST_ROWS, k)
    spec = pl.BlockSpec((None, rows, n), lambda i, j: (i, j, 0))
    return pl.pallas_call(
        _cast_kernel,
        grid=(l, k // rows),
        in_specs=[spec],
        out_specs=spec,
        out_shape=jax.ShapeDtypeStruct((l, k, n), jnp.bfloat16),
        compiler_params=_cparams(2),
        name="cast_bf16",
    )(w)


def _rope_angles(s, rot):
    half = rot // 2
    pos = jnp.arange(s, dtype=jnp.float32)
    inv = jnp.power(jnp.float32(ROPE_THETA), -2.0 * jnp.arange(half, dtype=jnp.float32) / rot)
    ang = pos[:, None] * inv[None, :]
    return jnp.cos(ang), jnp.sin(ang)


def _rope_tables(s, rot, period):
    half = rot // 2
    cos, sin = _rope_angles(s, rot)
    ones = jnp.ones((s, period - rot), jnp.float32)
    zeros = lambda n: jnp.zeros((s, n), jnp.float32)
    c = jnp.concatenate([cos, cos, ones], axis=1)
    s_up = jnp.concatenate([-sin, zeros(period - half)], axis=1)
    s_dn = jnp.concatenate([zeros(half), sin, zeros(period - rot)], axis=1)
    reps = LANES // period
    return tuple(jnp.tile(t, (1, reps)) for t in (c, s_up, s_dn))


def _pad_row(v, n):
    return jnp.pad(v, (0, n - v.shape[0])).reshape(1, n)


def kernel(x, p, g_ffn1, w1_gate, w1_up, w1_down, g_mix, w_in, b_f, g_qa, g_ka, g_qb, g_kb, g_ik,
           w_br_fox, w_br_dsa, w_o, g_ffn2, w2_gate, w2_up, w2_down, g_ple, w_ple_gate, w_ple_proj):
    b, s, d = x.shape
    assert b == 1
    bf16 = jnp.bfloat16
    topk = min(INDEX_TOPK_MAX, s // 4)
    idx_scale = (N_IDX_HEADS ** -0.5) * (IDX_DIM ** -0.5)
    rope_b = _rope_tables(s, HEAD_DIM // ROT_FRAC_DIV, HEAD_DIM)
    half_b = HEAD_DIM // ROT_FRAC_DIV // 2
    rope_i = _rope_tables(s, IDX_DIM // ROT_FRAC_DIV, IDX_DIM)
    half_i = IDX_DIM // ROT_FRAC_DIV // 2
    row_tab = lambda t: (t, (MM_TM, LANES), lambda i, j: (i, 0))
    res_tile = lambda h, tn=MM_TN: (h, (MM_TM, tn), lambda i, j: (i, j))
    blk = lambda cols: cols // MM_TN_WIDE
    tn_res = min(MM_TN_WIDE, d)

    w1g, w1u, w1d = _cast_bf16(w1_gate), _cast_bf16(w1_up), _cast_bf16(w1_down)
    w2g, w2u, w2d = _cast_bf16(w2_gate), _cast_bf16(w2_up), _cast_bf16(w2_down)
    wbf, wbd, wo = _cast_bf16(w_br_fox), _cast_bf16(w_br_dsa), _cast_bf16(w_o)
    wpg, wpp = _cast_bf16(w_ple_gate), _cast_bf16(w_ple_proj)
    off_fa = 3 * FOX_W
    off_b = off_fa + N_FOX_HEADS
    off_ki = off_b + 3 * DSA_W + IDX_Q_W
    off_wi = off_ki + IDX_DIM
    off_g = off_wi + N_IDX_HEADS
    w_in_bf = w_in.astype(bf16)
    w_a = w_in_bf
    w_b = w_in_bf[:, :, off_b:off_ki]
    w_g = w_in_bf[:, :, off_g:]
    assert off_fa % LANES == 0 and off_g <= (off_ki // LANES + 1) * LANES
    kw_window = off_ki // LANES
    ki_lane, wi_lane = off_ki - kw_window * LANES, off_wi - kw_window * LANES
    p_bf = p.reshape(DEPTH, s, PLE_DIM).astype(bf16)

    def ffn(h, g, wg, wu, wd, layer, tag):
        u = _rmsnorm(h, g)
        a = _matmul([u], [wg, wu], [(0, 0), (0, 1)], [], _ep_swiglu, bf16,
                    layer=layer, name=tag + "_swiglu")
        return _matmul([a], [wd], [(0, 0)], [res_tile(h, tn_res)],
                       functools.partial(_ep_residual, scale=0.5), jnp.float32,
                       layer=layer, name=tag + "_down", tn=tn_res)

    h = x.reshape(s, d)
    for i in range(DEPTH):
        h = ffn(h, g_ffn1[i], w1g, w1u, w1d, i, "ffn1")

        u = _rmsnorm(h, g_mix[i])
        gain_a = jnp.concatenate([jnp.tile(g_qa[i], N_FOX_HEADS), jnp.tile(g_ka[i], N_FOX_HEADS)])
        gain_b = jnp.concatenate([jnp.tile(g_qb[i], N_DSA_HEADS), jnp.tile(g_kb[i], N_DSA_HEADS)])
        col_vec = lambda v: (v.reshape(1, -1), (1, MM_TN_WIDE), lambda i, j: (0, j))

        qk_a = _matmul([u], [w_a], [(0, 0)], [col_vec(gain_a)],
                       functools.partial(_ep_headnorm, rope_half=0), bf16,
                       layer=i, name="proj_qk_fox", tn=MM_TN_WIDE, n=2 * FOX_W)
        v_a = _matmul([u], [w_a], [(0, 0)], [], _ep_cast, bf16,
                      layer=i, name="proj_v_fox", tn=MM_TN_WIDE, n=FOX_W, rhs_off=[blk(2 * FOX_W)])
        qk_b = _matmul([u], [w_b], [(0, 0)], [col_vec(gain_b)] + [row_tab(t) for t in rope_b],
                       functools.partial(_ep_headnorm, rope_half=half_b), bf16,
                       layer=i, name="proj_qk_dsa", tn=MM_TN_WIDE, n=2 * DSA_W)
        v_b = _matmul([u], [w_b], [(0, 0)], [], _ep_cast, bf16,
                      layer=i, name="proj_v_dsa", tn=MM_TN_WIDE, n=DSA_W, rhs_off=[blk(2 * DSA_W)])
        q_i = _matmul([u], [w_b], [(0, 0)], [row_tab(t) for t in rope_i],
                      functools.partial(_ep_rope, rope_half=half_i), bf16,
                      layer=i, name="proj_qidx", tn=MM_TN_WIDE, n=IDX_Q_W, rhs_off=[blk(3 * DSA_W)])
        gates = _matmul([u], [w_g], [(0, 0)], [], _ep_sigmoid, bf16, layer=i, name="proj_gates", tn=MM_TN_WIDE)
        z_f = _matmul([u], [w_in_bf], [(0, 0)], [], _ep_cast, jnp.float32, layer=i,
                      name="proj_forget", tn=LANES, n=LANES, rhs_off=[off_fa // LANES])
        z_kw = _matmul([u], [w_in_bf], [(0, 0)], [], _ep_cast, jnp.float32, layer=i,
                       name="proj_idx_key", tn=LANES, n=LANES, rhs_off=[kw_window])

        f_cum, ki_e, ki_o, wi_s = _small(z_f, z_kw, _pad_row(b_f[i], LANES),
                                         _pad_row(g_ik[i], LANES), *rope_i, idx_scale,
                                         ki_lane, wi_lane)

        f_t = f_cum[:, :N_FOX_HEADS].T
        skip = _fox_skip_flags(f_t, g_qa[i], g_ka[i])
        shift_a = _logit_bound2(g_qa[i], g_ka[i])
        o_a = lax.cond(
            2.0 * shift_a < FIXED_SHIFT_MAX_RANGE,
            lambda: _attention_shift(qk_a, v_a, 0, N_FOX_HEADS, 0,
                                     fcol=(f_t * LOG2E - shift_a)[:, :, None],
                                     frow=(f_t * LOG2E)[:, None, :], skip=skip),
            lambda: _attention(qk_a, v_a, 0, N_FOX_HEADS, 0, frow=f_t[:, None, :], skip=skip))

        shift_b = _logit_bound2(g_qb[i], g_kb[i])
        fixed_b = 2.0 * shift_b < FIXED_SHIFT_MAX_RANGE
        bias = _index_bias(q_i, ki_e.T, ki_o.T, wi_s, topk, jnp.where(fixed_b, -shift_b, 0.0))
        o_b = lax.cond(
            fixed_b,
            lambda: _attention_shift(qk_b, v_b, 0, N_DSA_HEADS, 0, bias=bias),
            lambda: _attention(qk_b, v_b, 0, N_DSA_HEADS, 0, bias=bias))

        n_gate_blocks = D_MODEL // tn_res
        y = _matmul([o_a, o_b], [wbf, wbd], [(0, 0), (1, 1)],
                    [(gates, (MM_TM, tn_res), lambda i, j: (i, j)),
                     (gates, (MM_TM, tn_res), lambda i, j: (i, j + n_gate_blocks))],
                    _ep_merge, bf16, layer=i, name="branch_merge", tn=tn_res)
        h = _matmul([y], [wo], [(0, 0)], [res_tile(h, tn_res)],
                    functools.partial(_ep_residual, scale=1.0), jnp.float32,
                    layer=i, name="out_proj", tn=tn_res)

        h = ffn(h, g_ffn2[i], w2g, w2u, w2d, i, "ffn2")

        u = _rmsnorm(h, g_ple[i])
        h = _matmul([u, p_bf[i]], [wpg, wpp], [(0, 0), (1, 1)], [res_tile(h)],
                    _ep_ple, jnp.float32, layer=i, name="ple")
    return h.reshape(b, s, d)
```

```python
import functools
import math

import jax
import jax.numpy as jnp
from jax import lax
from jax.experimental import pallas as pl
from jax.experimental.pallas import tpu as pltpu

D_MODEL = 4096
SEQ = 8192
DEPTH = 4
HEAD_DIM = 128
N_FOX_HEADS = 16
N_DSA_HEADS = 16
N_IDX_HEADS = 32
IDX_DIM = 64
INDEX_TOPK_MAX = 256
D_FF = 3584
PLE_DIM = 256
ROPE_THETA = 500000.0
ROT_FRAC_DIV = 4
NORM_EPS = 1e-6
NEG_BIG = -1e30
LOG2E = 1.4426950408889634
EXP2_ZERO_BELOW = -160.0
FIXED_SHIFT_MAX_RANGE = 100.0
BF16_NORM_SLACK = 1.02
FOX_W = N_FOX_HEADS * HEAD_DIM
DSA_W = N_DSA_HEADS * HEAD_DIM
IDX_Q_W = N_IDX_HEADS * IDX_DIM

LANES = 128
VMEM_LIMIT = 56 * 1024 * 1024

MM_TM = 1024
MM_TN = 512
MM_TN_WIDE = 1024
NORM_TM = 512
CAST_ROWS = 512
ATT_T = 1024
ATT_T_DSA = 2048
ATT_RC = 256
ATT_HP = 4
IDX_TQ = 256
IDX_TC = 512
IDX_SG = 128

_INT_MIN = -(2 ** 31)
COL_BITS = 14


def _cparams(n_axes):
    return pltpu.CompilerParams(
        dimension_semantics=("arbitrary",) * n_axes,
        vmem_limit_bytes=VMEM_LIMIT,
    )


def _rmsnorm_kernel(x_ref, g_ref, o_ref):
    x = x_ref[...]
    ms = jnp.mean(x * x, axis=-1, keepdims=True)
    o_ref[...] = (x * lax.rsqrt(ms + NORM_EPS) * g_ref[...]).astype(o_ref.dtype)


def _rmsnorm(x, g):
    m, d = x.shape
    return pl.pallas_call(
        _rmsnorm_kernel,
        grid=(m // NORM_TM,),
        in_specs=[pl.BlockSpec((NORM_TM, d), lambda i: (i, 0)),
                  pl.BlockSpec((1, d), lambda i: (0, 0))],
        out_specs=pl.BlockSpec((NORM_TM, d), lambda i: (i, 0)),
        out_shape=jax.ShapeDtypeStruct((m, d), jnp.bfloat16),
        compiler_params=_cparams(1),
        name="rmsnorm",
    )(x, g.reshape(1, d))


def _mm_kernel(*refs, n_lhs, n_rhs, pairs, n_extra, epilogue):
    lhs = refs[:n_lhs]
    rhs = refs[n_lhs:n_lhs + n_rhs]
    extra = refs[n_lhs + n_rhs:n_lhs + n_rhs + n_extra]
    out = refs[n_lhs + n_rhs + n_extra]
    accs = [jnp.dot(lhs[a][...], rhs[b][...], preferred_element_type=jnp.float32)
            for a, b in pairs]
    epilogue(accs, extra, out)


def _matmul(lhs, rhs, pairs, extras, epilogue, out_dtype, *, layer, name, tn=MM_TN, n=None,
            rhs_off=None):
    m = lhs[0].shape[0]
    n = rhs[0].shape[2] if n is None else n
    rhs_off = [0] * len(rhs) if rhs_off is None else rhs_off
    tm = MM_TM
    in_specs = []
    for a in lhs:
        in_specs.append(pl.BlockSpec((tm, a.shape[1]), lambda i, j: (i, 0)))
    for w, off in zip(rhs, rhs_off):
        in_specs.append(pl.BlockSpec((None, w.shape[1], tn),
                                     lambda i, j, off=off: (layer, 0, j + off)))
    for _, bs, im in extras:
        in_specs.append(pl.BlockSpec(bs, im))
    kern = functools.partial(_mm_kernel, n_lhs=len(lhs), n_rhs=len(rhs), pairs=pairs,
                             n_extra=len(extras), epilogue=epilogue)
    return pl.pallas_call(
        kern,
        grid=(m // tm, n // tn),
        in_specs=in_specs,
        out_specs=pl.BlockSpec((tm, tn), lambda i, j: (i, j)),
        out_shape=jax.ShapeDtypeStruct((m, n), out_dtype),
        compiler_params=_cparams(2),
        name=name,
    )(*lhs, *rhs, *[e[0] for e in extras])


def _ep_cast(accs, extra, out):
    out[...] = accs[0].astype(out.dtype)


def _ep_sigmoid(accs, extra, out):
    out[...] = jax.nn.sigmoid(accs[0]).astype(out.dtype)


def _ep_swiglu(accs, extra, out):
    out[...] = (jax.nn.silu(accs[0]) * accs[1]).astype(out.dtype)


def _ep_residual(accs, extra, out, *, scale):
    out[...] = extra[0][...] + scale * accs[0]


def _ep_merge(accs, extra, out):
    ga = extra[0][...].astype(jnp.float32)
    gb = extra[1][...].astype(jnp.float32)
    out[...] = (ga * accs[0] + gb * accs[1]).astype(out.dtype)


def _ep_ple(accs, extra, out):
    out[...] = extra[0][...] + jax.nn.sigmoid(accs[0]) * accs[1]


def _rope_lanes(y, cos, s_up, s_dn, half):
    up = pltpu.roll(y, LANES - half, 1)
    dn = pltpu.roll(y, half, 1)
    return y * cos + up * s_up + dn * s_dn


def _ep_headnorm(accs, extra, out, *, rope_half):
    acc = accs[0]
    gain = extra[0][...]
    groups = [slice(g * LANES, (g + 1) * LANES) for g in range(acc.shape[1] // LANES)]
    normed = []
    for sl in groups:
        x = acc[:, sl]
        ms = jnp.mean(x * x, axis=-1, keepdims=True)
        normed.append(x * lax.rsqrt(ms + NORM_EPS) * gain[:, sl])
    for sl, y in zip(groups, normed):
        if rope_half:
            y = _rope_lanes(y, extra[1][...], extra[2][...], extra[3][...], rope_half)
        out[:, sl] = y.astype(out.dtype)


def _ep_rope(accs, extra, out, *, rope_half):
    acc = accs[0]
    for g in range(acc.shape[1] // LANES):
        sl = slice(g * LANES, (g + 1) * LANES)
        y = _rope_lanes(acc[:, sl], extra[0][...], extra[1][...], extra[2][...], rope_half)
        out[:, sl] = y.astype(out.dtype)


def _small_kernel(zf_ref, zkw_ref, bf_ref, gik_ref, cos_ref, sup_ref, sdn_ref,
                  f_ref, kie_ref, kio_ref, wi_ref, carry_sc, *, idx_scale, ki_lane, wi_lane):
    s = zf_ref.shape[0]
    lane = lax.broadcasted_iota(jnp.int32, (s, LANES), 1)

    @pl.when(pl.program_id(0) == 0)
    def _():
        carry_sc[...] = jnp.zeros(carry_sc.shape, jnp.float32)

    x = jnp.where(lane < N_FOX_HEADS, zf_ref[...], 0.0) + bf_ref[...]
    logf = jnp.minimum(x, 0.0) - jnp.log1p(jnp.exp(-jnp.abs(x)))
    row = lax.broadcasted_iota(jnp.int32, (s, LANES), 0)
    d = 1
    while d < s:
        logf = logf + jnp.where(row >= d, pltpu.roll(logf, d, 0), 0.0)
        d *= 2
    f_ref[...] = logf + carry_sc[0:1, :]
    carry_sc[0:1, :] = f_ref[s - 1:s, :]
    zkw = zkw_ref[...]
    k = jnp.where(lane < IDX_DIM, pltpu.roll(zkw, (LANES - ki_lane) % LANES, 1), 0.0)
    ms = jnp.sum(k * k, axis=-1, keepdims=True) * (1.0 / IDX_DIM)
    k = k * lax.rsqrt(ms + NORM_EPS) * gik_ref[...]
    k = _rope_lanes(k, cos_ref[...], sup_ref[...], sdn_ref[...], IDX_DIM // ROT_FRAC_DIV // 2)
    k = jnp.where(lane < IDX_DIM, k, 0.0)
    kie_ref[...] = k.astype(kie_ref.dtype)
    kio_ref[...] = pltpu.roll(k, IDX_DIM, 1).astype(kio_ref.dtype)
    wi = pltpu.roll(zkw, (LANES - wi_lane) % LANES, 1)
    wi_ref[...] = jnp.where(lane < N_IDX_HEADS, wi, 0.0) * idx_scale


def _small(zf, zkw, bf, gik, cos, sup, sdn, idx_scale, ki_lane, wi_lane):
    s = zf.shape[0]
    tm = MM_TM
    rows = lambda w: pl.BlockSpec((tm, w), lambda i: (i, 0))
    vec = pl.BlockSpec((1, LANES), lambda i: (0, 0))
    return pl.pallas_call(
        functools.partial(_small_kernel, idx_scale=idx_scale, ki_lane=ki_lane, wi_lane=wi_lane),
        grid=(s // tm,),
        in_specs=[rows(LANES), rows(LANES), vec, vec, rows(LANES), rows(LANES), rows(LANES)],
        out_specs=[rows(LANES)] * 4,
        out_shape=[jax.ShapeDtypeStruct((s, LANES), jnp.float32),
                   jax.ShapeDtypeStruct((s, LANES), jnp.bfloat16),
                   jax.ShapeDtypeStruct((s, LANES), jnp.bfloat16),
                   jax.ShapeDtypeStruct((s, LANES), jnp.float32)],
        scratch_shapes=[pltpu.VMEM((8, LANES), jnp.float32)],
        compiler_params=_cparams(1),
        name="small_heads",
    )(zf, zkw, bf, gik, cos, sup, sdn)


def _sortable(x):
    b = pltpu.bitcast(x, jnp.int32)
    return jnp.where(b < 0, b ^ jnp.int32(0x7FFFFFFF), b)


def _index_kernel(qi_ref, kte_ref, kto_ref, wi_ref, keep_ref, bias_ref, keys_sc, wb_sc, *, topk):
    tq, s = bias_ref.shape
    tc = IDX_TC
    i = pl.program_id(0)
    row0 = i * tq
    n_chunks = (row0 + tq + tc - 1) // tc
    n_tiles = tc // LANES

    for j in range(N_IDX_HEADS):
        wb_sc[j] = jnp.broadcast_to(wi_ref[:, j:j + 1], (tq, LANES))

    row_ids = row0 + lax.broadcasted_iota(jnp.int32, (tq, tc), 0)
    col_iota = lax.broadcasted_iota(jnp.int32, (tq, tc), 1)

    def score_chunk(c, carry):
        col0 = pl.multiple_of(c * tc, tc)
        kte = kte_ref[:, pl.ds(col0, tc)]
        kto = kto_ref[:, pl.ds(col0, tc)]
        score = jnp.zeros((tq, tc), jnp.float32)
        for p in range(N_IDX_HEADS // 2):
            qp = qi_ref[:, p * LANES:(p + 1) * LANES]
            re = jnp.maximum(jnp.dot(qp, kte, preferred_element_type=jnp.float32), 0.0)
            ro = jnp.maximum(jnp.dot(qp, kto, preferred_element_type=jnp.float32), 0.0)
            we = jnp.concatenate([wb_sc[2 * p]] * n_tiles, axis=1)
            wo = jnp.concatenate([wb_sc[2 * p + 1]] * n_tiles, axis=1)
            score = score + re * we + ro * wo
        causal = (col0 + col_iota) <= row_ids
        score = jnp.where(score == 0.0, 0.0, score)
        score = jnp.where(causal, score, NEG_BIG)
        keys_sc[:, pl.ds(col0, tc)] = _sortable(score)
        return carry

    lax.fori_loop(0, n_chunks, score_chunk, 0)

    @pl.when(n_chunks % 2 == 1)
    def _():
        col0 = pl.multiple_of(n_chunks * tc, tc)
        keys_sc[:, pl.ds(col0, tc)] = _sortable(jnp.full((tq, tc), NEG_BIG, jnp.float32))

    tcc = 2 * tc

    def search(rows):
        n_rows = rows.stop - rows.start

        def one_bit(b, thr, kept):
            cand = thr + lax.shift_left(jnp.int32(1), 31 - b)

            def count_chunk(c, cnt):
                col0 = pl.multiple_of(c * tcc, tcc)
                k = keys_sc[rows, pl.ds(col0, tcc)]
                for t in range(tcc // LANES):
                    ge = k[:, t * LANES:(t + 1) * LANES] >= cand
                    cnt = cnt + jnp.where(ge, 1.0, 0.0)
                return cnt

            cnt = lax.fori_loop(0, (n_chunks + 1) // 2, count_chunk,
                                jnp.zeros((n_rows, LANES), jnp.float32))
            total = jnp.sum(cnt, axis=1, keepdims=True)
            accept = total >= topk
            return jnp.where(accept, cand, thr), jnp.where(accept, total, kept)

        def bit_step(state):
            b, thr, kept, _ = state
            thr, kept = one_bit(b, thr, kept)
            thr, kept = one_bit(b + 1, thr, kept)
            unresolved = jnp.max(jnp.where(kept == topk, 0.0, 1.0))
            return b + 2, thr, kept, unresolved

        def search_on(state):
            b, _, _, unresolved = state
            return jnp.logical_and(b < 32, unresolved > 0.5)

        _, thr, kept, _ = lax.while_loop(
            search_on, bit_step,
            (jnp.int32(0), jnp.full((n_rows, LANES), _INT_MIN, jnp.int32),
             jnp.full((n_rows, LANES), float(2 * s), jnp.float32), jnp.float32(1.0)))

        def tie_bound():
            def count(pred):
                def chunk(c, cnt):
                    col0 = pl.multiple_of(c * tcc, tcc)
                    k = keys_sc[rows, pl.ds(col0, tcc)]
                    for t in range(tcc // LANES):
                        hit = pred(k[:, t * LANES:(t + 1) * LANES], col0 + t * LANES)
                        cnt = cnt + jnp.where(hit, 1.0, 0.0)
                    return cnt
                cnt = lax.fori_loop(0, (n_chunks + 1) // 2, chunk,
                                    jnp.zeros((n_rows, LANES), jnp.float32))
                return jnp.sum(cnt, axis=1, keepdims=True)

            lane = lax.broadcasted_iota(jnp.int32, (n_rows, LANES), 1)
            need = topk - count(lambda k, col: k > thr)

            def col_bit(j, bound):
                cand = bound + lax.shift_left(jnp.int32(1), COL_BITS - 1 - j)
                below = count(lambda k, col: jnp.logical_and(k == thr, col + lane < cand))
                return jnp.where(below <= need, cand, bound)

            return lax.fori_loop(0, COL_BITS, col_bit, jnp.zeros((n_rows, LANES), jnp.int32))

        ties = jnp.max(jnp.where(kept > topk, 1.0, 0.0)) > 0.5
        bound = lax.cond(ties, tie_bound, lambda: jnp.full((n_rows, LANES), s, jnp.int32))
        return thr, bound

    found = [search(slice(g * IDX_SG, (g + 1) * IDX_SG)) for g in range(tq // IDX_SG)]
    thr = jnp.concatenate([f[0] for f in found], axis=0)
    bound = jnp.concatenate([f[1] for f in found], axis=0)

    def bias_chunk(c, carry):
        col0 = pl.multiple_of(c * tc, tc)
        k = keys_sc[:, pl.ds(col0, tc)]
        cols = col0 + col_iota
        thr_w = jnp.concatenate([thr] * n_tiles, axis=1)
        bound_w = jnp.concatenate([bound] * n_tiles, axis=1)
        keep = jnp.logical_or(k > thr_w, jnp.logical_and(k == thr_w, cols < bound_w))
        keep = jnp.logical_and(keep, cols <= row_ids)
        bias_ref[:, pl.ds(col0, tc)] = jnp.where(keep, keep_ref[...], NEG_BIG).astype(bias_ref.dtype)
        return carry

    lax.fori_loop(0, n_chunks, bias_chunk, 0)

    def fill_chunk(c, carry):
        col0 = pl.multiple_of(c * tc, tc)
        bias_ref[:, pl.ds(col0, tc)] = jnp.full((tq, tc), NEG_BIG, bias_ref.dtype)
        return carry

    lax.fori_loop(n_chunks, s // tc, fill_chunk, 0)


def _index_bias(qi, kte, kto, wi, topk, keep_value):
    s = qi.shape[0]
    tq = IDX_TQ
    return pl.pallas_call(
        functools.partial(_index_kernel, topk=topk),
        grid=(s // tq,),
        in_specs=[pl.BlockSpec((tq, IDX_Q_W), lambda i: (i, 0)),
                  pl.BlockSpec((LANES, s), lambda i: (0, 0)),
                  pl.BlockSpec((LANES, s), lambda i: (0, 0)),
                  pl.BlockSpec((tq, LANES), lambda i: (i, 0)),
                  pl.BlockSpec((1, IDX_TC), lambda i: (0, 0))],
        out_specs=pl.BlockSpec((tq, s), lambda i: (i, 0)),
        out_shape=jax.ShapeDtypeStruct((s, s), jnp.bfloat16),
        scratch_shapes=[pltpu.VMEM((tq, s), jnp.int32),
                        pltpu.VMEM((N_IDX_HEADS, tq, LANES), jnp.float32)],
        compiler_params=_cparams(1),
        name="index_topk_bias",
    )(qi, kte, kto, wi, jnp.full((1, IDX_TC), keep_value, jnp.float32))


def _attn_kernel(qt_ref, kt_ref, skip_ref, q_ref, k_ref, v_ref, *rest, fox):
    if fox:
        fq_ref, fk_ref, o_ref, m_sc, l_sc, acc_sc, s_sc, p_sc, alpha_sc = rest
    else:
        bias_ref, o_ref, m_sc, l_sc, acc_sc, s_sc, p_sc, alpha_sc = rest
    p_id = pl.program_id(1)
    qb = qt_ref[p_id]
    kb = kt_ref[p_id]
    tq, tk = q_ref.shape[0], k_ref.shape[0]
    rc = ATT_RC
    qk_scale = (HEAD_DIM ** -0.5) * LOG2E

    @pl.when(kb == qb)
    def _():
        m_sc[...] = jnp.full(m_sc.shape, NEG_BIG, jnp.float32)
        l_sc[...] = jnp.zeros(l_sc.shape, jnp.float32)
        acc_sc[...] = jnp.zeros(acc_sc.shape, jnp.float32)

    if fox:
        key_bias = (fq_ref[:, 0:1] - fk_ref[...]) * LOG2E
        col_minus_row = (lax.broadcasted_iota(jnp.int32, (rc, tk), 1)
                         - lax.broadcasted_iota(jnp.int32, (rc, tk), 0))

    def key_cols(t, diag):
        return (t + 1) * rc if diag else tk

    def logits_stage(t, diag):
        rows = slice(t * rc, (t + 1) * rc)
        kc = key_cols(t, diag)
        s = lax.dot_general(q_ref[rows, :], k_ref[0:kc, :], (((1,), (1,)), ((), ())),
                            preferred_element_type=jnp.float32) * qk_scale
        if fox:
            s = s + key_bias[:, 0:kc]
            if diag:
                s = jnp.where(col_minus_row[:, 0:kc] <= t * rc, s, NEG_BIG)
        else:
            s = s + bias_ref[rows, 0:kc].astype(jnp.float32)
        s_sc[t % 2, :, 0:kc] = s

    def softmax_stage(t, diag):
        rows = slice(t * rc, (t + 1) * rc)
        kc = key_cols(t, diag)
        s = s_sc[t % 2, :, 0:kc]
        m_prev = m_sc[rows, :]
        m_new = jnp.maximum(m_prev, jnp.max(s, axis=1, keepdims=True))
        alpha = jnp.exp2(m_prev - m_new)
        p = jnp.exp2(s - m_new)
        l_sc[rows, :] = alpha * l_sc[rows, :] + jnp.sum(p, axis=1, keepdims=True)
        m_sc[rows, :] = m_new
        alpha_sc[t % 2] = alpha
        p_sc[t % 2, :, 0:kc] = p.astype(p_sc.dtype)

    def value_stage(t, diag):
        rows = slice(t * rc, (t + 1) * rc)
        kc = key_cols(t, diag)
        acc_sc[rows, :] = alpha_sc[t % 2] * acc_sc[rows, :] + jnp.dot(
            p_sc[t % 2, :, 0:kc], v_ref[0:kc, :], preferred_element_type=jnp.float32)

    def all_chunks(diag):
        n_rc = tq // rc
        for t in range(n_rc + 2):
            if t < n_rc:
                logits_stage(t, diag)
            if 1 <= t <= n_rc:
                softmax_stage(t - 1, diag)
            if t >= 2:
                value_stage(t - 2, diag)

    pl.when(kb == qb)(functools.partial(all_chunks, True))
    live = skip_ref[pl.program_id(0), p_id] == 0
    pl.when(jnp.logical_and(kb != qb, live))(functools.partial(all_chunks, False))

    @pl.when(kb == 0)
    def _():
        o_ref[...] = (acc_sc[...] / l_sc[...]).astype(o_ref.dtype)


def _block_pairs(s, t):
    nq = s // t
    return [(a, b) for a in range(nq) for b in range(a, -1, -1)]


def _logit_bound2(g_q, g_k):
    return (HEAD_DIM * jnp.max(jnp.abs(g_q)) * jnp.max(jnp.abs(g_k)) * BF16_NORM_SLACK
            * (HEAD_DIM ** -0.5) * LOG2E)


def _fox_skip_flags(f_t, g_q, g_k):
    s = f_t.shape[1]
    bound2 = _logit_bound2(g_q, g_k)
    pairs = _block_pairs(s, ATT_T)
    qrow = jnp.asarray([a * ATT_T for a, _ in pairs], jnp.int32)
    kcol = jnp.asarray([b * ATT_T + ATT_T - 1 for _, b in pairs], jnp.int32)
    gap2 = (f_t[:, qrow] - f_t[:, kcol]) * LOG2E
    return (gap2 + 2.0 * bound2 < EXP2_ZERO_BELOW).astype(jnp.int32)


def _attention(qk, v, q_col, k_col, v_col, *, frow=None, skip=None, bias=None):
    s = qk.shape[0]
    n_heads = N_FOX_HEADS
    fox = bias is None
    tq = tk = min(ATT_T if fox else ATT_T_DSA, s)
    pairs = _block_pairs(s, tq)
    qt = jnp.asarray([a for a, _ in pairs], jnp.int32)
    kt = jnp.asarray([b for _, b in pairs], jnp.int32)
    fox = bias is None
    if skip is None:
        skip = jnp.zeros((n_heads, len(pairs)), jnp.int32)
    in_specs = [
        pl.BlockSpec((tq, HEAD_DIM), lambda h, p, qt, kt, sk: (qt[p], q_col + h)),
        pl.BlockSpec((tk, HEAD_DIM), lambda h, p, qt, kt, sk: (kt[p], k_col + h)),
        pl.BlockSpec((tk, HEAD_DIM), lambda h, p, qt, kt, sk: (kt[p], v_col + h)),
    ]
    if fox:
        in_specs += [pl.BlockSpec((None, 1, tq), lambda h, p, qt, kt, sk: (h, 0, qt[p])),
                     pl.BlockSpec((None, 1, tk), lambda h, p, qt, kt, sk: (h, 0, kt[p]))]
        operands = (qk, qk, v, frow, frow)
    else:
        in_specs += [pl.BlockSpec((tq, tk), lambda h, p, qt, kt, sk: (qt[p], kt[p]))]
        operands = (qk, qk, v, bias)
    grid_spec = pltpu.PrefetchScalarGridSpec(
        num_scalar_prefetch=3,
        grid=(n_heads, len(pairs)),
        in_specs=in_specs,
        out_specs=pl.BlockSpec((tq, HEAD_DIM), lambda h, p, qt, kt, sk: (qt[p], h)),
        scratch_shapes=[pltpu.VMEM((tq, 1), jnp.float32),
                        pltpu.VMEM((tq, 1), jnp.float32),
                        pltpu.VMEM((tq, HEAD_DIM), jnp.float32),
                        pltpu.VMEM((2, ATT_RC, tk), jnp.float32),
                        pltpu.VMEM((2, ATT_RC, tk), jnp.bfloat16),
                        pltpu.VMEM((2, ATT_RC, 1), jnp.float32)],
    )
    return pl.pallas_call(
        functools.partial(_attn_kernel, fox=fox),
        grid_spec=grid_spec,
        out_shape=jax.ShapeDtypeStruct((s, n_heads * HEAD_DIM), jnp.bfloat16),
        compiler_params=_cparams(2),
        name="fox_attention" if fox else "dsa_attention",
    )(qt, kt, skip, *operands)


def _attn_shift_kernel(qt_ref, kt_ref, skip_ref, q_ref, k_ref, v_ref, *rest, fox):
    if fox:
        fcol_ref, frow_ref, o_ref, acc_sc = rest
    else:
        bias_ref, o_ref, acc_sc = rest
    p_id = pl.program_id(1)
    qb = qt_ref[p_id]
    kb = kt_ref[p_id]
    tq, tk = q_ref.shape[0], k_ref.shape[0]
    rc = ATT_RC
    qk_scale = (HEAD_DIM ** -0.5) * LOG2E

    @pl.when(kb == qb)
    def _():
        acc_sc[...] = jnp.zeros(acc_sc.shape, jnp.float32)

    def chunks(diag, heads):
        if fox:
            col_minus_row = (lax.broadcasted_iota(jnp.int32, (rc, tk), 1)
                             - lax.broadcasted_iota(jnp.int32, (rc, tk), 0))
        for hh in heads:
            lanes = slice(hh * HEAD_DIM, (hh + 1) * HEAD_DIM)
            v = v_ref[:, lanes]
            v1 = jnp.concatenate([v, jnp.ones(v.shape, v.dtype)], axis=1)
            for t in range(tq // rc):
                rows = slice(t * rc, (t + 1) * rc)
                kc = (t + 1) * rc if diag else tk
                s = lax.dot_general(q_ref[rows, lanes], k_ref[0:kc, lanes],
                                    (((1,), (1,)), ((), ())),
                                    preferred_element_type=jnp.float32) * qk_scale
                if fox:
                    s = s + fcol_ref[hh, rows, :] - frow_ref[hh, :, 0:kc]
                    if diag:
                        s = jnp.where(col_minus_row[:, 0:kc] <= t * rc, s, NEG_BIG)
                else:
                    s = s + bias_ref[rows, 0:kc].astype(jnp.float32)
                p = jnp.exp2(s).astype(v.dtype)
                acc_sc[hh, rows, :] += jnp.dot(p, v1[0:kc, :], preferred_element_type=jnp.float32)

    all_heads = tuple(range(ATT_HP))
    pl.when(kb == qb)(functools.partial(chunks, True, all_heads))
    if fox:
        for hh in all_heads:
            live = skip_ref[pl.program_id(0) * ATT_HP + hh, p_id] == 0
            pl.when(jnp.logical_and(kb != qb, live))(functools.partial(chunks, False, (hh,)))
    else:
        pl.when(kb != qb)(functools.partial(chunks, False, all_heads))

    @pl.when(kb == 0)
    def _():
        for hh in all_heads:
            lanes = slice(hh * HEAD_DIM, (hh + 1) * HEAD_DIM)
            o_ref[:, lanes] = (acc_sc[hh, :, 0:HEAD_DIM] / acc_sc[hh, :, HEAD_DIM:]).astype(o_ref.dtype)


def _attention_shift(qk, v, q_col, k_col, v_col, *, fcol=None, frow=None, skip=None, bias=None):
    s = qk.shape[0]
    n_heads = N_FOX_HEADS
    hp = ATT_HP
    assert n_heads % hp == 0 and q_col % hp == 0 and k_col % hp == 0 and v_col % hp == 0
    fox = bias is None
    tq = tk = min(ATT_T if fox else ATT_T_DSA, s)
    pairs = _block_pairs(s, tq)
    qt = jnp.asarray([a for a, _ in pairs], jnp.int32)
    kt = jnp.asarray([b for _, b in pairs], jnp.int32)
    if skip is None:
        skip = jnp.zeros((n_heads, len(pairs)), jnp.int32)
    w = hp * HEAD_DIM
    in_specs = [
        pl.BlockSpec((tq, w), lambda h, p, qt, kt, sk: (qt[p], q_col // hp + h)),
        pl.BlockSpec((tk, w), lambda h, p, qt, kt, sk: (kt[p], k_col // hp + h)),
        pl.BlockSpec((tk, w), lambda h, p, qt, kt, sk: (kt[p], v_col // hp + h)),
    ]
    if fox:
        in_specs += [pl.BlockSpec((hp, tq, 1), lambda h, p, qt, kt, sk: (h, qt[p], 0)),
                     pl.BlockSpec((hp, 1, tk), lambda h, p, qt, kt, sk: (h, 0, kt[p]))]
        operands = (qk, qk, v, fcol, frow)
    else:
        in_specs += [pl.BlockSpec((tq, tk), lambda h, p, qt, kt, sk: (qt[p], kt[p]))]
        operands = (qk, qk, v, bias)
    grid_spec = pltpu.PrefetchScalarGridSpec(
        num_scalar_prefetch=3,
        grid=(n_heads // hp, len(pairs)),
        in_specs=in_specs,
        out_specs=pl.BlockSpec((tq, w), lambda h, p, qt, kt, sk: (qt[p], h)),
        scratch_shapes=[pltpu.VMEM((hp, tq, 2 * HEAD_DIM), jnp.float32)],
    )
    return pl.pallas_call(
        functools.partial(_attn_shift_kernel, fox=fox),
        grid_spec=grid_spec,
        out_shape=jax.ShapeDtypeStruct((s, n_heads * HEAD_DIM), jnp.bfloat16),
        compiler_params=_cparams(2),
        name="fox_attention_shift" if fox else "dsa_attention_shift",
    )(qt, kt, skip, *operands)


def _cast_kernel(x_ref, o_ref):
    o_ref[...] = x_ref[...].astype(o_ref.dtype)


def _cast_bf16(w):
    l, k, n = w.shape
    rows = math.gcd(CAST_ROWS, k)
    spec = pl.BlockSpec((None, rows, n), lambda i, j: (i, j, 0))
    return pl.pallas_call(
        _cast_kernel,
        grid=(l, k // rows),
        in_specs=[spec],
        out_specs=spec,
        out_shape=jax.ShapeDtypeStruct((l, k, n), jnp.bfloat16),
        compiler_params=_cparams(2),
        name="cast_bf16",
    )(w)


def _rope_angles(s, rot):
    half = rot // 2
    pos = jnp.arange(s, dtype=jnp.float32)
    inv = jnp.power(jnp.float32(ROPE_THETA), -2.0 * jnp.arange(half, dtype=jnp.float32) / rot)
    ang = pos[:, None] * inv[None, :]
    return jnp.cos(ang), jnp.sin(ang)


def _rope_tables(s, rot, period):
    half = rot // 2
    cos, sin = _rope_angles(s, rot)
    ones = jnp.ones((s, period - rot), jnp.float32)
    zeros = lambda n: jnp.zeros((s, n), jnp.float32)
    c = jnp.concatenate([cos, cos, ones], axis=1)
    s_up = jnp.concatenate([-sin, zeros(period - half)], axis=1)
    s_dn = jnp.concatenate([zeros(half), sin, zeros(period - rot)], axis=1)
    reps = LANES // period
    return tuple(jnp.tile(t, (1, reps)) for t in (c, s_up, s_dn))


def _pad_row(v, n):
    return jnp.pad(v, (0, n - v.shape[0])).reshape(1, n)


def kernel(x, p, g_ffn1, w1_gate, w1_up, w1_down, g_mix, w_in, b_f, g_qa, g_ka, g_qb, g_kb, g_ik,
           w_br_fox, w_br_dsa, w_o, g_ffn2, w2_gate, w2_up, w2_down, g_ple, w_ple_gate, w_ple_proj):
    b, s, d = x.shape
    assert b == 1
    bf16 = jnp.bfloat16
    topk = min(INDEX_TOPK_MAX, s // 4)
    idx_scale = (N_IDX_HEADS ** -0.5) * (IDX_DIM ** -0.5)
    rope_b = _rope_tables(s, HEAD_DIM // ROT_FRAC_DIV, HEAD_DIM)
    half_b = HEAD_DIM // ROT_FRAC_DIV // 2
    rope_i = _rope_tables(s, IDX_DIM // ROT_FRAC_DIV, IDX_DIM)
    half_i = IDX_DIM // ROT_FRAC_DIV // 2
    row_tab = lambda t: (t, (MM_TM, LANES), lambda i, j: (i, 0))
    res_tile = lambda h, tn=MM_TN: (h, (MM_TM, tn), lambda i, j: (i, j))
    blk = lambda cols: cols // MM_TN_WIDE
    tn_res = min(MM_TN_WIDE, d)

    w1g, w1u, w1d = _cast_bf16(w1_gate), _cast_bf16(w1_up), _cast_bf16(w1_down)
    w2g, w2u, w2d = _cast_bf16(w2_gate), _cast_bf16(w2_up), _cast_bf16(w2_down)
    wbf, wbd, wo = _cast_bf16(w_br_fox), _cast_bf16(w_br_dsa), _cast_bf16(w_o)
    wpg, wpp = _cast_bf16(w_ple_gate), _cast_bf16(w_ple_proj)
    off_fa = 3 * FOX_W
    off_b = off_fa + N_FOX_HEADS
    off_ki = off_b + 3 * DSA_W + IDX_Q_W
    off_wi = off_ki + IDX_DIM
    off_g = off_wi + N_IDX_HEADS
    w_in_bf = w_in.astype(bf16)
    w_a = w_in_bf
    w_b = w_in_bf[:, :, off_b:off_ki]
    w_g = w_in_bf[:, :, off_g:]
    assert off_fa % LANES == 0 and off_g <= (off_ki // LANES + 1) * LANES
    kw_window = off_ki // LANES
    ki_lane, wi_lane = off_ki - kw_window * LANES, off_wi - kw_window * LANES
    p_bf = p.reshape(DEPTH, s, PLE_DIM).astype(bf16)

    def ffn(h, g, wg, wu, wd, layer, tag):
        u = _rmsnorm(h, g)
        a = _matmul([u], [wg, wu], [(0, 0), (0, 1)], [], _ep_swiglu, bf16,
                    layer=layer, name=tag + "_swiglu")
        return _matmul([a], [wd], [(0, 0)], [res_tile(h, tn_res)],
                       functools.partial(_ep_residual, scale=0.5), jnp.float32,
                       layer=layer, name=tag + "_down", tn=tn_res)

    h = x.reshape(s, d)
    for i in range(DEPTH):
        h = ffn(h, g_ffn1[i], w1g, w1u, w1d, i, "ffn1")

        u = _rmsnorm(h, g_mix[i])
        gain_a = jnp.concatenate([jnp.tile(g_qa[i], N_FOX_HEADS), jnp.tile(g_ka[i], N_FOX_HEADS)])
        gain_b = jnp.concatenate([jnp.tile(g_qb[i], N_DSA_HEADS), jnp.tile(g_kb[i], N_DSA_HEADS)])
        col_vec = lambda v: (v.reshape(1, -1), (1, MM_TN_WIDE), lambda i, j: (0, j))

        qk_a = _matmul([u], [w_a], [(0, 0)], [col_vec(gain_a)],
                       functools.partial(_ep_headnorm, rope_half=0), bf16,
                       layer=i, name="proj_qk_fox", tn=MM_TN_WIDE, n=2 * FOX_W)
        v_a = _matmul([u], [w_a], [(0, 0)], [], _ep_cast, bf16,
                      layer=i, name="proj_v_fox", tn=MM_TN_WIDE, n=FOX_W, rhs_off=[blk(2 * FOX_W)])
        qk_b = _matmul([u], [w_b], [(0, 0)], [col_vec(gain_b)] + [row_tab(t) for t in rope_b],
                       functools.partial(_ep_headnorm, rope_half=half_b), bf16,
                       layer=i, name="proj_qk_dsa", tn=MM_TN_WIDE, n=2 * DSA_W)
        v_b = _matmul([u], [w_b], [(0, 0)], [], _ep_cast, bf16,
                      layer=i, name="proj_v_dsa", tn=MM_TN_WIDE, n=DSA_W, rhs_off=[blk(2 * DSA_W)])
        q_i = _matmul([u], [w_b], [(0, 0)], [row_tab(t) for t in rope_i],
                      functools.partial(_ep_rope, rope_half=half_i), bf16,
                      layer=i, name="proj_qidx", tn=MM_TN_WIDE, n=IDX_Q_W, rhs_off=[blk(3 * DSA_W)])
        gates = _matmul([u], [w_g], [(0, 0)], [], _ep_sigmoid, bf16, layer=i, name="proj_gates", tn=MM_TN_WIDE)
        z_f = _matmul([u], [w_in_bf], [(0, 0)], [], _ep_cast, jnp.float32, layer=i,
                      name="proj_forget", tn=LANES, n=LANES, rhs_off=[off_fa // LANES])
        z_kw = _matmul([u], [w_in_bf], [(0, 0)], [], _ep_cast, jnp.float32, layer=i,
                       name="proj_idx_key", tn=LANES, n=LANES, rhs_off=[kw_window])

        f_cum, ki_e, ki_o, wi_s = _small(z_f, z_kw, _pad_row(b_f[i], LANES),
                                         _pad_row(g_ik[i], LANES), *rope_i, idx_scale,
                                         ki_lane, wi_lane)

        f_t = f_cum[:, :N_FOX_HEADS].T
        skip = _fox_skip_flags(f_t, g_qa[i], g_ka[i])
        shift_a = _logit_bound2(g_qa[i], g_ka[i])
        o_a = lax.cond(
            2.0 * shift_a < FIXED_SHIFT_MAX_RANGE,
            lambda: _attention_shift(qk_a, v_a, 0, N_FOX_HEADS, 0,
                                     fcol=(f_t * LOG2E - shift_a)[:, :, None],
                                     frow=(f_t * LOG2E)[:, None, :], skip=skip),
            lambda: _attention(qk_a, v_a, 0, N_FOX_HEADS, 0, frow=f_t[:, None, :], skip=skip))

        shift_b = _logit_bound2(g_qb[i], g_kb[i])
        fixed_b = 2.0 * shift_b < FIXED_SHIFT_MAX_RANGE
        bias = _index_bias(q_i, ki_e.T, ki_o.T, wi_s, topk, jnp.where(fixed_b, -shift_b, 0.0))
        o_b = lax.cond(
            fixed_b,
            lambda: _attention_shift(qk_b, v_b, 0, N_DSA_HEADS, 0, bias=bias),
            lambda: _attention(qk_b, v_b, 0, N_DSA_HEADS, 0, bias=bias))

        n_gate_blocks = D_MODEL // tn_res
        y = _matmul([o_a, o_b], [wbf, wbd], [(0, 0), (1, 1)],
                    [(gates, (MM_TM, tn_res), lambda i, j: (i, j)),
                     (gates, (MM_TM, tn_res), lambda i, j: (i, j + n_gate_blocks))],
                    _ep_merge, bf16, layer=i, name="branch_merge", tn=tn_res)
        h = _matmul([y], [wo], [(0, 0)], [res_tile(h, tn_res)],
                    functools.partial(_ep_residual, scale=1.0), jnp.float32,
                    layer=i, name="out_proj", tn=tn_res)

        h = ffn(h, g_ffn2[i], w2g, w2u, w2d, i, "ffn2")

        u = _rmsnorm(h, g_ple[i])
        h = _matmul([u, p_bf[i]], [wpg, wpp], [(0, 0), (1, 1)], [res_tile(h)],
                    _ep_ple, jnp.float32, layer=i, name="ple")
    return h.reshape(b, s, d)
```

```python
import functools
import math

import jax
import jax.numpy as jnp
from jax import lax
from jax.experimental import pallas as pl
from jax.experimental.pallas import tpu as pltpu

D_MODEL = 4096
SEQ = 8192
DEPTH = 4
HEAD_DIM = 128
N_FOX_HEADS = 16
N_DSA_HEADS = 16
N_IDX_HEADS = 32
IDX_DIM = 64
INDEX_TOPK_MAX = 256
D_FF = 3584
PLE_DIM = 256
ROPE_THETA = 500000.0
ROT_FRAC_DIV = 4
NORM_EPS = 1e-6
NEG_BIG = -1e30
LOG2E = 1.4426950408889634
EXP2_ZERO_BELOW = -160.0
FIXED_SHIFT_MAX_RANGE = 100.0
BF16_NORM_SLACK = 1.02
FOX_W = N_FOX_HEADS * HEAD_DIM
DSA_W = N_DSA_HEADS * HEAD_DIM
IDX_Q_W = N_IDX_HEADS * IDX_DIM

LANES = 128
VMEM_LIMIT = 56 * 1024 * 1024

MM_TM = 1024
MM_TN = 512
MM_TN_WIDE = 1024
NORM_TM = 512
CAST_ROWS = 512
ATT_T = 1024
ATT_T_DSA = 2048
ATT_RC = 256
ATT_HP = 4
IDX_TQ = 256
IDX_TC = 512
IDX_SG = 128

_INT_MIN = -(2 ** 31)
COL_BITS = 14


def _cparams(n_axes):
    return pltpu.CompilerParams(
        dimension_semantics=("arbitrary",) * n_axes,
        vmem_limit_bytes=VMEM_LIMIT,
    )


def _rmsnorm_kernel(x_ref, g_ref, o_ref):
    x = x_ref[...]
    ms = jnp.mean(x * x, axis=-1, keepdims=True)
    o_ref[...] = (x * lax.rsqrt(ms + NORM_EPS) * g_ref[...]).astype(o_ref.dtype)


def _rmsnorm(x, g):
    m, d = x.shape
    return pl.pallas_call(
        _rmsnorm_kernel,
        grid=(m // NORM_TM,),
        in_specs=[pl.BlockSpec((NORM_TM, d), lambda i: (i, 0)),
                  pl.BlockSpec((1, d), lambda i: (0, 0))],
        out_specs=pl.BlockSpec((NORM_TM, d), lambda i: (i, 0)),
        out_shape=jax.ShapeDtypeStruct((m, d), jnp.bfloat16),
        compiler_params=_cparams(1),
        name="rmsnorm",
    )(x, g.reshape(1, d))


def _mm_kernel(*refs, n_lhs, n_rhs, pairs, n_extra, epilogue):
    lhs = refs[:n_lhs]
    rhs = refs[n_lhs:n_lhs + n_rhs]
    extra = refs[n_lhs + n_rhs:n_lhs + n_rhs + n_extra]
    out = refs[n_lhs + n_rhs + n_extra]
    accs = [jnp.dot(lhs[a][...], rhs[b][...], preferred_element_type=jnp.float32)
            for a, b in pairs]
    epilogue(accs, extra, out)


def _matmul(lhs, rhs, pairs, extras, epilogue, out_dtype, *, layer, name, tn=MM_TN, n=None,
            rhs_off=None):
    m = lhs[0].shape[0]
    n = rhs[0].shape[2] if n is None else n
    rhs_off = [0] * len(rhs) if rhs_off is None else rhs_off
    tm = MM_TM
    in_specs = []
    for a in lhs:
        in_specs.append(pl.BlockSpec((tm, a.shape[1]), lambda i, j: (i, 0)))
    for w, off in zip(rhs, rhs_off):
        in_specs.append(pl.BlockSpec((None, w.shape[1], tn),
                                     lambda i, j, off=off: (layer, 0, j + off)))
    for _, bs, im in extras:
        in_specs.append(pl.BlockSpec(bs, im))
    kern = functools.partial(_mm_kernel, n_lhs=len(lhs), n_rhs=len(rhs), pairs=pairs,
                             n_extra=len(extras), epilogue=epilogue)
    return pl.pallas_call(
        kern,
        grid=(m // tm, n // tn),
        in_specs=in_specs,
        out_specs=pl.BlockSpec((tm, tn), lambda i, j: (i, j)),
        out_shape=jax.ShapeDtypeStruct((m, n), out_dtype),
        compiler_params=_cparams(2),
        name=name,
    )(*lhs, *rhs, *[e[0] for e in extras])


def _ep_cast(accs, extra, out):
    out[...] = accs[0].astype(out.dtype)


def _ep_sigmoid(accs, extra, out):
    out[...] = jax.nn.sigmoid(accs[0]).astype(out.dtype)


def _ep_swiglu(accs, extra, out):
    out[...] = (jax.nn.silu(accs[0]) * accs[1]).astype(out.dtype)


def _ep_residual(accs, extra, out, *, scale):
    out[...] = extra[0][...] + scale * accs[0]


def _ep_merge(accs, extra, out):
    ga = extra[0][...].astype(jnp.float32)
    gb = extra[1][...].astype(jnp.float32)
    out[...] = (ga * accs[0] + gb * accs[1]).astype(out.dtype)


def _ep_ple(accs, extra, out):
    out[...] = extra[0][...] + jax.nn.sigmoid(accs[0]) * accs[1]


def _rope_lanes(y, cos, s_up, s_dn, half):
    up = pltpu.roll(y, LANES - half, 1)
    dn = pltpu.roll(y, half, 1)
    return y * cos + up * s_up + dn * s_dn


def _ep_headnorm(accs, extra, out, *, rope_half):
    acc = accs[0]
    gain = extra[0][...]
    groups = [slice(g * LANES, (g + 1) * LANES) for g in range(acc.shape[1] // LANES)]
    normed = []
    for sl in groups:
        x = acc[:, sl]
        ms = jnp.mean(x * x, axis=-1, keepdims=True)
        normed.append(x * lax.rsqrt(ms + NORM_EPS) * gain[:, sl])
    for sl, y in zip(groups, normed):
        if rope_half:
            y = _rope_lanes(y, extra[1][...], extra[2][...], extra[3][...], rope_half)
        out[:, sl] = y.astype(out.dtype)


def _ep_rope(accs, extra, out, *, rope_half):
    acc = accs[0]
    for g in range(acc.shape[1] // LANES):
        sl = slice(g * LANES, (g + 1) * LANES)
        y = _rope_lanes(acc[:, sl], extra[0][...], extra[1][...], extra[2][...], rope_half)
        out[:, sl] = y.astype(out.dtype)


def _small_kernel(zf_ref, zkw_ref, bf_ref, gik_ref, cos_ref, sup_ref, sdn_ref,
                  f_ref, kie_ref, kio_ref, wi_ref, carry_sc, *, idx_scale, ki_lane, wi_lane):
    s = zf_ref.shape[0]
    lane = lax.broadcasted_iota(jnp.int32, (s, LANES), 1)

    @pl.when(pl.program_id(0) == 0)
    def _():
        carry_sc[...] = jnp.zeros(carry_sc.shape, jnp.float32)

    x = jnp.where(lane < N_FOX_HEADS, zf_ref[...], 0.0) + bf_ref[...]
    logf = jnp.minimum(x, 0.0) - jnp.log1p(jnp.exp(-jnp.abs(x)))
    row = lax.broadcasted_iota(jnp.int32, (s, LANES), 0)
    d = 1
    while d < s:
        logf = logf + jnp.where(row >= d, pltpu.roll(logf, d, 0), 0.0)
        d *= 2
    f_ref[...] = logf + carry_sc[0:1, :]
    carry_sc[0:1, :] = f_ref[s - 1:s, :]
    zkw = zkw_ref[...]
    k = jnp.where(lane < IDX_DIM, pltpu.roll(zkw, (LANES - ki_lane) % LANES, 1), 0.0)
    ms = jnp.sum(k * k, axis=-1, keepdims=True) * (1.0 / IDX_DIM)
    k = k * lax.rsqrt(ms + NORM_EPS) * gik_ref[...]
    k = _rope_lanes(k, cos_ref[...], sup_ref[...], sdn_ref[...], IDX_DIM // ROT_FRAC_DIV // 2)
    k = jnp.where(lane < IDX_DIM, k, 0.0)
    kie_ref[...] = k.astype(kie_ref.dtype)
    kio_ref[...] = pltpu.roll(k, IDX_DIM, 1).astype(kio_ref.dtype)
    wi = pltpu.roll(zkw, (LANES - wi_lane) % LANES, 1)
    wi_ref[...] = jnp.where(lane < N_IDX_HEADS, wi, 0.0) * idx_scale


def _small(zf, zkw, bf, gik, cos, sup, sdn, idx_scale, ki_lane, wi_lane):
    s = zf.shape[0]
    tm = MM_TM
    rows = lambda w: pl.BlockSpec((tm, w), lambda i: (i, 0))
    vec = pl.BlockSpec((1, LANES), lambda i: (0, 0))
    return pl.pallas_call(
        functools.partial(_small_kernel, idx_scale=idx_scale, ki_lane=ki_lane, wi_lane=wi_lane),
        grid=(s // tm,),
        in_specs=[rows(LANES), rows(LANES), vec, vec, rows(LANES), rows(LANES), rows(LANES)],
        out_specs=[rows(LANES)] * 4,
        out_shape=[jax.ShapeDtypeStruct((s, LANES), jnp.float32),
                   jax.ShapeDtypeStruct((s, LANES), jnp.bfloat16),
                   jax.ShapeDtypeStruct((s, LANES), jnp.bfloat16),
                   jax.ShapeDtypeStruct((s, LANES), jnp.float32)],
        scratch_shapes=[pltpu.VMEM((8, LANES), jnp.float32)],
        compiler_params=_cparams(1),
        name="small_heads",
    )(zf, zkw, bf, gik, cos, sup, sdn)


def _sortable(x):
    b = pltpu.bitcast(x, jnp.int32)
    return jnp.where(b < 0, b ^ jnp.int32(0x7FFFFFFF), b)


def _index_kernel(qi_ref, kte_ref, kto_ref, wi_ref, keep_ref, bias_ref, keys_sc, wb_sc, *, topk):
    tq, s = bias_ref.shape
    tc = IDX_TC
    i = pl.program_id(0)
    row0 = i * tq
    n_chunks = (row0 + tq + tc - 1) // tc
    n_tiles = tc // LANES

    for j in range(N_IDX_HEADS):
        wb_sc[j] = jnp.broadcast_to(wi_ref[:, j:j + 1], (tq, LANES))

    row_ids = row0 + lax.broadcasted_iota(jnp.int32, (tq, tc), 0)
    col_iota = lax.broadcasted_iota(jnp.int32, (tq, tc), 1)

    def score_chunk(c, carry):
        col0 = pl.multiple_of(c * tc, tc)
        kte = kte_ref[:, pl.ds(col0, tc)]
        kto = kto_ref[:, pl.ds(col0, tc)]
        score = jnp.zeros((tq, tc), jnp.float32)
        for p in range(N_IDX_HEADS // 2):
            qp = qi_ref[:, p * LANES:(p + 1) * LANES]
            re = jnp.maximum(jnp.dot(qp, kte, preferred_element_type=jnp.float32), 0.0)
            ro = jnp.maximum(jnp.dot(qp, kto, preferred_element_type=jnp.float32), 0.0)
            we = jnp.concatenate([wb_sc[2 * p]] * n_tiles, axis=1)
            wo = jnp.concatenate([wb_sc[2 * p + 1]] * n_tiles, axis=1)
            score = score + re * we + ro * wo
        causal = (col0 + col_iota) <= row_ids
        score = jnp.where(score == 0.0, 0.0, score)
        score = jnp.where(causal, score, NEG_BIG)
        keys_sc[:, pl.ds(col0, tc)] = _sortable(score)
        return carry

    lax.fori_loop(0, n_chunks, score_chunk, 0)

    @pl.when(n_chunks % 2 == 1)
    def _():
        col0 = pl.multiple_of(n_chunks * tc, tc)
        keys_sc[:, pl.ds(col0, tc)] = _sortable(jnp.full((tq, tc), NEG_BIG, jnp.float32))

    tcc = 2 * tc

    def search(rows):
        n_rows = rows.stop - rows.start

        def one_bit(b, thr, kept):
            cand = thr + lax.shift_left(jnp.int32(1), 31 - b)

            def count_chunk(c, cnt):
                col0 = pl.multiple_of(c * tcc, tcc)
                k = keys_sc[rows, pl.ds(col0, tcc)]
                for t in range(tcc // LANES):
                    ge = k[:, t * LANES:(t + 1) * LANES] >= cand
                    cnt = cnt + jnp.where(ge, 1.0, 0.0)
                return cnt

            cnt = lax.fori_loop(0, (n_chunks + 1) // 2, count_chunk,
                                jnp.zeros((n_rows, LANES), jnp.float32))
            total = jnp.sum(cnt, axis=1, keepdims=True)
            accept = total >= topk
            return jnp.where(accept, cand, thr), jnp.where(accept, total, kept)

        def bit_step(state):
            b, thr, kept, _ = state
            thr, kept = one_bit(b, thr, kept)
            thr, kept = one_bit(b + 1, thr, kept)
            unresolved = jnp.max(jnp.where(kept == topk, 0.0, 1.0))
            return b + 2, thr, kept, unresolved

        def search_on(state):
            b, _, _, unresolved = state
            return jnp.logical_and(b < 32, unresolved > 0.5)

        _, thr, kept, _ = lax.while_loop(
            search_on, bit_step,
            (jnp.int32(0), jnp.full((n_rows, LANES), _INT_MIN, jnp.int32),
             jnp.full((n_rows, LANES), float(2 * s), jnp.float32), jnp.float32(1.0)))

        def tie_bound():
            def count(pred):
                def chunk(c, cnt):
                    col0 = pl.multiple_of(c * tcc, tcc)
                    k = keys_sc[rows, pl.ds(col0, tcc)]
                    for t in range(tcc // LANES):
                        hit = pred(k[:, t * LANES:(t + 1) * LANES], col0 + t * LANES)
                        cnt = cnt + jnp.where(hit, 1.0, 0.0)
                    return cnt
                cnt = lax.fori_loop(0, (n_chunks + 1) // 2, chunk,
                                    jnp.zeros((n_rows, LANES), jnp.float32))
                return jnp.sum(cnt, axis=1, keepdims=True)

            lane = lax.broadcasted_iota(jnp.int32, (n_rows, LANES), 1)
            need = topk - count(lambda k, col: k > thr)

            def col_bit(j, bound):
                cand = bound + lax.shift_left(jnp.int32(1), COL_BITS - 1 - j)
                below = count(lambda k, col: jnp.logical_and(k == thr, col + lane < cand))
                return jnp.where(below <= need, cand, bound)

            return lax.fori_loop(0, COL_BITS, col_bit, jnp.zeros((n_rows, LANES), jnp.int32))

        ties = jnp.max(jnp.where(kept > topk, 1.0, 0.0)) > 0.5
        bound = lax.cond(ties, tie_bound, lambda: jnp.full((n_rows, LANES), s, jnp.int32))
        return thr, bound

    found = [search(slice(g * IDX_SG, (g + 1) * IDX_SG)) for g in range(tq // IDX_SG)]
    thr = jnp.concatenate([f[0] for f in found], axis=0)
    bound = jnp.concatenate([f[1] for f in found], axis=0)

    def bias_chunk(c, carry):
        col0 = pl.multiple_of(c * tc, tc)
        k = keys_sc[:, pl.ds(col0, tc)]
        cols = col0 + col_iota
        thr_w = jnp.concatenate([thr] * n_tiles, axis=1)
        bound_w = jnp.concatenate([bound] * n_tiles, axis=1)
        keep = jnp.logical_or(k > thr_w, jnp.logical_and(k == thr_w, cols < bound_w))
        keep = jnp.logical_and(keep, cols <= row_ids)
        bias_ref[:, pl.ds(col0, tc)] = jnp.where(keep, keep_ref[...], NEG_BIG).astype(bias_ref.dtype)
        return carry

    lax.fori_loop(0, n_chunks, bias_chunk, 0)

    def fill_chunk(c, carry):
        col0 = pl.multiple_of(c * tc, tc)
        bias_ref[:, pl.ds(col0, tc)] = jnp.full((tq, tc), NEG_BIG, bias_ref.dtype)
        return carry

    lax.fori_loop(n_chunks, s // tc, fill_chunk, 0)


def _index_bias(qi, kte, kto, wi, topk, keep_value):
    s = qi.shape[0]
    tq = IDX_TQ
    return pl.pallas_call(
        functools.partial(_index_kernel, topk=topk),
        grid=(s // tq,),
        in_specs=[pl.BlockSpec((tq, IDX_Q_W), lambda i: (i, 0)),
                  pl.BlockSpec((LANES, s), lambda i: (0, 0)),
                  pl.BlockSpec((LANES, s), lambda i: (0, 0)),
                  pl.BlockSpec((tq, LANES), lambda i: (i, 0)),
                  pl.BlockSpec((1, IDX_TC), lambda i: (0, 0))],
        out_specs=pl.BlockSpec((tq, s), lambda i: (i, 0)),
        out_shape=jax.ShapeDtypeStruct((s, s), jnp.bfloat16),
        scratch_shapes=[pltpu.VMEM((tq, s), jnp.int32),
                        pltpu.VMEM((N_IDX_HEADS, tq, LANES), jnp.float32)],
        compiler_params=_cparams(1),
        name="index_topk_bias",
    )(qi, kte, kto, wi, jnp.full((1, IDX_TC), keep_value, jnp.float32))


def _attn_kernel(qt_ref, kt_ref, skip_ref, q_ref, k_ref, v_ref, *rest, fox):
    if fox:
        fq_ref, fk_ref, o_ref, m_sc, l_sc, acc_sc, s_sc, p_sc, alpha_sc = rest
    else:
        bias_ref, o_ref, m_sc, l_sc, acc_sc, s_sc, p_sc, alpha_sc = rest
    p_id = pl.program_id(1)
    qb = qt_ref[p_id]
    kb = kt_ref[p_id]
    tq, tk = q_ref.shape[0], k_ref.shape[0]
    rc = ATT_RC
    qk_scale = (HEAD_DIM ** -0.5) * LOG2E

    @pl.when(kb == qb)
    def _():
        m_sc[...] = jnp.full(m_sc.shape, NEG_BIG, jnp.float32)
        l_sc[...] = jnp.zeros(l_sc.shape, jnp.float32)
        acc_sc[...] = jnp.zeros(acc_sc.shape, jnp.float32)

    if fox:
        key_bias = (fq_ref[:, 0:1] - fk_ref[...]) * LOG2E
        col_minus_row = (lax.broadcasted_iota(jnp.int32, (rc, tk), 1)
                         - lax.broadcasted_iota(jnp.int32, (rc, tk), 0))

    def key_cols(t, diag):
        return (t + 1) * rc if diag else tk

    def logits_stage(t, diag):
        rows = slice(t * rc, (t + 1) * rc)
        kc = key_cols(t, diag)
        s = lax.dot_general(q_ref[rows, :], k_ref[0:kc, :], (((1,), (1,)), ((), ())),
                            preferred_element_type=jnp.float32) * qk_scale
        if fox:
            s = s + key_bias[:, 0:kc]
            if diag:
                s = jnp.where(col_minus_row[:, 0:kc] <= t * rc, s, NEG_BIG)
        else:
            s = s + bias_ref[rows, 0:kc].astype(jnp.float32)
        s_sc[t % 2, :, 0:kc] = s

    def softmax_stage(t, diag):
        rows = slice(t * rc, (t + 1) * rc)
        kc = key_cols(t, diag)
        s = s_sc[t % 2, :, 0:kc]
        m_prev = m_sc[rows, :]
        m_new = jnp.maximum(m_prev, jnp.max(s, axis=1, keepdims=True))
        alpha = jnp.exp2(m_prev - m_new)
        p = jnp.exp2(s - m_new)
        l_sc[rows, :] = alpha * l_sc[rows, :] + jnp.sum(p, axis=1, keepdims=True)
        m_sc[rows, :] = m_new
        alpha_sc[t % 2] = alpha
        p_sc[t % 2, :, 0:kc] = p.astype(p_sc.dtype)

    def value_stage(t, diag):
        rows = slice(t * rc, (t + 1) * rc)
        kc = key_cols(t, diag)
        acc_sc[rows, :] = alpha_sc[t % 2] * acc_sc[rows, :] + jnp.dot(
            p_sc[t % 2, :, 0:kc], v_ref[0:kc, :], preferred_element_type=jnp.float32)

    def all_chunks(diag):
        n_rc = tq // rc
        for t in range(n_rc + 2):
            if t < n_rc:
                logits_stage(t, diag)
            if 1 <= t <= n_rc:
                softmax_stage(t - 1, diag)
            if t >= 2:
                value_stage(t - 2, diag)

    pl.when(kb == qb)(functools.partial(all_chunks, True))
    live = skip_ref[pl.program_id(0), p_id] == 0
    pl.when(jnp.logical_and(kb != qb, live))(functools.partial(all_chunks, False))

    @pl.when(kb == 0)
    def _():
        o_ref[...] = (acc_sc[...] / l_sc[...]).astype(o_ref.dtype)


def _block_pairs(s, t):
    nq = s // t
    return [(a, b) for a in range(nq) for b in range(a, -1, -1)]


def _logit_bound2(g_q, g_k):
    return (HEAD_DIM * jnp.max(jnp.abs(g_q)) * jnp.max(jnp.abs(g_k)) * BF16_NORM_SLACK
            * (HEAD_DIM ** -0.5) * LOG2E)


def _fox_skip_flags(f_t, g_q, g_k):
    s = f_t.shape[1]
    bound2 = _logit_bound2(g_q, g_k)
    pairs = _block_pairs(s, ATT_T)
    qrow = jnp.asarray([a * ATT_T for a, _ in pairs], jnp.int32)
    kcol = jnp.asarray([b * ATT_T + ATT_T - 1 for _, b in pairs], jnp.int32)
    gap2 = (f_t[:, qrow] - f_t[:, kcol]) * LOG2E
    return (gap2 + 2.0 * bound2 < EXP2_ZERO_BELOW).astype(jnp.int32)


def _attention(qk, v, q_col, k_col, v_col, *, frow=None, skip=None, bias=None):
    s = qk.shape[0]
    n_heads = N_FOX_HEADS
    fox = bias is None
    tq = tk = min(ATT_T if fox else ATT_T_DSA, s)
    pairs = _block_pairs(s, tq)
    qt = jnp.asarray([a for a, _ in pairs], jnp.int32)
    kt = jnp.asarray([b for _, b in pairs], jnp.int32)
    if skip is None:
        skip = jnp.zeros((n_heads, len(pairs)), jnp.int32)
    in_specs = [
        pl.BlockSpec((tq, HEAD_DIM), lambda h, p, qt, kt, sk: (qt[p], q_col + h)),
        pl.BlockSpec((tk, HEAD_DIM), lambda h, p, qt, kt, sk: (kt[p], k_col + h)),
        pl.BlockSpec((tk, HEAD_DIM), lambda h, p, qt, kt, sk: (kt[p], v_col + h)),
    ]
    if fox:
        in_specs += [pl.BlockSpec((None, 1, tq), lambda h, p, qt, kt, sk: (h, 0, qt[p])),
                     pl.BlockSpec((None, 1, tk), lambda h, p, qt, kt, sk: (h, 0, kt[p]))]
        operands = (qk, qk, v, frow, frow)
    else:
        in_specs += [pl.BlockSpec((tq, tk), lambda h, p, qt, kt, sk: (qt[p], kt[p]))]
        operands = (qk, qk, v, bias)
    grid_spec = pltpu.PrefetchScalarGridSpec(
        num_scalar_prefetch=3,
        grid=(n_heads, len(pairs)),
        in_specs=in_specs,
        out_specs=pl.BlockSpec((tq, HEAD_DIM), lambda h, p, qt, kt, sk: (qt[p], h)),
        scratch_shapes=[pltpu.VMEM((tq, 1), jnp.float32),
                        pltpu.VMEM((tq, 1), jnp.float32),
                        pltpu.VMEM((tq, HEAD_DIM), jnp.float32),
                        pltpu.VMEM((2, ATT_RC, tk), jnp.float32),
                        pltpu.VMEM((2, ATT_RC, tk), jnp.bfloat16),
                        pltpu.VMEM((2, ATT_RC, 1), jnp.float32)],
    )
    return pl.pallas_call(
        functools.partial(_attn_kernel, fox=fox),
        grid_spec=grid_spec,
        out_shape=jax.ShapeDtypeStruct((s, n_heads * HEAD_DIM), jnp.bfloat16),
        compiler_params=_cparams(2),
        name="fox_attention" if fox else "dsa_attention",
    )(qt, kt, skip, *operands)


def _attn_shift_kernel(qt_ref, kt_ref, skip_ref, q_ref, k_ref, v_ref, *rest, fox):
    if fox:
        fcol_ref, frow_ref, o_ref, acc_sc = rest
    else:
        bias_ref, o_ref, acc_sc = rest
    p_id = pl.program_id(1)
    qb = qt_ref[p_id]
    kb = kt_ref[p_id]
    tq, tk = q_ref.shape[0], k_ref.shape[0]
    rc = ATT_RC
    qk_scale = (HEAD_DIM ** -0.5) * LOG2E

    @pl.when(kb == qb)
    def _():
        acc_sc[...] = jnp.zeros(acc_sc.shape, jnp.float32)

    def chunks(diag, heads):
        if fox:
            col_minus_row = (lax.broadcasted_iota(jnp.int32, (rc, tk), 1)
                             - lax.broadcasted_iota(jnp.int32, (rc, tk), 0))
        for hh in heads:
            lanes = slice(hh * HEAD_DIM, (hh + 1) * HEAD_DIM)
            v = v_ref[:, lanes]
            v1 = jnp.concatenate([v, jnp.ones(v.shape, v.dtype)], axis=1)
            for t in range(tq // rc):
                rows = slice(t * rc, (t + 1) * rc)
                kc = (t + 1) * rc if diag else tk
                s = lax.dot_general(q_ref[rows, lanes], k_ref[0:kc, lanes],
                                    (((1,), (1,)), ((), ())),
                                    preferred_element_type=jnp.float32) * qk_scale
                if fox:
                    s = s + fcol_ref[hh, rows, :] - frow_ref[hh, :, 0:kc]
                    if diag:
                        s = jnp.where(col_minus_row[:, 0:kc] <= t * rc, s, NEG_BIG)
                else:
                    s = s + bias_ref[rows, 0:kc].astype(jnp.float32)
                p = jnp.exp2(s).astype(v.dtype)
                acc_sc[hh, rows, :] += jnp.dot(p, v1[0:kc, :], preferred_element_type=jnp.float32)

    all_heads = tuple(range(ATT_HP))
    pl.when(kb == qb)(functools.partial(chunks, True, all_heads))
    if fox:
        for hh in all_heads:
            live = skip_ref[pl.program_id(0) * ATT_HP + hh, p_id] == 0
            pl.when(jnp.logical_and(kb != qb, live))(functools.partial(chunks, False, (hh,)))
    else:
        pl.when(kb != qb)(functools.partial(chunks, False, all_heads))

    @pl.when(kb == 0)
    def _():
        for hh in all_heads:
            lanes = slice(hh * HEAD_DIM, (hh + 1) * HEAD_DIM)
            o_ref[:, lanes] = (acc_sc[hh, :, 0:HEAD_DIM] / acc_sc[hh, :, HEAD_DIM:]).astype(o_ref.dtype)


def _attention_shift(qk, v, q_col, k_col, v_col, *, fcol=None, frow=None, skip=None, bias=None):
    s = qk.shape[0]
    n_heads = N_FOX_HEADS
    hp = ATT_HP
    assert n_heads % hp == 0 and q_col % hp == 0 and k_col % hp == 0 and v_col % hp == 0
    fox = bias is None
    tq = tk = min(ATT_T if fox else ATT_T_DSA, s)
    pairs = _block_pairs(s, tq)
    qt = jnp.asarray([a for a, _ in pairs], jnp.int32)
    kt = jnp.asarray([b for _, b in pairs], jnp.int32)
    if skip is None:
        skip = jnp.zeros((n_heads, len(pairs)), jnp.int32)
    w = hp * HEAD_DIM
    in_specs = [
        pl.BlockSpec((tq, w), lambda h, p, qt, kt, sk: (qt[p], q_col // hp + h)),
        pl.BlockSpec((tk, w), lambda h, p, qt, kt, sk: (kt[p], k_col // hp + h)),
        pl.BlockSpec((tk, w), lambda h, p, qt, kt, sk: (kt[p], v_col // hp + h)),
    ]
    if fox:
        in_specs += [pl.BlockSpec((hp, tq, 1), lambda h, p, qt, kt, sk: (h, qt[p], 0)),
                     pl.BlockSpec((hp, 1, tk), lambda h, p, qt, kt, sk: (h, 0, kt[p]))]
        operands = (qk, qk, v, fcol, frow)
    else:
        in_specs += [pl.BlockSpec((tq, tk), lambda h, p, qt, kt, sk: (qt[p], kt[p]))]
        operands = (qk, qk, v, bias)
    grid_spec = pltpu.PrefetchScalarGridSpec(
        num_scalar_prefetch=3,
        grid=(n_heads // hp, len(pairs)),
        in_specs=in_specs,
        out_specs=pl.BlockSpec((tq, w), lambda h, p, qt, kt, sk: (qt[p], h)),
        scratch_shapes=[pltpu.VMEM((hp, tq, 2 * HEAD_DIM), jnp.float32)],
    )
    return pl.pallas_call(
        functools.partial(_attn_shift_kernel, fox=fox),
        grid_spec=grid_spec,
        out_shape=jax.ShapeDtypeStruct((s, n_heads * HEAD_DIM), jnp.bfloat16),
        compiler_params=_cparams(2),
        name="fox_attention_shift" if fox else "dsa_attention_shift",
    )(qt, kt, skip, *operands)


def _cast_kernel(x_ref, o_ref):
    o_ref[...] = x_ref[...].astype(o_ref.dtype)


def _cast_bf16(w):
    l, k, n = w.shape
    rows = math.gcd(CAST_ROWS, k)
    spec = pl.BlockSpec((None, rows, n), lambda i, j: (i, j, 0))
    return pl.pallas_call(
        _cast_kernel,
        grid=(l, k // rows),
        in_specs=[spec],
        out_specs=spec,
        out_shape=jax.ShapeDtypeStruct((l, k, n), jnp.bfloat16),
        compiler_params=_cparams(2),
        name="cast_bf16",
    )(w)


def _rope_angles(s, rot):
    half = rot // 2
    pos = jnp.arange(s, dtype=jnp.float32)
    inv = jnp.power(jnp.float32(ROPE_THETA), -2.0 * jnp.arange(half, dtype=jnp.float32) / rot)
    ang = pos[:, None] * inv[None, :]
    return jnp.cos(ang), jnp.sin(ang)


def _rope_tables(s, rot, period):
    half = rot // 2
    cos, sin = _rope_angles(s, rot)
    ones = jnp.ones((s, period - rot), jnp.float32)
    zeros = lambda n: jnp.zeros((s, n), jnp.float32)
    c = jnp.concatenate([cos, cos, ones], axis=1)
    s_up = jnp.concatenate([-sin, zeros(period - half)], axis=1)
    s_dn = jnp.concatenate([zeros(half), sin, zeros(period - rot)], axis=1)
    reps = LANES // period
    return tuple(jnp.tile(t, (1, reps)) for t in (c, s_up, s_dn))


def _pad_row(v, n):
    return jnp.pad(v, (0, n - v.shape[0])).reshape(1, n)


def kernel(x, p, g_ffn1, w1_gate, w1_up, w1_down, g_mix, w_in, b_f, g_qa, g_ka, g_qb, g_kb, g_ik,
           w_br_fox, w_br_dsa, w_o, g_ffn2, w2_gate, w2_up, w2_down, g_ple, w_ple_gate, w_ple_proj):
    b, s, d = x.shape
    assert b == 1
    bf16 = jnp.bfloat16
    topk = min(INDEX_TOPK_MAX, s // 4)
    idx_scale = (N_IDX_HEADS ** -0.5) * (IDX_DIM ** -0.5)
    rope_b = _rope_tables(s, HEAD_DIM // ROT_FRAC_DIV, HEAD_DIM)
    half_b = HEAD_DIM // ROT_FRAC_DIV // 2
    rope_i = _rope_tables(s, IDX_DIM // ROT_FRAC_DIV, IDX_DIM)
    half_i = IDX_DIM // ROT_FRAC_DIV // 2
    row_tab = lambda t: (t, (MM_TM, LANES), lambda i, j: (i, 0))
    res_tile = lambda h, tn=MM_TN: (h, (MM_TM, tn), lambda i, j: (i, j))
    blk = lambda cols: cols // MM_TN_WIDE
    tn_res = min(MM_TN_WIDE, d)

    w1g, w1u, w1d = _cast_bf16(w1_gate), _cast_bf16(w1_up), _cast_bf16(w1_down)
    w2g, w2u, w2d = _cast_bf16(w2_gate), _cast_bf16(w2_up), _cast_bf16(w2_down)
    wbf, wbd, wo = _cast_bf16(w_br_fox), _cast_bf16(w_br_dsa), _cast_bf16(w_o)
    wpg, wpp = _cast_bf16(w_ple_gate), _cast_bf16(w_ple_proj)
    off_fa = 3 * FOX_W
    off_b = off_fa + N_FOX_HEADS
    off_ki = off_b + 3 * DSA_W + IDX_Q_W
    off_wi = off_ki + IDX_DIM
    off_g = off_wi + N_IDX_HEADS
    w_in_bf = w_in.astype(bf16)
    w_a = w_in_bf
    w_b = w_in_bf[:, :, off_b:off_ki]
    w_g = w_in_bf[:, :, off_g:]
    assert off_fa % LANES == 0 and off_g <= (off_ki // LANES + 1) * LANES
    kw_window = off_ki // LANES
    ki_lane, wi_lane = off_ki - kw_window * LANES, off_wi - kw_window * LANES
    p_bf = p.reshape(DEPTH, s, PLE_DIM).astype(bf16)

    def ffn(h, g, wg, wu, wd, layer, tag):
        u = _rmsnorm(h, g)
        a = _matmul([u], [wg, wu], [(0, 0), (0, 1)], [], _ep_swiglu, bf16,
                    layer=layer, name=tag + "_swiglu")
        return _matmul([a], [wd], [(0, 0)], [res_tile(h, tn_res)],
                       functools.partial(_ep_residual, scale=0.5), jnp.float32,
                       layer=layer, name=tag + "_down", tn=tn_res)

    h = x.reshape(s, d)
    for i in range(DEPTH):
        h = ffn(h, g_ffn1[i], w1g, w1u, w1d, i, "ffn1")

        u = _rmsnorm(h, g_mix[i])
        gain_a = jnp.concatenate([jnp.tile(g_qa[i], N_FOX_HEADS), jnp.tile(g_ka[i], N_FOX_HEADS)])
        gain_b = jnp.concatenate([jnp.tile(g_qb[i], N_DSA_HEADS), jnp.tile(g_kb[i], N_DSA_HEADS)])
        col_vec = lambda v: (v.reshape(1, -1), (1, MM_TN_WIDE), lambda i, j: (0, j))

        qk_a = _matmul([u], [w_a], [(0, 0)], [col_vec(gain_a)],
                       functools.partial(_ep_headnorm, rope_half=0), bf16,
                       layer=i, name="proj_qk_fox", tn=MM_TN_WIDE, n=2 * FOX_W)
        v_a = _matmul([u], [w_a], [(0, 0)], [], _ep_cast, bf16,
                      layer=i, name="proj_v_fox", tn=MM_TN_WIDE, n=FOX_W, rhs_off=[blk(2 * FOX_W)])
        qk_b = _matmul([u], [w_b], [(0, 0)], [col_vec(gain_b)] + [row_tab(t) for t in rope_b],
                       functools.partial(_ep_headnorm, rope_half=half_b), bf16,
                       layer=i, name="proj_qk_dsa", tn=MM_TN_WIDE, n=2 * DSA_W)
        v_b = _matmul([u], [w_b], [(0, 0)], [], _ep_cast, bf16,
                      layer=i, name="proj_v_dsa", tn=MM_TN_WIDE, n=DSA_W, rhs_off=[blk(2 * DSA_W)])
        q_i = _matmul([u], [w_b], [(0, 0)], [row_tab(t) for t in rope_i],
                      functools.partial(_ep_rope, rope_half=half_i), bf16,
                      layer=i, name="proj_qidx", tn=MM_TN_WIDE, n=IDX_Q_W, rhs_off=[blk(3 * DSA_W)])
        gates = _matmul([u], [w_g], [(0, 0)], [], _ep_sigmoid, bf16, layer=i, name="proj_gates", tn=MM_TN_WIDE)
        z_f = _matmul([u], [w_in_bf], [(0, 0)], [], _ep_cast, jnp.float32, layer=i,
                      name="proj_forget", tn=LANES, n=LANES, rhs_off=[off_fa // LANES])
        z_kw = _matmul([u], [w_in_bf], [(0, 0)], [], _ep_cast, jnp.float32, layer=i,
                       name="proj_idx_key", tn=LANES, n=LANES, rhs_off=[kw_window])

        f_cum, ki_e, ki_o, wi_s = _small(z_f, z_kw, _pad_row(b_f[i], LANES),
                                         _pad_row(g_ik[i], LANES), *rope_i, idx_scale,
                                         ki_lane, wi_lane)

        f_t = f_cum[:, :N_FOX_HEADS].T
        skip = _fox_skip_flags(f_t, g_qa[i], g_ka[i])
        shift_a = _logit_bound2(g_qa[i], g_ka[i])
        o_a = lax.cond(
            2.0 * shift_a < FIXED_SHIFT_MAX_RANGE,
            lambda: _attention_shift(qk_a, v_a, 0, N_FOX_HEADS, 0,
                                     fcol=(f_t * LOG2E - shift_a)[:, :, None],
                                     frow=(f_t * LOG2E)[:, None, :], skip=skip),
            lambda: _attention(qk_a, v_a, 0, N_FOX_HEADS, 0, frow=f_t[:, None, :], skip=skip))

        shift_b = _logit_bound2(g_qb[i], g_kb[i])
        fixed_b = 2.0 * shift_b < FIXED_SHIFT_MAX_RANGE
        bias = _index_bias(q_i, ki_e.T, ki_o.T, wi_s, topk, jnp.where(fixed_b, -shift_b, 0.0))
        o_b = lax.cond(
            fixed_b,
            lambda: _attention_shift(qk_b, v_b, 0, N_DSA_HEADS, 0, bias=bias),
            lambda: _attention(qk_b, v_b, 0, N_DSA_HEADS, 0, bias=bias))

        n_gate_blocks = D_MODEL // tn_res
        y = _matmul([o_a, o_b], [wbf, wbd], [(0, 0), (1, 1)],
                    [(gates, (MM_TM, tn_res), lambda i, j: (i, j)),
                     (gates, (MM_TM, tn_res), lambda i, j: (i, j + n_gate_blocks))],
                    _ep_merge, bf16, layer=i, name="branch_merge", tn=tn_res)
        h = _matmul([y], [wo], [(0, 0)], [res_tile(h, tn_res)],
                    functools.partial(_ep_residual, scale=1.0), jnp.float32,
                    layer=i, name="out_proj", tn=tn_res)

        h = ffn(h, g_ffn2[i], w2g, w2u, w2d, i, "ffn2")

        u = _rmsnorm(h, g_ple[i])
        h = _matmul([u, p_bf[i]], [wpg, wpp], [(0, 0), (1, 1)], [res_tile(h)],
                    _ep_ple, jnp.float32, layer=i, name="ple")
    return h.reshape(b, s, d)
```

```python
import functools
import math

import jax
import jax.numpy as jnp
from jax import lax
from jax.experimental import pallas as pl
from jax.experimental.pallas import tpu as pltpu

D_MODEL = 4096
SEQ = 8192
DEPTH = 4
HEAD_DIM = 128
N_FOX_HEADS = 16
N_DSA_HEADS = 16
N_IDX_HEADS = 32
IDX_DIM = 64
INDEX_TOPK_MAX = 256
D_FF = 3584
PLE_DIM = 256
ROPE_THETA = 500000.0
ROT_FRAC_DIV = 4
NORM_EPS = 1e-6
NEG_BIG = -1e30
LOG2E = 1.4426950408889634
EXP2_ZERO_BELOW = -160.0
FIXED_SHIFT_MAX_RANGE = 100.0
BF16_NORM_SLACK = 1.02
FOX_W = N_FOX_HEADS * HEAD_DIM
DSA_W = N_DSA_HEADS * HEAD_DIM
IDX_Q_W = N_IDX_HEADS * IDX_DIM

LANES = 128
VMEM_LIMIT = 56 * 1024 * 1024

MM_TM = 1024
MM_TN = 512
MM_TN_WIDE = 1024
NORM_TM = 512
CAST_ROWS = 512
ATT_T = 1024
ATT_T_DSA = 2048
ATT_RC = 256
ATT_HP = 4
IDX_TQ = 256
IDX_TC = 512
IDX_SG = 128

_INT_MIN = -(2 ** 31)
COL_BITS = 14


def _cparams(n_axes):
    return pltpu.CompilerParams(
        dimension_semantics=("arbitrary",) * n_axes,
        vmem_limit_bytes=VMEM_LIMIT,
    )


def _rmsnorm_kernel(x_ref, g_ref, o_ref):
    x = x_ref[...]
    ms = jnp.mean(x * x, axis=-1, keepdims=True)
    o_ref[...] = (x * lax.rsqrt(ms + NORM_EPS) * g_ref[...]).astype(o_ref.dtype)


def _rmsnorm(x, g):
    m, d = x.shape
    return pl.pallas_call(
        _rmsnorm_kernel,
        grid=(m // NORM_TM,),
        in_specs=[pl.BlockSpec((NORM_TM, d), lambda i: (i, 0)),
                  pl.BlockSpec((1, d), lambda i: (0, 0))],
        out_specs=pl.BlockSpec((NORM_TM, d), lambda i: (i, 0)),
        out_shape=jax.ShapeDtypeStruct((m, d), jnp.bfloat16),
        compiler_params=_cparams(1),
        name="rmsnorm",
    )(x, g.reshape(1, d))


def _mm_kernel(*refs, n_lhs, n_rhs, pairs, n_extra, epilogue):
    lhs = refs[:n_lhs]
    rhs = refs[n_lhs:n_lhs + n_rhs]
    extra = refs[n_lhs + n_rhs:n_lhs + n_rhs + n_extra]
    out = refs[n_lhs + n_rhs + n_extra]
    accs = [jnp.dot(lhs[a][...], rhs[b][...], preferred_element_type=jnp.float32)
            for a, b in pairs]
    epilogue(accs, extra, out)


def _matmul(lhs, rhs, pairs, extras, epilogue, out_dtype, *, layer, name, tn=MM_TN, n=None,
            rhs_off=None, lhs_single_buffer=False):
    m = lhs[0].shape[0]
    n = rhs[0].shape[2] if n is None else n
    rhs_off = [0] * len(rhs) if rhs_off is None else rhs_off
    tm = MM_TM
    in_specs = []
    for a in lhs:
        mode = dict(pipeline_mode=pl.Buffered(1)) if lhs_single_buffer else {}
        in_specs.append(pl.BlockSpec((tm, a.shape[1]), lambda i, j: (i, 0), **mode))
    for w, off in zip(rhs, rhs_off):
        in_specs.append(pl.BlockSpec((None, w.shape[1], tn),
                                     lambda i, j, off=off: (layer, 0, j + off)))
    for _, bs, im in extras:
        in_specs.append(pl.BlockSpec(bs, im))
    kern = functools.partial(_mm_kernel, n_lhs=len(lhs), n_rhs=len(rhs), pairs=pairs,
                             n_extra=len(extras), epilogue=epilogue)
    return pl.pallas_call(
        kern,
        grid=(m // tm, n // tn),
        in_specs=in_specs,
        out_specs=pl.BlockSpec((tm, tn), lambda i, j: (i, j)),
        out_shape=jax.ShapeDtypeStruct((m, n), out_dtype),
        compiler_params=_cparams(2),
        name=name,
    )(*lhs, *rhs, *[e[0] for e in extras])


def _ep_cast(accs, extra, out):
    out[...] = accs[0].astype(out.dtype)


def _ep_sigmoid(accs, extra, out):
    out[...] = jax.nn.sigmoid(accs[0]).astype(out.dtype)


def _ep_swiglu(accs, extra, out):
    out[...] = (jax.nn.silu(accs[0]) * accs[1]).astype(out.dtype)


def _ep_residual(accs, extra, out, *, scale):
    out[...] = extra[0][...] + scale * accs[0]


def _ep_merge(accs, extra, out):
    ga = extra[0][...].astype(jnp.float32)
    gb = extra[1][...].astype(jnp.float32)
    out[...] = (ga * accs[0] + gb * accs[1]).astype(out.dtype)


def _ep_ple(accs, extra, out):
    out[...] = extra[0][...] + jax.nn.sigmoid(accs[0]) * accs[1]


def _rope_lanes(y, cos, s_up, s_dn, half):
    up = pltpu.roll(y, LANES - half, 1)
    dn = pltpu.roll(y, half, 1)
    return y * cos + up * s_up + dn * s_dn


def _ep_headnorm(accs, extra, out, *, rope_half):
    acc = accs[0]
    gain = extra[0][...]
    groups = [slice(g * LANES, (g + 1) * LANES) for g in range(acc.shape[1] // LANES)]
    normed = []
    for sl in groups:
        x = acc[:, sl]
        ms = jnp.mean(x * x, axis=-1, keepdims=True)
        normed.append(x * lax.rsqrt(ms + NORM_EPS) * gain[:, sl])
    for sl, y in zip(groups, normed):
        if rope_half:
            y = _rope_lanes(y, extra[1][...], extra[2][...], extra[3][...], rope_half)
        out[:, sl] = y.astype(out.dtype)


def _ep_rope(accs, extra, out, *, rope_half):
    acc = accs[0]
    for g in range(acc.shape[1] // LANES):
        sl = slice(g * LANES, (g + 1) * LANES)
        y = _rope_lanes(acc[:, sl], extra[0][...], extra[1][...], extra[2][...], rope_half)
        out[:, sl] = y.astype(out.dtype)


def _small_kernel(zf_ref, zkw_ref, bf_ref, gik_ref, cos_ref, sup_ref, sdn_ref,
                  f_ref, kie_ref, kio_ref, wi_ref, carry_sc, *, idx_scale, ki_lane, wi_lane):
    s = zf_ref.shape[0]
    lane = lax.broadcasted_iota(jnp.int32, (s, LANES), 1)

    @pl.when(pl.program_id(0) == 0)
    def _():
        carry_sc[...] = jnp.zeros(carry_sc.shape, jnp.float32)

    x = jnp.where(lane < N_FOX_HEADS, zf_ref[...], 0.0) + bf_ref[...]
    logf = jnp.minimum(x, 0.0) - jnp.log1p(jnp.exp(-jnp.abs(x)))
    row = lax.broadcasted_iota(jnp.int32, (s, LANES), 0)
    d = 1
    while d < s:
        logf = logf + jnp.where(row >= d, pltpu.roll(logf, d, 0), 0.0)
        d *= 2
    f_ref[...] = logf + carry_sc[0:1, :]
    carry_sc[0:1, :] = f_ref[s - 1:s, :]
    zkw = zkw_ref[...]
    k = jnp.where(lane < IDX_DIM, pltpu.roll(zkw, (LANES - ki_lane) % LANES, 1), 0.0)
    ms = jnp.sum(k * k, axis=-1, keepdims=True) * (1.0 / IDX_DIM)
    k = k * lax.rsqrt(ms + NORM_EPS) * gik_ref[...]
    k = _rope_lanes(k, cos_ref[...], sup_ref[...], sdn_ref[...], IDX_DIM // ROT_FRAC_DIV // 2)
    k = jnp.where(lane < IDX_DIM, k, 0.0)
    kie_ref[...] = k.astype(kie_ref.dtype)
    kio_ref[...] = pltpu.roll(k, IDX_DIM, 1).astype(kio_ref.dtype)
    wi = pltpu.roll(zkw, (LANES - wi_lane) % LANES, 1)
    wi_ref[...] = jnp.where(lane < N_IDX_HEADS, wi, 0.0) * idx_scale


def _small(zf, zkw, bf, gik, cos, sup, sdn, idx_scale, ki_lane, wi_lane):
    s = zf.shape[0]
    tm = MM_TM
    rows = lambda w: pl.BlockSpec((tm, w), lambda i: (i, 0))
    vec = pl.BlockSpec((1, LANES), lambda i: (0, 0))
    return pl.pallas_call(
        functools.partial(_small_kernel, idx_scale=idx_scale, ki_lane=ki_lane, wi_lane=wi_lane),
        grid=(s // tm,),
        in_specs=[rows(LANES), rows(LANES), vec, vec, rows(LANES), rows(LANES), rows(LANES)],
        out_specs=[rows(LANES)] * 4,
        out_shape=[jax.ShapeDtypeStruct((s, LANES), jnp.float32),
                   jax.ShapeDtypeStruct((s, LANES), jnp.bfloat16),
                   jax.ShapeDtypeStruct((s, LANES), jnp.bfloat16),
                   jax.ShapeDtypeStruct((s, LANES), jnp.float32)],
        scratch_shapes=[pltpu.VMEM((8, LANES), jnp.float32)],
        compiler_params=_cparams(1),
        name="small_heads",
    )(zf, zkw, bf, gik, cos, sup, sdn)


def _sortable(x):
    b = pltpu.bitcast(x, jnp.int32)
    return jnp.where(b < 0, b ^ jnp.int32(0x7FFFFFFF), b)


def _index_kernel(qi_ref, kte_ref, kto_ref, wi_ref, keep_ref, bias_ref, keys_sc, wb_sc, *, topk):
    tq, s = bias_ref.shape
    tc = IDX_TC
    i = pl.program_id(0)
    row0 = i * tq
    n_chunks = (row0 + tq + tc - 1) // tc
    n_tiles = tc // LANES

    for j in range(N_IDX_HEADS):
        wb_sc[j] = jnp.broadcast_to(wi_ref[:, j:j + 1], (tq, LANES))

    row_ids = row0 + lax.broadcasted_iota(jnp.int32, (tq, tc), 0)
    col_iota = lax.broadcasted_iota(jnp.int32, (tq, tc), 1)

    def score_chunk(c, carry):
        col0 = pl.multiple_of(c * tc, tc)
        kte = kte_ref[:, pl.ds(col0, tc)]
        kto = kto_ref[:, pl.ds(col0, tc)]
        score = jnp.zeros((tq, tc), jnp.float32)
        for p in range(N_IDX_HEADS // 2):
            qp = qi_ref[:, p * LANES:(p + 1) * LANES]
            re = jnp.maximum(jnp.dot(qp, kte, preferred_element_type=jnp.float32), 0.0)
            ro = jnp.maximum(jnp.dot(qp, kto, preferred_element_type=jnp.float32), 0.0)
            we = jnp.concatenate([wb_sc[2 * p]] * n_tiles, axis=1)
            wo = jnp.concatenate([wb_sc[2 * p + 1]] * n_tiles, axis=1)
            score = score + re * we + ro * wo
        causal = (col0 + col_iota) <= row_ids
        score = jnp.where(score == 0.0, 0.0, score)
        score = jnp.where(causal, score, NEG_BIG)
        keys_sc[:, pl.ds(col0, tc)] = _sortable(score)
        return carry

    lax.fori_loop(0, n_chunks, score_chunk, 0)

    @pl.when(n_chunks % 2 == 1)
    def _():
        col0 = pl.multiple_of(n_chunks * tc, tc)
        keys_sc[:, pl.ds(col0, tc)] = _sortable(jnp.full((tq, tc), NEG_BIG, jnp.float32))

    tcc = 2 * tc

    def search(rows):
        n_rows = rows.stop - rows.start

        def one_bit(b, thr, kept):
            cand = thr + lax.shift_left(jnp.int32(1), 31 - b)

            def count_chunk(c, cnt):
                col0 = pl.multiple_of(c * tcc, tcc)
                k = keys_sc[rows, pl.ds(col0, tcc)]
                for t in range(tcc // LANES):
                    ge = k[:, t * LANES:(t + 1) * LANES] >= cand
                    cnt = cnt + jnp.where(ge, 1.0, 0.0)
                return cnt

            cnt = lax.fori_loop(0, (n_chunks + 1) // 2, count_chunk,
                                jnp.zeros((n_rows, LANES), jnp.float32))
            total = jnp.sum(cnt, axis=1, keepdims=True)
            accept = total >= topk
            return jnp.where(accept, cand, thr), jnp.where(accept, total, kept)

        def bit_step(state):
            b, thr, kept, _ = state
            thr, kept = one_bit(b, thr, kept)
            thr, kept = one_bit(b + 1, thr, kept)
            unresolved = jnp.max(jnp.where(kept == topk, 0.0, 1.0))
            return b + 2, thr, kept, unresolved

        def search_on(state):
            b, _, _, unresolved = state
            return jnp.logical_and(b < 32, unresolved > 0.5)

        _, thr, kept, _ = lax.while_loop(
            search_on, bit_step,
            (jnp.int32(0), jnp.full((n_rows, LANES), _INT_MIN, jnp.int32),
             jnp.full((n_rows, LANES), float(2 * s), jnp.float32), jnp.float32(1.0)))

        def tie_bound():
            def count(pred):
                def chunk(c, cnt):
                    col0 = pl.multiple_of(c * tcc, tcc)
                    k = keys_sc[rows, pl.ds(col0, tcc)]
                    for t in range(tcc // LANES):
                        hit = pred(k[:, t * LANES:(t + 1) * LANES], col0 + t * LANES)
                        cnt = cnt + jnp.where(hit, 1.0, 0.0)
                    return cnt
                cnt = lax.fori_loop(0, (n_chunks + 1) // 2, chunk,
                                    jnp.zeros((n_rows, LANES), jnp.float32))
                return jnp.sum(cnt, axis=1, keepdims=True)

            lane = lax.broadcasted_iota(jnp.int32, (n_rows, LANES), 1)
            need = topk - count(lambda k, col: k > thr)

            def col_bit(j, bound):
                cand = bound + lax.shift_left(jnp.int32(1), COL_BITS - 1 - j)
                below = count(lambda k, col: jnp.logical_and(k == thr, col + lane < cand))
                return jnp.where(below <= need, cand, bound)

            return lax.fori_loop(0, COL_BITS, col_bit, jnp.zeros((n_rows, LANES), jnp.int32))

        ties = jnp.max(jnp.where(kept > topk, 1.0, 0.0)) > 0.5
        bound = lax.cond(ties, tie_bound, lambda: jnp.full((n_rows, LANES), s, jnp.int32))
        return thr, bound

    found = [search(slice(g * IDX_SG, (g + 1) * IDX_SG)) for g in range(tq // IDX_SG)]
    thr = jnp.concatenate([f[0] for f in found], axis=0)
    bound = jnp.concatenate([f[1] for f in found], axis=0)

    def bias_chunk(c, carry):
        col0 = pl.multiple_of(c * tc, tc)
        k = keys_sc[:, pl.ds(col0, tc)]
        cols = col0 + col_iota
        thr_w = jnp.concatenate([thr] * n_tiles, axis=1)
        bound_w = jnp.concatenate([bound] * n_tiles, axis=1)
        keep = jnp.logical_or(k > thr_w, jnp.logical_and(k == thr_w, cols < bound_w))
        keep = jnp.logical_and(keep, cols <= row_ids)
        bias_ref[:, pl.ds(col0, tc)] = jnp.where(keep, keep_ref[...], NEG_BIG).astype(bias_ref.dtype)
        return carry

    lax.fori_loop(0, n_chunks, bias_chunk, 0)

    def fill_chunk(c, carry):
        col0 = pl.multiple_of(c * tc, tc)
        bias_ref[:, pl.ds(col0, tc)] = jnp.full((tq, tc), NEG_BIG, bias_ref.dtype)
        return carry

    lax.fori_loop(n_chunks, s // tc, fill_chunk, 0)


def _index_bias(qi, kte, kto, wi, topk, keep_value):
    s = qi.shape[0]
    tq = IDX_TQ
    return pl.pallas_call(
        functools.partial(_index_kernel, topk=topk),
        grid=(s // tq,),
        in_specs=[pl.BlockSpec((tq, IDX_Q_W), lambda i: (i, 0)),
                  pl.BlockSpec((LANES, s), lambda i: (0, 0)),
                  pl.BlockSpec((LANES, s), lambda i: (0, 0)),
                  pl.BlockSpec((tq, LANES), lambda i: (i, 0)),
                  pl.BlockSpec((1, IDX_TC), lambda i: (0, 0))],
        out_specs=pl.BlockSpec((tq, s), lambda i: (i, 0)),
        out_shape=jax.ShapeDtypeStruct((s, s), jnp.bfloat16),
        scratch_shapes=[pltpu.VMEM((tq, s), jnp.int32),
                        pltpu.VMEM((N_IDX_HEADS, tq, LANES), jnp.float32)],
        compiler_params=_cparams(1),
        name="index_topk_bias",
    )(qi, kte, kto, wi, jnp.full((1, IDX_TC), keep_value, jnp.float32))


def _attn_kernel(qt_ref, kt_ref, skip_ref, q_ref, k_ref, v_ref, *rest, fox):
    if fox:
        fq_ref, fk_ref, o_ref, m_sc, l_sc, acc_sc, s_sc, p_sc, alpha_sc = rest
    else:
        bias_ref, o_ref, m_sc, l_sc, acc_sc, s_sc, p_sc, alpha_sc = rest
    p_id = pl.program_id(1)
    qb = qt_ref[p_id]
    kb = kt_ref[p_id]
    tq, tk = q_ref.shape[0], k_ref.shape[0]
    rc = ATT_RC
    qk_scale = (HEAD_DIM ** -0.5) * LOG2E

    @pl.when(kb == qb)
    def _():
        m_sc[...] = jnp.full(m_sc.shape, NEG_BIG, jnp.float32)
        l_sc[...] = jnp.zeros(l_sc.shape, jnp.float32)
        acc_sc[...] = jnp.zeros(acc_sc.shape, jnp.float32)

    if fox:
        key_bias = (fq_ref[:, 0:1] - fk_ref[...]) * LOG2E
        col_minus_row = (lax.broadcasted_iota(jnp.int32, (rc, tk), 1)
                         - lax.broadcasted_iota(jnp.int32, (rc, tk), 0))

    def key_cols(t, diag):
        return (t + 1) * rc if diag else tk

    def logits_stage(t, diag):
        rows = slice(t * rc, (t + 1) * rc)
        kc = key_cols(t, diag)
        s = lax.dot_general(q_ref[rows, :], k_ref[0:kc, :], (((1,), (1,)), ((), ())),
                            preferred_element_type=jnp.float32) * qk_scale
        if fox:
            s = s + key_bias[:, 0:kc]
            if diag:
                s = jnp.where(col_minus_row[:, 0:kc] <= t * rc, s, NEG_BIG)
        else:
            s = s + bias_ref[rows, 0:kc].astype(jnp.float32)
        s_sc[t % 2, :, 0:kc] = s

    def softmax_stage(t, diag):
        rows = slice(t * rc, (t + 1) * rc)
        kc = key_cols(t, diag)
        s = s_sc[t % 2, :, 0:kc]
        m_prev = m_sc[rows, :]
        m_new = jnp.maximum(m_prev, jnp.max(s, axis=1, keepdims=True))
        alpha = jnp.exp2(m_prev - m_new)
        p = jnp.exp2(s - m_new)
        l_sc[rows, :] = alpha * l_sc[rows, :] + jnp.sum(p, axis=1, keepdims=True)
        m_sc[rows, :] = m_new
        alpha_sc[t % 2] = alpha
        p_sc[t % 2, :, 0:kc] = p.astype(p_sc.dtype)

    def value_stage(t, diag):
        rows = slice(t * rc, (t + 1) * rc)
        kc = key_cols(t, diag)
        acc_sc[rows, :] = alpha_sc[t % 2] * acc_sc[rows, :] + jnp.dot(
            p_sc[t % 2, :, 0:kc], v_ref[0:kc, :], preferred_element_type=jnp.float32)

    def all_chunks(diag):
        n_rc = tq // rc
        for t in range(n_rc + 2):
            if t < n_rc:
                logits_stage(t, diag)
            if 1 <= t <= n_rc:
                softmax_stage(t - 1, diag)
            if t >= 2:
                value_stage(t - 2, diag)

    pl.when(kb == qb)(functools.partial(all_chunks, True))
    live = skip_ref[pl.program_id(0), p_id] == 0
    pl.when(jnp.logical_and(kb != qb, live))(functools.partial(all_chunks, False))

    @pl.when(kb == 0)
    def _():
        o_ref[...] = (acc_sc[...] / l_sc[...]).astype(o_ref.dtype)


def _block_pairs(s, t):
    nq = s // t
    return [(a, b) for a in range(nq) for b in range(a, -1, -1)]


def _logit_bound2(g_q, g_k):
    return (HEAD_DIM * jnp.max(jnp.abs(g_q)) * jnp.max(jnp.abs(g_k)) * BF16_NORM_SLACK
            * (HEAD_DIM ** -0.5) * LOG2E)


def _fox_skip_flags(f_t, g_q, g_k):
    s = f_t.shape[1]
    bound2 = _logit_bound2(g_q, g_k)
    pairs = _block_pairs(s, ATT_T)
    qrow = jnp.asarray([a * ATT_T for a, _ in pairs], jnp.int32)
    kcol = jnp.asarray([b * ATT_T + ATT_T - 1 for _, b in pairs], jnp.int32)
    gap2 = (f_t[:, qrow] - f_t[:, kcol]) * LOG2E
    return (gap2 + 2.0 * bound2 < EXP2_ZERO_BELOW).astype(jnp.int32)


def _attention(qk, v, q_col, k_col, v_col, *, frow=None, skip=None, bias=None):
    s = qk.shape[0]
    n_heads = N_FOX_HEADS
    fox = bias is None
    tq = tk = min(ATT_T if fox else ATT_T_DSA, s)
    pairs = _block_pairs(s, tq)
    qt = jnp.asarray([a for a, _ in pairs], jnp.int32)
    kt = jnp.asarray([b for _, b in pairs], jnp.int32)
    if skip is None:
        skip = jnp.zeros((n_heads, len(pairs)), jnp.int32)
    in_specs = [
        pl.BlockSpec((tq, HEAD_DIM), lambda h, p, qt, kt, sk: (qt[p], q_col + h)),
        pl.BlockSpec((tk, HEAD_DIM), lambda h, p, qt, kt, sk: (kt[p], k_col + h)),
        pl.BlockSpec((tk, HEAD_DIM), lambda h, p, qt, kt, sk: (kt[p], v_col + h)),
    ]
    if fox:
        in_specs += [pl.BlockSpec((None, 1, tq), lambda h, p, qt, kt, sk: (h, 0, qt[p])),
                     pl.BlockSpec((None, 1, tk), lambda h, p, qt, kt, sk: (h, 0, kt[p]))]
        operands = (qk, qk, v, frow, frow)
    else:
        in_specs += [pl.BlockSpec((tq, tk), lambda h, p, qt, kt, sk: (qt[p], kt[p]))]
        operands = (qk, qk, v, bias)
    grid_spec = pltpu.PrefetchScalarGridSpec(
        num_scalar_prefetch=3,
        grid=(n_heads, len(pairs)),
        in_specs=in_specs,
        out_specs=pl.BlockSpec((tq, HEAD_DIM), lambda h, p, qt, kt, sk: (qt[p], h)),
        scratch_shapes=[pltpu.VMEM((tq, 1), jnp.float32),
                        pltpu.VMEM((tq, 1), jnp.float32),
                        pltpu.VMEM((tq, HEAD_DIM), jnp.float32),
                        pltpu.VMEM((2, ATT_RC, tk), jnp.float32),
                        pltpu.VMEM((2, ATT_RC, tk), jnp.bfloat16),
                        pltpu.VMEM((2, ATT_RC, 1), jnp.float32)],
    )
    return pl.pallas_call(
        functools.partial(_attn_kernel, fox=fox),
        grid_spec=grid_spec,
        out_shape=jax.ShapeDtypeStruct((s, n_heads * HEAD_DIM), jnp.bfloat16),
        compiler_params=_cparams(2),
        name="fox_attention" if fox else "dsa_attention",
    )(qt, kt, skip, *operands)


def _attn_shift_kernel(qt_ref, kt_ref, skip_ref, q_ref, k_ref, v_ref, *rest, fox):
    if fox:
        fcol_ref, frow_ref, o_ref, acc_sc = rest
    else:
        bias_ref, o_ref, acc_sc = rest
    p_id = pl.program_id(1)
    qb = qt_ref[p_id]
    kb = kt_ref[p_id]
    tq, tk = q_ref.shape[0], k_ref.shape[0]
    rc = ATT_RC
    qk_scale = (HEAD_DIM ** -0.5) * LOG2E

    @pl.when(kb == qb)
    def _():
        acc_sc[...] = jnp.zeros(acc_sc.shape, jnp.float32)

    def chunks(diag, heads):
        if fox:
            col_minus_row = (lax.broadcasted_iota(jnp.int32, (rc, tk), 1)
                             - lax.broadcasted_iota(jnp.int32, (rc, tk), 0))
        for hh in heads:
            lanes = slice(hh * HEAD_DIM, (hh + 1) * HEAD_DIM)
            v = v_ref[:, lanes]
            v1 = jnp.concatenate([v, jnp.ones(v.shape, v.dtype)], axis=1)
            for t in range(tq // rc):
                rows = slice(t * rc, (t + 1) * rc)
                kc = (t + 1) * rc if diag else tk
                s = lax.dot_general(q_ref[rows, lanes], k_ref[0:kc, lanes],
                                    (((1,), (1,)), ((), ())),
                                    preferred_element_type=jnp.float32) * qk_scale
                if fox:
                    s = s + fcol_ref[hh, rows, :] - frow_ref[hh, :, 0:kc]
                    if diag:
                        s = jnp.where(col_minus_row[:, 0:kc] <= t * rc, s, NEG_BIG)
                else:
                    s = s + bias_ref[rows, 0:kc].astype(jnp.float32)
                p = jnp.exp2(s).astype(v.dtype)
                acc_sc[hh, rows, :] += jnp.dot(p, v1[0:kc, :], preferred_element_type=jnp.float32)

    all_heads = tuple(range(ATT_HP))
    pl.when(kb == qb)(functools.partial(chunks, True, all_heads))
    if fox:
        for hh in all_heads:
            live = skip_ref[pl.program_id(0) * ATT_HP + hh, p_id] == 0
            pl.when(jnp.logical_and(kb != qb, live))(functools.partial(chunks, False, (hh,)))
    else:
        pl.when(kb != qb)(functools.partial(chunks, False, all_heads))

    @pl.when(kb == 0)
    def _():
        for hh in all_heads:
            lanes = slice(hh * HEAD_DIM, (hh + 1) * HEAD_DIM)
            o_ref[:, lanes] = (acc_sc[hh, :, 0:HEAD_DIM] / acc_sc[hh, :, HEAD_DIM:]).astype(o_ref.dtype)


def _attention_shift(qk, v, q_col, k_col, v_col, *, fcol=None, frow=None, skip=None, bias=None):
    s = qk.shape[0]
    n_heads = N_FOX_HEADS
    hp = ATT_HP
    assert n_heads % hp == 0 and q_col % hp == 0 and k_col % hp == 0 and v_col % hp == 0
    fox = bias is None
    tq = tk = min(ATT_T if fox else ATT_T_DSA, s)
    pairs = _block_pairs(s, tq)
    qt = jnp.asarray([a for a, _ in pairs], jnp.int32)
    kt = jnp.asarray([b for _, b in pairs], jnp.int32)
    if skip is None:
        skip = jnp.zeros((n_heads, len(pairs)), jnp.int32)
    w = hp * HEAD_DIM
    in_specs = [
        pl.BlockSpec((tq, w), lambda h, p, qt, kt, sk: (qt[p], q_col // hp + h)),
        pl.BlockSpec((tk, w), lambda h, p, qt, kt, sk: (kt[p], k_col // hp + h)),
        pl.BlockSpec((tk, w), lambda h, p, qt, kt, sk: (kt[p], v_col // hp + h)),
    ]
    if fox:
        in_specs += [pl.BlockSpec((hp, tq, 1), lambda h, p, qt, kt, sk: (h, qt[p], 0)),
                     pl.BlockSpec((hp, 1, tk), lambda h, p, qt, kt, sk: (h, 0, kt[p]))]
        operands = (qk, qk, v, fcol, frow)
    else:
        in_specs += [pl.BlockSpec((tq, tk), lambda h, p, qt, kt, sk: (qt[p], kt[p]))]
        operands = (qk, qk, v, bias)
    grid_spec = pltpu.PrefetchScalarGridSpec(
        num_scalar_prefetch=3,
        grid=(n_heads // hp, len(pairs)),
        in_specs=in_specs,
        out_specs=pl.BlockSpec((tq, w), lambda h, p, qt, kt, sk: (qt[p], h)),
        scratch_shapes=[pltpu.VMEM((hp, tq, 2 * HEAD_DIM), jnp.float32)],
    )
    return pl.pallas_call(
        functools.partial(_attn_shift_kernel, fox=fox),
        grid_spec=grid_spec,
        out_shape=jax.ShapeDtypeStruct((s, n_heads * HEAD_DIM), jnp.bfloat16),
        compiler_params=_cparams(2),
        name="fox_attention_shift" if fox else "dsa_attention_shift",
    )(qt, kt, skip, *operands)


def _cast_kernel(x_ref, o_ref):
    o_ref[...] = x_ref[...].astype(o_ref.dtype)


def _cast_bf16(w):
    l, k, n = w.shape
    rows = math.gcd(CAST_ROWS, k)
    spec = pl.BlockSpec((None, rows, n), lambda i, j: (i, j, 0))
    return pl.pallas_call(
        _cast_kernel,
        grid=(l, k // rows),
        in_specs=[spec],
        out_specs=spec,
        out_shape=jax.ShapeDtypeStruct((l, k, n), jnp.bfloat16),
        compiler_params=_cparams(2),
        name="cast_bf16",
    )(w)


def _rope_angles(s, rot):
    half = rot // 2
    pos = jnp.arange(s, dtype=jnp.float32)
    inv = jnp.power(jnp.float32(ROPE_THETA), -2.0 * jnp.arange(half, dtype=jnp.float32) / rot)
    ang = pos[:, None] * inv[None, :]
    return jnp.cos(ang), jnp.sin(ang)


def _rope_tables(s, rot, period):
    half = rot // 2
    cos, sin = _rope_angles(s, rot)
    ones = jnp.ones((s, period - rot), jnp.float32)
    zeros = lambda n: jnp.zeros((s, n), jnp.float32)
    c = jnp.concatenate([cos, cos, ones], axis=1)
    s_up = jnp.concatenate([-sin, zeros(period - half)], axis=1)
    s_dn = jnp.concatenate([zeros(half), sin, zeros(period - rot)], axis=1)
    reps = LANES // period
    return tuple(jnp.tile(t, (1, reps)) for t in (c, s_up, s_dn))


def _pad_row(v, n):
    return jnp.pad(v, (0, n - v.shape[0])).reshape(1, n)


def kernel(x, p, g_ffn1, w1_gate, w1_up, w1_down, g_mix, w_in, b_f, g_qa, g_ka, g_qb, g_kb, g_ik,
           w_br_fox, w_br_dsa, w_o, g_ffn2, w2_gate, w2_up, w2_down, g_ple, w_ple_gate, w_ple_proj):
    b, s, d = x.shape
    assert b == 1
    bf16 = jnp.bfloat16
    topk = min(INDEX_TOPK_MAX, s // 4)
    idx_scale = (N_IDX_HEADS ** -0.5) * (IDX_DIM ** -0.5)
    rope_b = _rope_tables(s, HEAD_DIM // ROT_FRAC_DIV, HEAD_DIM)
    half_b = HEAD_DIM // ROT_FRAC_DIV // 2
    rope_i = _rope_tables(s, IDX_DIM // ROT_FRAC_DIV, IDX_DIM)
    half_i = IDX_DIM // ROT_FRAC_DIV // 2
    row_tab = lambda t: (t, (MM_TM, LANES), lambda i, j: (i, 0))
    res_tile = lambda h, tn=MM_TN: (h, (MM_TM, tn), lambda i, j: (i, j))
    blk = lambda cols: cols // MM_TN_WIDE
    tn_res = min(MM_TN_WIDE, d)

    w1g, w1u, w1d = _cast_bf16(w1_gate), _cast_bf16(w1_up), _cast_bf16(w1_down)
    w2g, w2u, w2d = _cast_bf16(w2_gate), _cast_bf16(w2_up), _cast_bf16(w2_down)
    wbf, wbd, wo = _cast_bf16(w_br_fox), _cast_bf16(w_br_dsa), _cast_bf16(w_o)
    wpg, wpp = _cast_bf16(w_ple_gate), _cast_bf16(w_ple_proj)
    off_fa = 3 * FOX_W
    off_b = off_fa + N_FOX_HEADS
    off_ki = off_b + 3 * DSA_W + IDX_Q_W
    off_wi = off_ki + IDX_DIM
    off_g = off_wi + N_IDX_HEADS
    w_in_bf = w_in.astype(bf16)
    w_a = w_in_bf
    w_b = w_in_bf[:, :, off_b:off_ki]
    w_g = w_in_bf[:, :, off_g:]
    assert off_fa % LANES == 0 and off_g <= (off_ki // LANES + 1) * LANES
    kw_window = off_ki // LANES
    ki_lane, wi_lane = off_ki - kw_window * LANES, off_wi - kw_window * LANES
    p_bf = p.reshape(DEPTH, s, PLE_DIM).astype(bf16)

    def ffn(h, g, wg, wu, wd, layer, tag):
        u = _rmsnorm(h, g)
        a = _matmul([u], [wg, wu], [(0, 0), (0, 1)], [], _ep_swiglu, bf16,
                    layer=layer, name=tag + "_swiglu")
        return _matmul([a], [wd], [(0, 0)], [res_tile(h, tn_res)],
                       functools.partial(_ep_residual, scale=0.5), jnp.float32,
                       layer=layer, name=tag + "_down", tn=tn_res)

    h = x.reshape(s, d)
    for i in range(DEPTH):
        h = ffn(h, g_ffn1[i], w1g, w1u, w1d, i, "ffn1")

        u = _rmsnorm(h, g_mix[i])
        gain_a = jnp.concatenate([jnp.tile(g_qa[i], N_FOX_HEADS), jnp.tile(g_ka[i], N_FOX_HEADS)])
        gain_b = jnp.concatenate([jnp.tile(g_qb[i], N_DSA_HEADS), jnp.tile(g_kb[i], N_DSA_HEADS)])
        col_vec = lambda v: (v.reshape(1, -1), (1, MM_TN_WIDE), lambda i, j: (0, j))

        qk_a = _matmul([u], [w_a], [(0, 0)], [col_vec(gain_a)],
                       functools.partial(_ep_headnorm, rope_half=0), bf16,
                       layer=i, name="proj_qk_fox", tn=MM_TN_WIDE, n=2 * FOX_W)
        v_a = _matmul([u], [w_a], [(0, 0)], [], _ep_cast, bf16,
                      layer=i, name="proj_v_fox", tn=MM_TN_WIDE, n=FOX_W, rhs_off=[blk(2 * FOX_W)])
        qk_b = _matmul([u], [w_b], [(0, 0)], [col_vec(gain_b)] + [row_tab(t) for t in rope_b],
                       functools.partial(_ep_headnorm, rope_half=half_b), bf16,
                       layer=i, name="proj_qk_dsa", tn=MM_TN_WIDE, n=2 * DSA_W)
        v_b = _matmul([u], [w_b], [(0, 0)], [], _ep_cast, bf16,
                      layer=i, name="proj_v_dsa", tn=MM_TN_WIDE, n=DSA_W, rhs_off=[blk(2 * DSA_W)])
        q_i = _matmul([u], [w_b], [(0, 0)], [row_tab(t) for t in rope_i],
                      functools.partial(_ep_rope, rope_half=half_i), bf16,
                      layer=i, name="proj_qidx", tn=MM_TN_WIDE, n=IDX_Q_W, rhs_off=[blk(3 * DSA_W)])
        gates = _matmul([u], [w_g], [(0, 0)], [], _ep_sigmoid, bf16, layer=i, name="proj_gates", tn=MM_TN_WIDE)
        z_f = _matmul([u], [w_in_bf], [(0, 0)], [], _ep_cast, jnp.float32, layer=i,
                      name="proj_forget", tn=LANES, n=LANES, rhs_off=[off_fa // LANES])
        z_kw = _matmul([u], [w_in_bf], [(0, 0)], [], _ep_cast, jnp.float32, layer=i,
                       name="proj_idx_key", tn=LANES, n=LANES, rhs_off=[kw_window])

        f_cum, ki_e, ki_o, wi_s = _small(z_f, z_kw, _pad_row(b_f[i], LANES),
                                         _pad_row(g_ik[i], LANES), *rope_i, idx_scale,
                                         ki_lane, wi_lane)

        f_t = f_cum[:, :N_FOX_HEADS].T
        skip = _fox_skip_flags(f_t, g_qa[i], g_ka[i])
        shift_a = _logit_bound2(g_qa[i], g_ka[i])
        o_a = lax.cond(
            2.0 * shift_a < FIXED_SHIFT_MAX_RANGE,
            lambda: _attention_shift(qk_a, v_a, 0, N_FOX_HEADS, 0,
                                     fcol=(f_t * LOG2E - shift_a)[:, :, None],
                                     frow=(f_t * LOG2E)[:, None, :], skip=skip),
            lambda: _attention(qk_a, v_a, 0, N_FOX_HEADS, 0, frow=f_t[:, None, :], skip=skip))

        shift_b = _logit_bound2(g_qb[i], g_kb[i])
        fixed_b = 2.0 * shift_b < FIXED_SHIFT_MAX_RANGE
        bias = _index_bias(q_i, ki_e.T, ki_o.T, wi_s, topk, jnp.where(fixed_b, -shift_b, 0.0))
        o_b = lax.cond(
            fixed_b,
            lambda: _attention_shift(qk_b, v_b, 0, N_DSA_HEADS, 0, bias=bias),
            lambda: _attention(qk_b, v_b, 0, N_DSA_HEADS, 0, bias=bias))

        n_gate_blocks = D_MODEL // tn_res
        y = _matmul([o_a, o_b], [wbf, wbd], [(0, 0), (1, 1)],
                    [(gates, (MM_TM, tn_res), lambda i, j: (i, j)),
                     (gates, (MM_TM, tn_res), lambda i, j: (i, j + n_gate_blocks))],
                    _ep_merge, bf16, layer=i, name="branch_merge", tn=tn_res)
        h = _matmul([y], [wo], [(0, 0)], [res_tile(h, tn_res)],
                    functools.partial(_ep_residual, scale=1.0), jnp.float32,
                    layer=i, name="out_proj", tn=tn_res)

        h = ffn(h, g_ffn2[i], w2g, w2u, w2d, i, "ffn2")

        u = _rmsnorm(h, g_ple[i])
        h = _matmul([u, p_bf[i]], [wpg, wpp], [(0, 0), (1, 1)], [res_tile(h, tn_res)],
                    _ep_ple, jnp.float32, layer=i, name="ple", tn=tn_res, lhs_single_buffer=True)
    return h.reshape(b, s, d)
```
